```python
import jax, jax.numpy as jnp
from jax import lax
import numpy as np

D_MODEL = 1024
BATCH = 1
SEQ = 16384
DEPTH = 2

N_A_LAYERS = DEPTH // 2
N_B_LAYERS = DEPTH - N_A_LAYERS
HEAD_DIM = 64
SB_HEADS = D_MODEL // HEAD_DIM
NSA_HEADS = D_MODEL // HEAD_DIM
NSA_KV_GROUPS = 4
NSA_HPG = NSA_HEADS // NSA_KV_GROUPS
CMP_LEN = 32
CMP_STRIDE = 16
CMP_HIDDEN = 256
SEL_LEN = 64
SEL_TOPK = 16
WINDOW = 512
D_FF = 4 * D_MODEL
ROPE_THETA = 500000.0
ROT_DIM = HEAD_DIM // 4
Q_BLOCK = 128
NORM_EPS = 1e-5
NEG = -1e30
FORCED_SCORE = 1e6

kernel_name = "yoco_stickbreaking_nsa_hybrid"


def rms_norm(x, g):
    xf = x.astype(jnp.float32)
    y = xf * lax.rsqrt(jnp.mean(xf * xf, axis=-1, keepdims=True) + NORM_EPS)
    return (y * g.astype(jnp.float32)).astype(x.dtype)


def partial_rope(x):
    s = x.shape[1]
    half = ROT_DIM // 2
    inv_freq = ROPE_THETA ** (-jnp.arange(half, dtype=jnp.float32) * 2.0 / ROT_DIM)
    ang = jnp.arange(s, dtype=jnp.float32)[:, None] * inv_freq[None, :]
    cos = jnp.cos(ang)[None, :, None, :]
    sin = jnp.sin(ang)[None, :, None, :]
    xr = x[..., :ROT_DIM].astype(jnp.float32)
    x1, x2 = xr[..., :half], xr[..., half:]
    rot = jnp.concatenate([x1 * cos - x2 * sin, x2 * cos + x1 * sin], axis=-1).astype(x.dtype)
    return jnp.concatenate([rot, x[..., ROT_DIM:]], axis=-1)


def to_qblocks(a):
    b, s = a.shape[0], a.shape[1]
    return jnp.moveaxis(a.reshape((b, s // Q_BLOCK, Q_BLOCK) + a.shape[2:]), 1, 0)


def from_qblocks(a):
    a = jnp.moveaxis(a, 0, 1)
    return a.reshape((a.shape[0], a.shape[1] * a.shape[2]) + a.shape[3:])


def sqrelu_mlp(x, w1, w2):
    h = jax.nn.relu(x @ w1)
    return (h * h) @ w2


def stick_breaking_attention(h, w_qkv, w_o):
    b, s, _ = h.shape
    n_qb = s // Q_BLOCK
    qkv = (h @ w_qkv).reshape(b, s, 3, SB_HEADS, HEAD_DIM)
    q, k, v = qkv[:, :, 0], qkv[:, :, 1], qkv[:, :, 2]
    scale = HEAD_DIM ** -0.5
    key_pos = jnp.arange(s)

    def block(args):
        qi, q_blk = args
        t = qi * Q_BLOCK + jnp.arange(Q_BLOCK)
        z = jnp.einsum('bqhd,bshd->bhqs', q_blk, k).astype(jnp.float32) * scale
        causal = key_pos[None, :] < t[:, None]
        log_1m = jnp.where(causal, jax.nn.log_sigmoid(-z), 0.0)
        excl = lax.cumsum(log_1m, axis=3, reverse=True) - log_1m
        a = jnp.where(causal, jnp.exp(jax.nn.log_sigmoid(z) + excl), 0.0)
        return jnp.einsum('bhqs,bshd->bqhd', a.astype(v.dtype), v)

    o = lax.map(block, (jnp.arange(n_qb), to_qblocks(q)))
    return from_qblocks(o).reshape(b, s, SB_HEADS * HEAD_DIM) @ w_o


def nsa_shared_kv(h, kv_norm, w_kv, cmp_pos, cmp_w1, cmp_w2):
    b, s, _ = h.shape
    hn = rms_norm(h, kv_norm)
    kv = (hn @ w_kv).reshape(b, s, 3, 2, NSA_KV_GROUPS, HEAD_DIM)
    n_cmp = (s - CMP_LEN) // CMP_STRIDE + 1
    idx = jnp.arange(n_cmp)[:, None] * CMP_STRIDE + jnp.arange(CMP_LEN)[None, :]
    blk = kv[:, :, 0][:, idx]
    blk = blk + jnp.transpose(cmp_pos, (1, 0, 2))[None, None, :, :, None, :]
    blk = jnp.transpose(blk, (0, 1, 3, 4, 2, 5)).reshape(b, n_cmp, 2, NSA_KV_GROUPS, CMP_LEN * HEAD_DIM)
    hid = jax.nn.gelu(jnp.einsum('bncgf,cfe->bncge', blk, cmp_w1))
    cmp = jnp.einsum('bncge,ced->bncgd', hid, cmp_w2)
    k_cmp, v_cmp = cmp[:, :, 0], cmp[:, :, 1]
    k_slc, v_slc = partial_rope(kv[:, :, 1, 0]), kv[:, :, 1, 1]
    k_win, v_win = partial_rope(kv[:, :, 2, 0]), kv[:, :, 2, 1]
    return (k_cmp, v_cmp, k_slc, v_slc, k_win, v_win)


def nsa_attention(h, w_q, gate_b, w_o, shared):
    k_cmp, v_cmp, k_slc, v_slc, k_win, v_win = shared
    b, s, _ = h.shape
    G, HPG, DH = NSA_KV_GROUPS, NSA_HPG, HEAD_DIM
    n_cmp = k_cmp.shape[1]
    n_sel = s // SEL_LEN
    n_qb = s // Q_BLOCK
    topk = min(SEL_TOPK, n_sel)
    scale = DH ** -0.5

    proj = h @ w_q
    q = proj[..., :NSA_HEADS * DH].reshape(b, s, NSA_HEADS, DH)
    gates = jax.nn.sigmoid((proj[..., NSA_HEADS * DH:] + gate_b).astype(jnp.float32)).reshape(b, s, 3, NSA_HEADS)
    q_rot = partial_rope(q)

    cmp_start = jnp.arange(n_cmp) * CMP_STRIDE
    cmp_end = cmp_start + CMP_LEN - 1
    sel_ids = jnp.arange(n_sel)
    sel_start = sel_ids * SEL_LEN
    overlap = ((cmp_start[:, None] < sel_start[None, :] + SEL_LEN)
               & (cmp_start[:, None] + CMP_LEN > sel_start[None, :])).astype(jnp.float32)
    k_sel_blocks = jnp.transpose(k_slc.reshape(b, n_sel, SEL_LEN, G, DH), (0, 3, 1, 2, 4))
    v_sel_blocks = jnp.transpose(v_slc.reshape(b, n_sel, SEL_LEN, G, DH), (0, 3, 1, 2, 4))
    pad = ((0, 0), (WINDOW, 0), (0, 0), (0, 0))
    k_win_p = jnp.pad(k_win, pad)
    v_win_p = jnp.pad(v_win, pad)
    gather = jax.vmap(jax.vmap(lambda blocks, ids: blocks[ids]))

    def block(args):
        qi, q_blk, qr_blk = args
        q0 = qi * Q_BLOCK
        t = q0 + jnp.arange(Q_BLOCK)
        qg = q_blk.reshape(b, Q_BLOCK, G, HPG, DH)
        qrg = qr_blk.reshape(b, Q_BLOCK, G, HPG, DH)

        sc = jnp.einsum('bqghd,bngd->bghqn', qg, k_cmp).astype(jnp.float32) * scale
        valid_c = cmp_end[None, :] <= t[:, None]
        p_cmp = jax.nn.softmax(jnp.where(valid_c, sc, NEG), axis=-1)
        p_cmp = p_cmp * jnp.any(valid_c, axis=-1)[:, None]
        o_cmp = jnp.einsum('bghqn,bngd->bqghd', p_cmp.astype(v_cmp.dtype), v_cmp)

        imp = jnp.einsum('bghqn,nm->bgqm', p_cmp, overlap)
        blk_t = t // SEL_LEN
        forced = (sel_ids[None, :] == 0) | (sel_ids[None, :] == blk_t[:, None]) | (sel_ids[None, :] == blk_t[:, None] - 1)
        causal_b = sel_ids[None, :] <= blk_t[:, None]
        imp = jnp.where(causal_b, jnp.where(forced, FORCED_SCORE, imp), -jnp.inf)
        _, top_idx = lax.top_k(imp, topk)
        k_sel = gather(k_sel_blocks, top_idx)
        v_sel = gather(v_sel_blocks, top_idx)
        ss = jnp.einsum('bqghd,bgqkld->bghqkl', qrg, k_sel).astype(jnp.float32) * scale
        key_pos = top_idx[..., None] * SEL_LEN + jnp.arange(SEL_LEN)
        valid_s = key_pos <= t[None, None, :, None, None]
        ss = jnp.where(valid_s[:, :, None], ss, NEG)
        p_sel = jax.nn.softmax(ss.reshape(b, G, HPG, Q_BLOCK, topk * SEL_LEN), axis=-1).reshape(ss.shape)
        o_sel = jnp.einsum('bghqkl,bgqkld->bqghd', p_sel.astype(v_sel.dtype), v_sel)

        k_w = lax.dynamic_slice_in_dim(k_win_p, q0, WINDOW + Q_BLOCK, axis=1)
        v_w = lax.dynamic_slice_in_dim(v_win_p, q0, WINDOW + Q_BLOCK, axis=1)
        sw = jnp.einsum('bqghd,bsgd->bghqs', qrg, k_w).astype(jnp.float32) * scale
        kpos = q0 - WINDOW + jnp.arange(WINDOW + Q_BLOCK)
        diff = t[:, None] - kpos[None, :]
        valid_w = (diff >= 0) & (diff < WINDOW) & (kpos[None, :] >= 0)
        p_w = jax.nn.softmax(jnp.where(valid_w, sw, NEG), axis=-1)
        o_win = jnp.einsum('bghqs,bsgd->bqghd', p_w.astype(v_w.dtype), v_w)

        return jnp.stack([o_cmp, o_sel, o_win], axis=2).reshape(b, Q_BLOCK, 3, NSA_HEADS, DH)

    o = from_qblocks(lax.map(block, (jnp.arange(n_qb), to_qblocks(q), to_qblocks(q_rot))))
    out = jnp.einsum('bsch,bschd->bshd', gates.astype(o.dtype), o).reshape(b, s, NSA_HEADS * DH)
    return out @ w_o


def setup_inputs(seed: int = 0) -> dict:
    key = jax.random.key(seed)
    ks = jax.random.split(key, 16)
    f32 = jnp.float32

    def w(k, shape, fan_in):
        return jax.random.normal(k, shape, f32) * fan_in ** -0.5

    x = jax.random.normal(ks[0], (BATCH, SEQ, D_MODEL), f32)
    norm_gain = 1.0 + 0.02 * jax.random.normal(ks[1], (DEPTH, 2, D_MODEL), f32)
    sb_w_qkv = w(ks[2], (N_A_LAYERS, D_MODEL, 3 * SB_HEADS * HEAD_DIM), D_MODEL)
    sb_w_o = w(ks[3], (N_A_LAYERS, SB_HEADS * HEAD_DIM, D_MODEL), SB_HEADS * HEAD_DIM)
    kv_norm = 1.0 + 0.02 * jax.random.normal(ks[4], (D_MODEL,), f32)
    nsa_w_kv = w(ks[5], (D_MODEL, 3 * 2 * NSA_KV_GROUPS * HEAD_DIM), D_MODEL)
    cmp_pos = 0.1 * jax.random.normal(ks[6], (2, CMP_LEN, HEAD_DIM), f32)
    cmp_w1 = w(ks[7], (2, CMP_LEN * HEAD_DIM, CMP_HIDDEN), CMP_LEN * HEAD_DIM)
    cmp_w2 = w(ks[8], (2, CMP_HIDDEN, HEAD_DIM), CMP_HIDDEN)
    nsa_w_q = w(ks[9], (N_B_LAYERS, D_MODEL, NSA_HEADS * HEAD_DIM + 3 * NSA_HEADS), D_MODEL)
    nsa_gate_b = 0.1 * jax.random.normal(ks[10], (N_B_LAYERS, 3 * NSA_HEADS), f32)
    nsa_w_o = w(ks[11], (N_B_LAYERS, NSA_HEADS * HEAD_DIM, D_MODEL), NSA_HEADS * HEAD_DIM)
    mlp_w1 = w(ks[12], (DEPTH, D_MODEL, D_FF), D_MODEL)
    mlp_w2 = w(ks[13], (DEPTH, D_FF, D_MODEL), D_FF)
    final_norm = 1.0 + 0.02 * jax.random.normal(ks[14], (D_MODEL,), f32)
    return {"x": x, "norm_gain": norm_gain, "sb_w_qkv": sb_w_qkv, "sb_w_o": sb_w_o,
            "kv_norm": kv_norm, "nsa_w_kv": nsa_w_kv, "cmp_pos": cmp_pos, "cmp_w1": cmp_w1,
            "cmp_w2": cmp_w2, "nsa_w_q": nsa_w_q, "nsa_gate_b": nsa_gate_b, "nsa_w_o": nsa_w_o,
            "mlp_w1": mlp_w1, "mlp_w2": mlp_w2, "final_norm": final_norm}


def reference(x, norm_gain, sb_w_qkv, sb_w_o, kv_norm, nsa_w_kv, cmp_pos, cmp_w1, cmp_w2,
              nsa_w_q, nsa_gate_b, nsa_w_o, mlp_w1, mlp_w2, final_norm):
    h = x
    shared = None
    for layer in range(DEPTH):
        hn = rms_norm(h, norm_gain[layer, 0])
        if layer < N_A_LAYERS:
            h = h + stick_breaking_attention(hn, sb_w_qkv[layer], sb_w_o[layer])
        else:
            j = layer - N_A_LAYERS
            h = h + nsa_attention(hn, nsa_w_q[j], nsa_gate_b[j], nsa_w_o[j], shared)
        h = h + sqrelu_mlp(rms_norm(h, norm_gain[layer, 1]), mlp_w1[layer], mlp_w2[layer])
        if layer == N_A_LAYERS - 1:
            shared = nsa_shared_kv(h, kv_norm, nsa_w_kv, cmp_pos, cmp_w1, cmp_w2)
    return rms_norm(h, final_norm)
```

```python
import functools

import jax
import jax.numpy as jnp
from jax import lax
from jax.experimental import pallas as pl
from jax.experimental.pallas import tpu as pltpu

HEAD_DIM = 64
N_HEADS = 16
KV_GROUPS = 4
HEADS_PER_GROUP = N_HEADS // KV_GROUPS
CMP_LEN = 32
CMP_STRIDE = 16
CMP_HIDDEN = 256
SEL_LEN = 64
SEL_TOPK = 16
WINDOW = 512
ROPE_THETA = 500000.0
ROT_DIM = HEAD_DIM // 4
Q_TILE = 128
NORM_EPS = 1e-5
NEG = -1e30
FORCED_SCORE = 1e6
SB_UNDERFLOW = -110.0
VMEM_LIMIT = 56 * 1024 * 1024

BF16 = jnp.bfloat16
F32 = jnp.float32


def _params(semantics, vmem=VMEM_LIMIT):
    return pltpu.CompilerParams(dimension_semantics=semantics, vmem_limit_bytes=vmem)


def _rms(x, g):
    return x * lax.rsqrt(jnp.mean(x * x, axis=-1, keepdims=True) + NORM_EPS) * g


def _dot(a, b):
    return jnp.dot(a, b, preferred_element_type=F32)


def _dot_nt(a, b):
    return lax.dot_general(a, b, (((1,), (1,)), ((), ())), preferred_element_type=F32)


def _rope128(x, c, s1, s2):
    return x * c + pltpu.roll(x, 128 - ROT_DIM // 2, axis=1) * s1 + pltpu.roll(x, ROT_DIM // 2, axis=1) * s2


def _split3(x):
    hi = x.astype(BF16)
    r = x - hi.astype(F32)
    mid = r.astype(BF16)
    lo = (r - mid.astype(F32)).astype(BF16)
    return hi, mid, lo


def _qkv_kernel(x_ref, g_ref, w_ref, o_ref):
    xn = _rms(x_ref[...], g_ref[...]).astype(BF16)
    acc = _dot(xn, w_ref[...])
    d = x_ref.shape[1]
    o_ref[:, :d] = (acc[:, :d] * (HEAD_DIM ** -0.5)).astype(BF16)
    o_ref[:, d:] = acc[:, d:].astype(BF16)


def _qkv_proj(x, gain, w):
    s, d = x.shape
    n = w.shape[1]
    tm = 256
    return pl.pallas_call(
        _qkv_kernel,
        grid=(s // tm,),
        in_specs=[pl.BlockSpec((tm, d), lambda i: (i, 0)),
                  pl.BlockSpec((1, d), lambda i: (0, 0)),
                  pl.BlockSpec((d, n), lambda i: (0, 0))],
        out_specs=pl.BlockSpec((tm, n), lambda i: (i, 0)),
        out_shape=jax.ShapeDtypeStruct((s, n), BF16),
        compiler_params=_params(("arbitrary",)),
        name="qkv_proj",
    )(x, gain.reshape(1, d), w)


def _sb_block(q, k, v, tri, carry, mask):
    z = _dot_nt(q, k)
    sp = jnp.maximum(z, 0.0) + jnp.log1p(jnp.exp(-jnp.abs(z)))
    log_1m = -sp
    if mask is not None:
        log_1m = jnp.where(mask, log_1m, 0.0)
    hi = log_1m.astype(BF16)
    lo = (log_1m - hi.astype(F32)).astype(BF16)
    excl = _dot(hi, tri) + _dot(lo, tri)
    a = jnp.exp((z - sp) + (excl + carry))
    if mask is not None:
        a = jnp.where(mask, a, 0.0)
    o = _dot(a.astype(BF16), v)
    return o, carry + jnp.sum(log_1m, axis=1, keepdims=True)


def _sb_kernel(q_ref, k_ref, v_ref, o_ref):
    i = pl.program_id(1)
    tq = q_ref.shape[0]
    row = lax.broadcasted_iota(jnp.int32, (tq, tq), 0)
    col = lax.broadcasted_iota(jnp.int32, (tq, tq), 1)
    tri = (row > col).astype(BF16)
    causal = col < row
    q = q_ref[...]
    q0, q1 = q[:, :HEAD_DIM], q[:, HEAD_DIM:]

    def heads(kb, carry0, carry1, mask):
        start = pl.multiple_of(kb * tq, tq)
        k = k_ref[pl.ds(start, tq), :]
        v = v_ref[pl.ds(start, tq), :]
        o0, c0 = _sb_block(q0, k[:, :HEAD_DIM], v[:, :HEAD_DIM], tri, carry0, mask)
        o1, c1 = _sb_block(q1, k[:, HEAD_DIM:], v[:, HEAD_DIM:], tri, carry1, mask)
        return o0, o1, c0, c1

    zero = jnp.zeros((tq, 1), F32)
    o0, o1, c0, c1 = heads(i, zero, zero, causal)

    def cond(st):
        kb, mx = st[0], st[1]
        return jnp.logical_and(kb >= 0, mx > SB_UNDERFLOW)

    def body(st):
        kb, _, a0, a1, c0, c1 = st
        o0, o1, n0, n1 = heads(kb, c0, c1, None)
        mx = jnp.maximum(jnp.max(n0), jnp.max(n1))
        return kb - 1, mx, a0 + o0, a1 + o1, n0, n1

    mx = jnp.maximum(jnp.max(c0), jnp.max(c1))
    st = lax.while_loop(cond, body, (i - 1, mx, o0, o1, c0, c1))
    o_ref[:, :HEAD_DIM] = st[2].astype(BF16)
    o_ref[:, HEAD_DIM:] = st[3].astype(BF16)


def _sb_attention(qkv):
    s = qkv.shape[0]
    d = N_HEADS * HEAD_DIM
    npair = N_HEADS // 2
    w = 2 * HEAD_DIM
    return pl.pallas_call(
        _sb_kernel,
        grid=(npair, s // Q_TILE),
        in_specs=[pl.BlockSpec((Q_TILE, w), lambda h, i: (i, h)),
                  pl.BlockSpec((s, w), lambda h, i: (0, npair + h)),
                  pl.BlockSpec((s, w), lambda h, i: (0, 2 * npair + h))],
        out_specs=pl.BlockSpec((Q_TILE, w), lambda h, i: (i, h)),
        out_shape=jax.ShapeDtypeStruct((s, d), BF16),
        compiler_params=_params(("arbitrary", "arbitrary")),
        name="sb_attention",
    )(qkv, qkv, qkv)


def _mlp_kernel(n_o, final, *refs):
    o_refs = refs[:n_o]
    res_ref, wo_ref, g_ref, w1_ref, w2_ref = refs[n_o:n_o + 5]
    pos = n_o + 5
    if final:
        fg_ref = refs[pos]
        pos += 1
    out_ref, h_ref, xn_ref, acc_ref = refs[pos:pos + 4]
    j = pl.program_id(1)

    @pl.when(j == 0)
    def _():
        o = o_refs[0][...]
        if n_o > 1:
            of = o.astype(F32)
            for r in o_refs[1:]:
                of = of + r[...].astype(F32)
            o = of.astype(BF16)
        h = res_ref[...] + _dot(o, wo_ref[...])
        h_ref[...] = h
        xn_ref[...] = _rms(h, g_ref[...]).astype(BF16)
        acc_ref[...] = jnp.zeros_like(acc_ref)

    a = jnp.maximum(_dot(xn_ref[...], w1_ref[...]), 0.0)
    acc_ref[...] += _dot((a * a).astype(BF16), w2_ref[...])

    @pl.when(j == pl.num_programs(1) - 1)
    def _():
        y = h_ref[...] + acc_ref[...]
        if final:
            y = _rms(y, fg_ref[...])
        out_ref[...] = y


def _attn_out_mlp(o_list, resid, wo, gain, w1, w2, final_gain=None):
    s, d = resid.shape
    f = w1.shape[1]
    tm, tf = 512, 512
    n_o = len(o_list)
    final = final_gain is not None
    row = lambda i, j: (i, 0)
    fixed = lambda i, j: (0, 0)
    in_specs = [pl.BlockSpec((tm, d), row) for _ in o_list]
    in_specs += [pl.BlockSpec((tm, d), row), pl.BlockSpec((d, d), fixed), pl.BlockSpec((1, d), fixed),
                 pl.BlockSpec((d, tf), lambda i, j: (0, j)), pl.BlockSpec((tf, d), lambda i, j: (j, 0))]
    args = list(o_list) + [resid, wo, gain.reshape(1, d), w1, w2]
    if final:
        in_specs.append(pl.BlockSpec((1, d), fixed))
        args.append(final_gain.reshape(1, d))
    return pl.pallas_call(
        functools.partial(_mlp_kernel, n_o, final),
        grid=(s // tm, f // tf),
        in_specs=in_specs,
        out_specs=pl.BlockSpec((tm, d), row),
        out_shape=jax.ShapeDtypeStruct((s, d), F32),
        scratch_shapes=[pltpu.VMEM((tm, d), F32), pltpu.VMEM((tm, d), BF16), pltpu.VMEM((tm, d), F32)],
        compiler_params=_params(("arbitrary", "arbitrary")),
        name="attn_out_mlp",
    )(*args)


def _kv_kernel(x_ref, g_ref, w_ref, c_ref, s1_ref, s2_ref, cmp_ref, ks_ref, vs_ref, kw_ref, vw_ref):
    xn = _rms(x_ref[...], g_ref[...]).astype(BF16)
    acc = _dot(xn, w_ref[...])
    gw = KV_GROUPS * HEAD_DIM
    cmp_ref[...] = acc[:, :2 * gw]
    c, s1, s2 = c_ref[...], s1_ref[...], s2_ref[...]

    def put(ref, x, rope):
        for p in range(gw // 128):
            xp = x[:, 128 * p:128 * (p + 1)]
            if rope:
                xp = _rope128(xp, c, s1, s2)
            ref[2 * p] = xp[:, :HEAD_DIM].astype(BF16)
            ref[2 * p + 1] = xp[:, HEAD_DIM:].astype(BF16)

    put(ks_ref, acc[:, 2 * gw:3 * gw], True)
    put(vs_ref, acc[:, 3 * gw:4 * gw], False)
    put(kw_ref, acc[:, 4 * gw:5 * gw], True)
    put(vw_ref, acc[:, 5 * gw:6 * gw], False)


def _kv_proj(h, gain, w, rope_tabs):
    s, d = h.shape
    n = w.shape[1]
    gw = KV_GROUPS * HEAD_DIM
    tm = 256
    row = lambda i: (i, 0)
    fixed = lambda i: (0, 0)
    grp = pl.BlockSpec((KV_GROUPS, tm, HEAD_DIM), lambda i: (0, i, 0))
    grp_shape = jax.ShapeDtypeStruct((KV_GROUPS, s, HEAD_DIM), BF16)
    return pl.pallas_call(
        _kv_kernel,
        grid=(s // tm,),
        in_specs=[pl.BlockSpec((tm, d), row), pl.BlockSpec((1, d), fixed), pl.BlockSpec((d, n), fixed),
                  pl.BlockSpec((tm, 128), row), pl.BlockSpec((tm, 128), row), pl.BlockSpec((tm, 128), row)],
        out_specs=[pl.BlockSpec((tm, 2 * gw), row), grp, grp, grp, grp],
        out_shape=[jax.ShapeDtypeStruct((s, 2 * gw), F32), grp_shape, grp_shape, grp_shape, grp_shape],
        compiler_params=_params(("arbitrary",)),
        name="kv_proj",
    )(h, gain.reshape(1, d), w, *rope_tabs)


def _cmp_kernel(n_cmp, x_ref, pos_ref, w1_ref, w2_ref, o_ref):
    nch = x_ref.shape[0] // CMP_STRIDE
    half = CMP_STRIDE * HEAD_DIM
    w1 = w1_ref[...]
    bias = _dot(pos_ref[...].astype(BF16), w1)[0:1]
    acc = [[jnp.zeros((nch, CMP_HIDDEN), F32) for _ in range(2)] for _ in range(2)]
    for l in range(CMP_STRIDE):
        y = x_ref[pl.ds(l, nch, stride=CMP_STRIDE), :].astype(BF16)
        wa = w1[HEAD_DIM * l:HEAD_DIM * (l + 1)]
        wb = w1[half + HEAD_DIM * l:half + HEAD_DIM * (l + 1)]
        for gg in range(2):
            yg = y[:, HEAD_DIM * gg:HEAD_DIM * (gg + 1)]
            acc[gg][0] = acc[gg][0] + _dot(yg, wa)
            acc[gg][1] = acc[gg][1] + _dot(yg, wb)
    live = lax.broadcasted_iota(jnp.int32, (nch, HEAD_DIM), 0) < n_cmp
    for gg in range(2):
        hid = acc[gg][0] + pltpu.roll(acc[gg][1], nch - 1, axis=0) + bias
        hid = jax.nn.gelu(hid, approximate=True)
        out = _dot(hid.astype(BF16), w2_ref[...])
        o_ref[gg] = jnp.where(live, out, 0.0).astype(BF16)


def _cmp_mlp(cmp_kv, pos_flat, w1, w2, n_cmp):
    s = cmp_kv.shape[0]
    nch = s // CMP_STRIDE
    feat = CMP_LEN * HEAD_DIM
    return pl.pallas_call(
        functools.partial(_cmp_kernel, n_cmp),
        grid=(2, KV_GROUPS // 2),
        in_specs=[pl.BlockSpec((s, 2 * HEAD_DIM), lambda c, p: (0, c * (KV_GROUPS // 2) + p)),
                  pl.BlockSpec((None, 8, feat), lambda c, p: (c, 0, 0)),
                  pl.BlockSpec((None, feat, CMP_HIDDEN), lambda c, p: (c, 0, 0)),
                  pl.BlockSpec((None, CMP_HIDDEN, HEAD_DIM), lambda c, p: (c, 0, 0))],
        out_specs=pl.BlockSpec((None, 2, nch, HEAD_DIM), lambda c, p: (c, p, 0, 0)),
        out_shape=jax.ShapeDtypeStruct((2, KV_GROUPS, nch, HEAD_DIM), BF16),
        compiler_params=_params(("arbitrary", "arbitrary")),
        name="cmp_mlp",
    )(cmp_kv, pos_flat, w1, w2)


def _q_kernel(x_ref, g_ref, w_ref, b_ref, c_ref, s1_ref, s2_ref, q_ref, qr_ref, gate_ref):
    xn = _rms(x_ref[...], g_ref[...]).astype(BF16)
    acc = _dot(xn, w_ref[...])
    d = q_ref.shape[1]
    c, s1, s2 = c_ref[...], s1_ref[...], s2_ref[...]
    for p in range(d // 128):
        qp = acc[:, 128 * p:128 * (p + 1)] * (HEAD_DIM ** -0.5)
        q_ref[:, 128 * p:128 * (p + 1)] = qp.astype(BF16)
        qr_ref[:, 128 * p:128 * (p + 1)] = _rope128(qp, c, s1, s2).astype(BF16)
    gate_ref[...] = 1.0 / (1.0 + jnp.exp(-(acc[:, d:] + b_ref[...])))


def _q_proj(h, gain, w, gate_b, rope_tabs):
    s, d = h.shape
    n = w.shape[1]
    tm = 256
    row = lambda i: (i, 0)
    fixed = lambda i: (0, 0)
    return pl.pallas_call(
        _q_kernel,
        grid=(s // tm,),
        in_specs=[pl.BlockSpec((tm, d), row), pl.BlockSpec((1, d), fixed), pl.BlockSpec((d, n), fixed),
                  pl.BlockSpec((1, 128), fixed),
                  pl.BlockSpec((tm, 128), row), pl.BlockSpec((tm, 128), row), pl.BlockSpec((tm, 128), row)],
        out_specs=[pl.BlockSpec((tm, d), row), pl.BlockSpec((tm, d), row), pl.BlockSpec((tm, 128), row)],
        out_shape=[jax.ShapeDtypeStruct((s, d), BF16), jax.ShapeDtypeStruct((s, d), BF16),
                   jax.ShapeDtypeStruct((s, 128), F32)],
        compiler_params=_params(("arbitrary",)),
        name="nsa_q_proj",
    )(h, gain.reshape(1, d), w, gate_b, *rope_tabs)


def _stack_heads(q):
    return jnp.concatenate([q[:, HEAD_DIM * h:HEAD_DIM * (h + 1)] for h in range(HEADS_PER_GROUP)], axis=0)


def _gate_cols(gates, branch, g):
    lane = lax.broadcasted_iota(jnp.int32, gates.shape, 1)
    cols = []
    for h in range(HEADS_PER_GROUP):
        target = branch * N_HEADS + g * HEADS_PER_GROUP + h
        cols.append(jnp.sum(jnp.where(lane == target, gates, 0.0), axis=1, keepdims=True))
    return cols


def _write_heads(o_ref, o, scale_cols):
    tq = o_ref.shape[0]
    for h in range(HEADS_PER_GROUP):
        o_ref[:, HEAD_DIM * h:HEAD_DIM * (h + 1)] = (o[tq * h:tq * (h + 1)] * scale_cols[h]).astype(BF16)


def _cmp_topk_kernel(q_ref, k_ref, v_ref, ov_ref, gate_ref, o_ref, bias_ref):
    g = pl.program_id(0)
    i = pl.program_id(1)
    tq = q_ref.shape[0]
    nc = k_ref.shape[0]
    n_sel = ov_ref.shape[1]
    hp = HEADS_PER_GROUP
    q4 = _stack_heads(q_ref[...])
    sc = _dot_nt(q4, k_ref[...]).reshape(hp, tq, nc)
    t = i * tq + lax.broadcasted_iota(jnp.int32, (tq, nc), 0)
    n = lax.broadcasted_iota(jnp.int32, (tq, nc), 1)
    valid = n * CMP_STRIDE + (CMP_LEN - 1) <= t
    sc = jnp.where(valid[None], sc, NEG)
    e = jnp.exp(sc - jnp.max(sc, axis=-1, keepdims=True))
    p = e * (1.0 / jnp.sum(e, axis=-1, keepdims=True))
    p = jnp.where((t >= CMP_LEN - 1)[None], p, 0.0)
    o = _dot(p.reshape(hp * tq, nc).astype(BF16), v_ref[...])
    gates = _gate_cols(gate_ref[...], 0, g)
    _write_heads(o_ref, o, gates)

    psum = p[0] + p[1] + p[2] + p[3]
    ov = ov_ref[...]
    imp = sum(_dot(term, ov) for term in _split3(psum))
    ts = i * tq + lax.broadcasted_iota(jnp.int32, (tq, n_sel), 0)
    m = lax.broadcasted_iota(jnp.int32, (tq, n_sel), 1)
    blk_t = ts // SEL_LEN
    forced = (m == 0) | (m == blk_t) | (m == blk_t - 1)
    causal = m <= blk_t
    imp = jnp.where(causal, jnp.where(forced, FORCED_SCORE, imp), -jnp.inf)

    def pick(_, st):
        v, sel = st
        mx = jnp.max(v, axis=1, keepdims=True)
        idx = jnp.min(jnp.where(v == mx, m, n_sel), axis=1, keepdims=True)
        hit = m == idx
        return jnp.where(hit, -jnp.inf, v), jnp.where(hit, 1.0, sel)

    _, sel = lax.fori_loop(0, min(SEL_TOPK, n_sel), pick, (imp, jnp.zeros((tq, n_sel), F32)))
    chosen = jnp.logical_and(sel > 0.5, causal)
    bias_ref[...] = jnp.where(chosen, 0.0, NEG).astype(BF16)


def _nsa_cmp_topk(q, k_cmp, v_cmp, overlap, gates):
    s, d = q.shape
    nc = k_cmp.shape[1]
    n_sel = overlap.shape[1]
    gw = HEADS_PER_GROUP * HEAD_DIM
    kv = pl.BlockSpec((None, nc, HEAD_DIM), lambda g, i: (g, 0, 0))
    return pl.pallas_call(
        _cmp_topk_kernel,
        grid=(KV_GROUPS, s // Q_TILE),
        in_specs=[pl.BlockSpec((Q_TILE, gw), lambda g, i: (i, g)), kv, kv,
                  pl.BlockSpec((nc, n_sel), lambda g, i: (0, 0)),
                  pl.BlockSpec((Q_TILE, 128), lambda g, i: (i, 0))],
        out_specs=[pl.BlockSpec((Q_TILE, gw), lambda g, i: (i, g)),
                   pl.BlockSpec((None, Q_TILE, n_sel), lambda g, i: (g, i, 0))],
        out_shape=[jax.ShapeDtypeStruct((s, d), BF16), jax.ShapeDtypeStruct((KV_GROUPS, s, n_sel), BF16)],
        compiler_params=_params(("arbitrary", "arbitrary")),
        name="nsa_cmp_topk",
    )(q, k_cmp, v_cmp, overlap, gates)


def _sel_kernel(tk, q_ref, k_ref, v_ref, bias_ref, gate_ref, o_ref, m_ref, l_ref, acc_ref):
    g = pl.program_id(0)
    i = pl.program_id(1)
    tq = q_ref.shape[0]
    n_sel = bias_ref.shape[1]
    hp = HEADS_PER_GROUP
    q4 = _stack_heads(q_ref[...])
    selbias = bias_ref[...]
    blk_row = lax.broadcasted_iota(jnp.int32, (n_sel, tk), 0)
    blk_col = lax.broadcasted_iota(jnp.int32, (n_sel, tk), 1) // SEL_LEN
    m_ref[...] = jnp.full_like(m_ref, NEG)
    l_ref[...] = jnp.zeros_like(l_ref)
    acc_ref[...] = jnp.zeros_like(acc_ref)

    def tile(kt, causal_mask):
        start = pl.multiple_of(kt * tk, tk)
        k = k_ref[pl.ds(start, tk), :]
        v = v_ref[pl.ds(start, tk), :]
        expand = (blk_row - kt * (tk // SEL_LEN) == blk_col).astype(BF16)
        bias = _dot(selbias, expand)
        s = _dot_nt(q4, k).reshape(hp, tq, tk) + bias[None]
        if causal_mask:
            t = i * tq + lax.broadcasted_iota(jnp.int32, (tq, tk), 0)
            kpos = start + lax.broadcasted_iota(jnp.int32, (tq, tk), 1)
            s = jnp.where((kpos <= t)[None], s, NEG)
        s = s.reshape(hp * tq, tk)
        m_old = m_ref[...]
        m_new = jnp.maximum(m_old, jnp.max(s, axis=1, keepdims=True))
        alpha = jnp.exp(m_old - m_new)
        p = jnp.exp(s - m_new)
        l_ref[...] = alpha * l_ref[...] + jnp.sum(p, axis=1, keepdims=True)
        acc_ref[...] = alpha * acc_ref[...] + _dot(p.astype(BF16), v)
        m_ref[...] = m_new

    last = ((i + 1) * tq - 1) // tk

    def body(kt, carry):
        tile(kt, False)
        return carry

    lax.fori_loop(0, last, body, 0)
    tile(last, True)
    gates = _gate_cols(gate_ref[...], 1, g)
    inv = 1.0 / l_ref[...]
    scale = [gates[h] * inv[tq * h:tq * (h + 1)] for h in range(hp)]
    _write_heads(o_ref, acc_ref[...], scale)


def _nsa_sel(q_rot, k_slc, v_slc, selbias, gates):
    s, d = q_rot.shape
    n_sel = selbias.shape[2]
    gw = HEADS_PER_GROUP * HEAD_DIM
    tk = 512
    kv = pl.BlockSpec((None, s, HEAD_DIM), lambda g, i: (g, 0, 0))
    rows = HEADS_PER_GROUP * Q_TILE
    return pl.pallas_call(
        functools.partial(_sel_kernel, tk),
        grid=(KV_GROUPS, s // Q_TILE),
        in_specs=[pl.BlockSpec((Q_TILE, gw), lambda g, i: (i, g)), kv, kv,
                  pl.BlockSpec((None, Q_TILE, n_sel), lambda g, i: (g, i, 0)),
                  pl.BlockSpec((Q_TILE, 128), lambda g, i: (i, 0))],
        out_specs=pl.BlockSpec((Q_TILE, gw), lambda g, i: (i, g)),
        out_shape=jax.ShapeDtypeStruct((s, d), BF16),
        scratch_shapes=[pltpu.VMEM((rows, 1), F32), pltpu.VMEM((rows, 1), F32), pltpu.VMEM((rows, HEAD_DIM), F32)],
        compiler_params=_params(("arbitrary", "arbitrary")),
        name="nsa_sel",
    )(q_rot, k_slc, v_slc, selbias, gates)


def _win_kernel(q_ref, k_ref, v_ref, gate_ref, o_ref):
    g = pl.program_id(0)
    i = pl.program_id(1)
    tq = q_ref.shape[0]
    hp = HEADS_PER_GROUP
    span = WINDOW + tq
    q4 = _stack_heads(q_ref[...])
    start = pl.multiple_of(jnp.maximum(i * tq - WINDOW, 0), tq)
    k = k_ref[pl.ds(start, span), :]
    v = v_ref[pl.ds(start, span), :]
    s = _dot_nt(q4, k).reshape(hp, tq, span)
    t = i * tq + lax.broadcasted_iota(jnp.int32, (tq, span), 0)
    kpos = start + lax.broadcasted_iota(jnp.int32, (tq, span), 1)
    diff = t - kpos
    valid = jnp.logical_and(diff >= 0, diff < WINDOW)
    s = jnp.where(valid[None], s, NEG)
    e = jnp.exp(s - jnp.max(s, axis=-1, keepdims=True))
    inv = 1.0 / jnp.sum(e, axis=-1, keepdims=True)
    o = _dot(e.reshape(hp * tq, span).astype(BF16), v)
    gates = _gate_cols(gate_ref[...], 2, g)
    _write_heads(o_ref, o, [gates[h] * inv[h] for h in range(hp)])


def _nsa_win(q_rot, k_win, v_win, gates):
    s, d = q_rot.shape
    gw = HEADS_PER_GROUP * HEAD_DIM
    kv = pl.BlockSpec((None, s, HEAD_DIM), lambda g, i: (g, 0, 0))
    return pl.pallas_call(
        _win_kernel,
        grid=(KV_GROUPS, s // Q_TILE),
        in_specs=[pl.BlockSpec((Q_TILE, gw), lambda g, i: (i, g)), kv, kv,
                  pl.BlockSpec((Q_TILE, 128), lambda g, i: (i, 0))],
        out_specs=pl.BlockSpec((Q_TILE, gw), lambda g, i: (i, g)),
        out_shape=jax.ShapeDtypeStruct((s, d), BF16),
        compiler_params=_params(("arbitrary", "arbitrary")),
        name="nsa_win",
    )(q_rot, k_win, v_win, gates)


def _rope_tables(s):
    half = ROT_DIM // 2
    inv_freq = ROPE_THETA ** (-jnp.arange(half, dtype=F32) * 2.0 / ROT_DIM)
    ang = jnp.arange(s, dtype=F32)[:, None] * inv_freq[None, :]
    cos, sin = jnp.cos(ang), jnp.sin(ang)
    rest = HEAD_DIM - ROT_DIM
    c = jnp.concatenate([cos, cos, jnp.ones((s, rest), F32)], axis=1)
    s1 = jnp.concatenate([-sin, jnp.zeros((s, half + rest), F32)], axis=1)
    s2 = jnp.concatenate([jnp.zeros((s, half), F32), sin, jnp.zeros((s, rest), F32)], axis=1)
    return tuple(jnp.tile(a, (1, 2)) for a in (c, s1, s2))


def _overlap_matrix(nch, n_sel):
    cmp_start = jnp.arange(nch)[:, None] * CMP_STRIDE
    sel_start = jnp.arange(n_sel)[None, :] * SEL_LEN
    return ((cmp_start < sel_start + SEL_LEN) & (cmp_start + CMP_LEN > sel_start)).astype(BF16)


def kernel(x, norm_gain, sb_w_qkv, sb_w_o, kv_norm, nsa_w_kv, cmp_pos, cmp_w1, cmp_w2,
           nsa_w_q, nsa_gate_b, nsa_w_o, mlp_w1, mlp_w2, final_norm):
    b, s, d = x.shape
    assert b == 1 and d == N_HEADS * HEAD_DIM
    assert s % 512 == 0 and s >= WINDOW + Q_TILE
    n_cmp = (s - CMP_LEN) // CMP_STRIDE + 1
    nch = s // CMP_STRIDE
    n_sel = s // SEL_LEN
    h0 = x[0]
    rope_tabs = _rope_tables(s)

    qkv = _qkv_proj(h0, norm_gain[0, 0], sb_w_qkv[0].astype(BF16))
    o_sb = _sb_attention(qkv)
    h1 = _attn_out_mlp([o_sb], h0, sb_w_o[0].astype(BF16), norm_gain[0, 1],
                       mlp_w1[0].astype(BF16), mlp_w2[0].astype(BF16))

    cmp_kv, k_slc, v_slc, k_win, v_win = _kv_proj(h1, kv_norm, nsa_w_kv.astype(BF16), rope_tabs)
    pos_flat = jnp.broadcast_to(cmp_pos.reshape(2, 1, CMP_LEN * HEAD_DIM), (2, 8, CMP_LEN * HEAD_DIM))
    kv_cmp = _cmp_mlp(cmp_kv, pos_flat, cmp_w1.astype(BF16), cmp_w2.astype(BF16), n_cmp)

    n_qk = N_HEADS * HEAD_DIM
    n_gate = 3 * N_HEADS
    w_q = jnp.pad(nsa_w_q[0], ((0, 0), (0, 128 - n_gate))).astype(BF16)
    gate_b = jnp.pad(nsa_gate_b[0], (0, 128 - n_gate)).reshape(1, 128)
    assert w_q.shape[1] == n_qk + 128
    q, q_rot, gates = _q_proj(h1, norm_gain[1, 0], w_q, gate_b, rope_tabs)
    o_cmp, selbias = _nsa_cmp_topk(q, kv_cmp[0], kv_cmp[1], _overlap_matrix(nch, n_sel), gates)
    o_sel = _nsa_sel(q_rot, k_slc, v_slc, selbias, gates)
    o_win = _nsa_win(q_rot, k_win, v_win, gates)
    out = _attn_out_mlp([o_cmp, o_sel, o_win], h1, nsa_w_o[0].astype(BF16), norm_gain[1, 1],
                        mlp_w1[1].astype(BF16), mlp_w2[1].astype(BF16), final_gain=final_norm)
    return out[None]
```

```python
import functools

import jax
import jax.numpy as jnp
from jax import lax
from jax.experimental import pallas as pl
from jax.experimental.pallas import tpu as pltpu

HEAD_DIM = 64
N_HEADS = 16
KV_GROUPS = 4
HEADS_PER_GROUP = N_HEADS // KV_GROUPS
CMP_LEN = 32
CMP_STRIDE = 16
CMP_HIDDEN = 256
SEL_LEN = 64
SEL_TOPK = 16
WINDOW = 512
ROPE_THETA = 500000.0
ROT_DIM = HEAD_DIM // 4
Q_TILE = 128
GATE_ROWS = 16
V_ROWS = HEAD_DIM + 16
LOG2E = 1.4426950408889634
NORM_EPS = 1e-5
NEG = -1e30
FORCED_SCORE = 1e6
SB_UNDERFLOW = -110.0
SB_NEAR = 256
VMEM_LIMIT = 56 * 1024 * 1024

BF16 = jnp.bfloat16
F32 = jnp.float32


def _params(semantics, vmem=VMEM_LIMIT):
    return pltpu.CompilerParams(dimension_semantics=semantics, vmem_limit_bytes=vmem)


def _rms(x, g):
    return x * lax.rsqrt(jnp.mean(x * x, axis=-1, keepdims=True) + NORM_EPS) * g


def _dot(a, b):
    return jnp.dot(a, b, preferred_element_type=F32)


def _dot_nt(a, b):
    return lax.dot_general(a, b, (((1,), (1,)), ((), ())), preferred_element_type=F32)


def _rope128(x, c, s1, s2):
    return x * c + pltpu.roll(x, 128 - ROT_DIM // 2, axis=1) * s1 + pltpu.roll(x, ROT_DIM // 2, axis=1) * s2


def _split3(x):
    hi = x.astype(BF16)
    r = x - hi.astype(F32)
    mid = r.astype(BF16)
    lo = (r - mid.astype(F32)).astype(BF16)
    return hi, mid, lo


def _qkv_kernel(x_ref, g_ref, w_ref, o_ref):
    xn = _rms(x_ref[...], g_ref[...]).astype(BF16)
    acc = _dot(xn, w_ref[...])
    d = x_ref.shape[1]
    o_ref[:, :d] = (acc[:, :d] * (HEAD_DIM ** -0.5)).astype(BF16)
    o_ref[:, d:] = acc[:, d:].astype(BF16)


def _qkv_proj(x, gain, w):
    s, d = x.shape
    n = w.shape[1]
    tm = 256
    return pl.pallas_call(
        _qkv_kernel,
        grid=(s // tm,),
        in_specs=[pl.BlockSpec((tm, d), lambda i: (i, 0)),
                  pl.BlockSpec((1, d), lambda i: (0, 0)),
                  pl.BlockSpec((d, n), lambda i: (0, 0))],
        out_specs=pl.BlockSpec((tm, n), lambda i: (i, 0)),
        out_shape=jax.ShapeDtypeStruct((s, n), BF16),
        compiler_params=_params(("arbitrary",)),
        name="qkv_proj",
    )(x, gain.reshape(1, d), w)


def _sb_blocks(qs, ks, vs, tri, carries, mask):
    n = len(qs)
    zs = [_dot_nt(qs[h], ks[h]) for h in range(n)]
    sps = [jnp.maximum(z, 0.0) + jnp.log1p(jnp.exp(-jnp.abs(z))) for z in zs]
    logs = [-sp for sp in sps]
    if mask is not None:
        logs = [jnp.where(mask, x, 0.0) for x in logs]
    his = [x.astype(BF16) for x in logs]
    los = [(logs[h] - his[h].astype(F32)).astype(BF16) for h in range(n)]
    excls = [_dot(his[h], tri) + _dot(los[h], tri) for h in range(n)]
    ws = [jnp.exp((zs[h] - sps[h]) + (excls[h] + carries[h])) for h in range(n)]
    if mask is not None:
        ws = [jnp.where(mask, w, 0.0) for w in ws]
    outs = [_dot(ws[h].astype(BF16), vs[h]) for h in range(n)]
    new_carries = [carries[h] + jnp.sum(logs[h], axis=1, keepdims=True) for h in range(n)]
    return outs, new_carries


def _sb_kernel(q_ref, k_ref, v_ref, tri_ref, o_ref):
    i = pl.program_id(1)
    tq = q_ref.shape[0]
    nh = q_ref.shape[1] // HEAD_DIM
    near = tri_ref.shape[0]
    tri_near = tri_ref[...]
    tri = tri_near[:tq, :tq]
    q = q_ref[...]
    head = lambda x, h: x[:, HEAD_DIM * h:HEAD_DIM * (h + 1)]

    start = pl.multiple_of(jnp.maximum(i * tq - (near - tq), 0), tq)
    row = lax.broadcasted_iota(jnp.int32, (tq, near), 0)
    col = lax.broadcasted_iota(jnp.int32, (tq, near), 1)
    causal = col - row < i * tq - start
    k = k_ref[pl.ds(start, near), :]
    v = v_ref[pl.ds(start, near), :]
    qs = [head(q, h) for h in range(nh)]
    split = lambda x: [head(x, h) for h in range(nh)]
    outs, carries = _sb_blocks(qs, split(k), split(v), tri_near, [jnp.zeros((tq, 1), F32)] * nh, causal)

    def worst(cs):
        mx = jnp.max(cs[0])
        for c in cs[1:]:
            mx = jnp.maximum(mx, jnp.max(c))
        return mx

    def cond(st):
        return jnp.logical_and(st[0] >= 0, st[1] > SB_UNDERFLOW)

    def body(st):
        kb, _, outs, carries = st
        start = pl.multiple_of(kb * tq, tq)
        k = k_ref[pl.ds(start, tq), :]
        v = v_ref[pl.ds(start, tq), :]
        more, new_c = _sb_blocks(qs, split(k), split(v), tri, list(carries), None)
        return kb - 1, worst(new_c), tuple(outs[h] + more[h] for h in range(nh)), tuple(new_c)

    st = lax.while_loop(cond, body, (start // tq - 1, worst(carries), tuple(outs), tuple(carries)))
    for h in range(nh):
        o_ref[:, HEAD_DIM * h:HEAD_DIM * (h + 1)] = st[2][h].astype(BF16)


def _sb_attention(qkv):
    s = qkv.shape[0]
    d = N_HEADS * HEAD_DIM
    hps = 4
    ngrp = N_HEADS // hps
    w = hps * HEAD_DIM
    near = SB_NEAR + Q_TILE
    idx = jnp.arange(near)
    tri = (idx[:, None] > idx[None, :]).astype(BF16)
    return pl.pallas_call(
        _sb_kernel,
        grid=(ngrp, s // Q_TILE),
        in_specs=[pl.BlockSpec((Q_TILE, w), lambda h, i: (i, h)),
                  pl.BlockSpec((s, w), lambda h, i: (0, ngrp + h)),
                  pl.BlockSpec((s, w), lambda h, i: (0, 2 * ngrp + h)),
                  pl.BlockSpec((near, near), lambda h, i: (0, 0))],
        out_specs=pl.BlockSpec((Q_TILE, w), lambda h, i: (i, h)),
        out_shape=jax.ShapeDtypeStruct((s, d), BF16),
        compiler_params=_params(("arbitrary", "arbitrary")),
        name="sb_attention",
    )(qkv, qkv, qkv, tri)


def _mlp_kernel(n_o, final, *refs):
    o_refs = refs[:n_o]
    res_ref, wo_ref, g_ref, w1_ref, w2_ref = refs[n_o:n_o + 5]
    pos = n_o + 5
    if final:
        fg_ref = refs[pos]
        pos += 1
    out_ref, h_ref, xn_ref, acc_ref = refs[pos:pos + 4]
    j = pl.program_id(1)

    @pl.when(j == 0)
    def _():
        o = o_refs[0][...]
        if n_o > 1:
            of = o.astype(F32)
            for r in o_refs[1:]:
                of = of + r[...].astype(F32)
            o = of.astype(BF16)
        h = res_ref[...] + _dot(o, wo_ref[...])
        h_ref[...] = h
        xn_ref[...] = _rms(h, g_ref[...]).astype(BF16)
        acc_ref[...] = jnp.zeros_like(acc_ref)

    a = jnp.maximum(_dot(xn_ref[...], w1_ref[...]), 0.0)
    acc_ref[...] += _dot((a * a).astype(BF16), w2_ref[...])

    @pl.when(j == pl.num_programs(1) - 1)
    def _():
        y = h_ref[...] + acc_ref[...]
        if final:
            y = _rms(y, fg_ref[...])
        out_ref[...] = y


def _attn_out_mlp(o_list, resid, wo, gain, w1, w2, final_gain=None):
    s, d = resid.shape
    f = w1.shape[1]
    tm, tf = 512, 512
    n_o = len(o_list)
    final = final_gain is not None
    row = lambda i, j: (i, 0)
    fixed = lambda i, j: (0, 0)
    in_specs = [pl.BlockSpec((tm, d), row) for _ in o_list]
    in_specs += [pl.BlockSpec((tm, d), row), pl.BlockSpec((d, d), fixed), pl.BlockSpec((1, d), fixed),
                 pl.BlockSpec((d, tf), lambda i, j: (0, j)), pl.BlockSpec((tf, d), lambda i, j: (j, 0))]
    args = list(o_list) + [resid, wo, gain.reshape(1, d), w1, w2]
    if final:
        in_specs.append(pl.BlockSpec((1, d), fixed))
        args.append(final_gain.reshape(1, d))
    return pl.pallas_call(
        functools.partial(_mlp_kernel, n_o, final),
        grid=(s // tm, f // tf),
        in_specs=in_specs,
        out_specs=pl.BlockSpec((tm, d), row),
        out_shape=jax.ShapeDtypeStruct((s, d), F32),
        scratch_shapes=[pltpu.VMEM((tm, d), F32), pltpu.VMEM((tm, d), BF16), pltpu.VMEM((tm, d), F32)],
        compiler_params=_params(("arbitrary", "arbitrary")),
        name="attn_out_mlp",
    )(*args)


def _kv_kernel(x_ref, g_ref, w_ref, c_ref, s1_ref, s2_ref, cmp_ref, ks_ref, vs_ref, kw_ref, vw_ref):
    xn = _rms(x_ref[...], g_ref[...]).astype(BF16)
    acc = _dot(xn, w_ref[...])
    gw = KV_GROUPS * HEAD_DIM
    cmp_ref[...] = acc[:, :2 * gw]
    c, s1, s2 = c_ref[...], s1_ref[...], s2_ref[...]

    def put_k(ref, x):
        for p in range(gw // 128):
            xp = _rope128(x[:, 128 * p:128 * (p + 1)], c, s1, s2)
            ref[2 * p] = xp[:, :HEAD_DIM].astype(BF16)
            ref[2 * p + 1] = xp[:, HEAD_DIM:].astype(BF16)

    def put_vt(ref, x):
        xt = x.T
        extra = ref.shape[1] - HEAD_DIM
        if extra:
            ones_row = (lax.broadcasted_iota(jnp.int32, (extra, xt.shape[1]), 0) == 0).astype(BF16)
        for g in range(KV_GROUPS):
            ref[g, :HEAD_DIM, :] = xt[HEAD_DIM * g:HEAD_DIM * (g + 1)].astype(BF16)
            if extra:
                ref[g, HEAD_DIM:, :] = ones_row

    put_k(ks_ref, acc[:, 2 * gw:3 * gw])
    put_vt(vs_ref, acc[:, 3 * gw:4 * gw])
    put_k(kw_ref, acc[:, 4 * gw:5 * gw])
    put_vt(vw_ref, acc[:, 5 * gw:6 * gw])


def _kv_proj(h, gain, w, rope_tabs):
    s, d = h.shape
    n = w.shape[1]
    gw = KV_GROUPS * HEAD_DIM
    tm = 256
    row = lambda i: (i, 0)
    fixed = lambda i: (0, 0)
    k_spec = pl.BlockSpec((KV_GROUPS, tm, HEAD_DIM), lambda i: (0, i, 0))
    k_shape = jax.ShapeDtypeStruct((KV_GROUPS, s, HEAD_DIM), BF16)
    vt_spec = pl.BlockSpec((KV_GROUPS, HEAD_DIM, tm), lambda i: (0, 0, i))
    vt_shape = jax.ShapeDtypeStruct((KV_GROUPS, HEAD_DIM, s), BF16)
    return pl.pallas_call(
        _kv_kernel,
        grid=(s // tm,),
        in_specs=[pl.BlockSpec((tm, d), row), pl.BlockSpec((1, d), fixed), pl.BlockSpec((d, n), fixed),
                  pl.BlockSpec((tm, 128), row), pl.BlockSpec((tm, 128), row), pl.BlockSpec((tm, 128), row)],
        out_specs=[pl.BlockSpec((tm, 2 * gw), row), k_spec,
                   pl.BlockSpec((KV_GROUPS, V_ROWS, tm), lambda i: (0, 0, i)), k_spec, vt_spec],
        out_shape=[jax.ShapeDtypeStruct((s, 2 * gw), F32), k_shape,
                   jax.ShapeDtypeStruct((KV_GROUPS, V_ROWS, s), BF16), k_shape, vt_shape],
        compiler_params=_params(("arbitrary",)),
        name="kv_proj",
    )(h, gain.reshape(1, d), w, *rope_tabs)


def _cmp_kernel(n_cmp, x_ref, pos_ref, w1_ref, w2_ref, w2t_ref, o_ref, ot_ref):
    nch = x_ref.shape[0] // CMP_STRIDE
    half = CMP_STRIDE * HEAD_DIM
    w1 = w1_ref[...]
    bias = _dot(pos_ref[...].astype(BF16), w1)[0:1]
    acc = [[jnp.zeros((nch, CMP_HIDDEN), F32) for _ in range(2)] for _ in range(2)]
    for l in range(CMP_STRIDE):
        y = x_ref[pl.ds(l, nch, stride=CMP_STRIDE), :].astype(BF16)
        wa = w1[HEAD_DIM * l:HEAD_DIM * (l + 1)]
        wb = w1[half + HEAD_DIM * l:half + HEAD_DIM * (l + 1)]
        for gg in range(2):
            yg = y[:, HEAD_DIM * gg:HEAD_DIM * (gg + 1)]
            acc[gg][0] = acc[gg][0] + _dot(yg, wa)
            acc[gg][1] = acc[gg][1] + _dot(yg, wb)
    live = lax.broadcasted_iota(jnp.int32, (nch, HEAD_DIM), 0) < n_cmp
    live_t = lax.broadcasted_iota(jnp.int32, (HEAD_DIM, nch), 1) < n_cmp
    for gg in range(2):
        hid = acc[gg][0] + pltpu.roll(acc[gg][1], nch - 1, axis=0) + bias
        hid = jax.nn.gelu(hid, approximate=True).astype(BF16)
        o_ref[gg] = jnp.where(live, _dot(hid, w2_ref[...]), 0.0).astype(BF16)
        ot_ref[gg] = jnp.where(live_t, _dot_nt(w2t_ref[...], hid), 0.0).astype(BF16)


def _cmp_mlp(cmp_kv, pos_flat, w1, w2, n_cmp):
    s = cmp_kv.shape[0]
    nch = s // CMP_STRIDE
    feat = CMP_LEN * HEAD_DIM
    return pl.pallas_call(
        functools.partial(_cmp_kernel, n_cmp),
        grid=(2, KV_GROUPS // 2),
        in_specs=[pl.BlockSpec((s, 2 * HEAD_DIM), lambda c, p: (0, c * (KV_GROUPS // 2) + p)),
                  pl.BlockSpec((None, 8, feat), lambda c, p: (c, 0, 0)),
                  pl.BlockSpec((None, feat, CMP_HIDDEN), lambda c, p: (c, 0, 0)),
                  pl.BlockSpec((None, CMP_HIDDEN, HEAD_DIM), lambda c, p: (c, 0, 0)),
                  pl.BlockSpec((None, HEAD_DIM, CMP_HIDDEN), lambda c, p: (c, 0, 0))],
        out_specs=[pl.BlockSpec((None, 2, nch, HEAD_DIM), lambda c, p: (c, p, 0, 0)),
                   pl.BlockSpec((None, 2, HEAD_DIM, nch), lambda c, p: (c, p, 0, 0))],
        out_shape=[jax.ShapeDtypeStruct((2, KV_GROUPS, nch, HEAD_DIM), BF16),
                   jax.ShapeDtypeStruct((2, KV_GROUPS, HEAD_DIM, nch), BF16)],
        compiler_params=_params(("arbitrary", "arbitrary")),
        name="cmp_mlp",
    )(cmp_kv, pos_flat, w1, w2, jnp.swapaxes(w2, 1, 2))


def _q_kernel(x_ref, g_ref, w_ref, b_ref, c_ref, s1_ref, s2_ref, q_ref, qr_ref, gate_ref):
    xn = _rms(x_ref[...], g_ref[...]).astype(BF16)
    acc = _dot(xn, w_ref[...])
    d = q_ref.shape[1]
    c, s1, s2 = c_ref[...], s1_ref[...], s2_ref[...]
    for p in range(d // 128):
        qp = acc[:, 128 * p:128 * (p + 1)] * (HEAD_DIM ** -0.5 * LOG2E)
        q_ref[:, 128 * p:128 * (p + 1)] = qp.astype(BF16)
        qr_ref[:, 128 * p:128 * (p + 1)] = _rope128(qp, c, s1, s2).astype(BF16)
    gates_t = (1.0 / (1.0 + jnp.exp(-(acc[:, d:] + b_ref[...])))).T
    for g in range(KV_GROUPS):
        gate_ref[g] = gates_t[GATE_ROWS * g:GATE_ROWS * (g + 1)]


def _q_proj(h, gain, w, gate_b, rope_tabs):
    s, d = h.shape
    n = w.shape[1]
    tm = 256
    row = lambda i: (i, 0)
    fixed = lambda i: (0, 0)
    return pl.pallas_call(
        _q_kernel,
        grid=(s // tm,),
        in_specs=[pl.BlockSpec((tm, d), row), pl.BlockSpec((1, d), fixed), pl.BlockSpec((d, n), fixed),
                  pl.BlockSpec((1, 128), fixed),
                  pl.BlockSpec((tm, 128), row), pl.BlockSpec((tm, 128), row), pl.BlockSpec((tm, 128), row)],
        out_specs=[pl.BlockSpec((tm, d), row), pl.BlockSpec((tm, d), row),
                   pl.BlockSpec((KV_GROUPS, GATE_ROWS, tm), lambda i: (0, 0, i))],
        out_shape=[jax.ShapeDtypeStruct((s, d), BF16), jax.ShapeDtypeStruct((s, d), BF16),
                   jax.ShapeDtypeStruct((KV_GROUPS, GATE_ROWS, s), F32)],
        compiler_params=_params(("arbitrary",)),
        name="nsa_q_proj",
    )(h, gain.reshape(1, d), w, gate_b, *rope_tabs)


def _group_gate_columns(a):
    lead = a.shape[:-1]
    a = a.reshape(lead + (3, KV_GROUPS, HEADS_PER_GROUP))
    a = jnp.moveaxis(a, -2, -3).reshape(lead + (KV_GROUPS, 3 * HEADS_PER_GROUP))
    a = jnp.pad(a, [(0, 0)] * len(lead) + [(0, 0), (0, GATE_ROWS - 3 * HEADS_PER_GROUP)])
    a = a.reshape(lead + (KV_GROUPS * GATE_ROWS,))
    return jnp.pad(a, [(0, 0)] * len(lead) + [(0, 128 - KV_GROUPS * GATE_ROWS)])


def _stack_heads(q):
    return jnp.concatenate([q[:, HEAD_DIM * h:HEAD_DIM * (h + 1)] for h in range(HEADS_PER_GROUP)], axis=0)


def _head_cols(x, h):
    return x[:, Q_TILE * h:Q_TILE * (h + 1)]


def _masked_softmax_t(st, valid):
    es, sums = [], []
    for h in range(HEADS_PER_GROUP):
        s = jnp.where(valid, _head_cols(st, h), NEG)
        e = jnp.exp2(s - jnp.max(s, axis=0, keepdims=True))
        es.append(e)
        sums.append(jnp.sum(e, axis=0, keepdims=True))
    return es, sums


def _nsa_kernel(tk, q_ref, qr_ref, kc_ref, vct_ref, ovt_ref, ks_ref, vst_ref, kw_ref, vwt_ref, gate_ref,
                o_ref, bias_ref, s0_ref, s1_ref, mx0_ref, mx1_ref, m_ref, acc_ref):
    i = pl.program_id(1)
    tq = q_ref.shape[0]
    nc = kc_ref.shape[0]
    n_sel = ovt_ref.shape[0]
    hp = HEADS_PER_GROUP
    q4 = _stack_heads(q_ref[...])
    qr4 = _stack_heads(qr_ref[...])
    tok = i * tq + lax.broadcasted_iota(jnp.int32, (1, tq), 1)
    gates = gate_ref[...]

    n_idx = lax.broadcasted_iota(jnp.int32, (nc, tq), 0)
    valid_c = n_idx * CMP_STRIDE + (CMP_LEN - 1) <= tok
    es, sums = _masked_softmax_t(_dot_nt(kc_ref[...], q4), valid_c)
    any_c = tok >= CMP_LEN - 1
    ps = [es[h] * jnp.where(any_c, 1.0 / sums[h], 0.0) for h in range(hp)]
    o_cmp = _dot(vct_ref[...], jnp.concatenate(ps, axis=1).astype(BF16))
    psum = ps[0] + ps[1] + ps[2] + ps[3]
    ovt = ovt_ref[...]
    imp = sum(_dot(ovt, term) for term in _split3(psum))

    m_idx = lax.broadcasted_iota(jnp.int32, (n_sel, tq), 0)
    blk_t = tok // SEL_LEN
    forced = (m_idx == 0) | (m_idx == blk_t) | (m_idx == blk_t - 1)
    causal = m_idx <= blk_t
    imp = jnp.where(causal, jnp.where(forced, FORCED_SCORE, imp), -jnp.inf)

    def pick(_, st):
        v, sel = st
        mx = jnp.max(v, axis=0, keepdims=True)
        idx = jnp.min(jnp.where(v == mx, m_idx, n_sel), axis=0, keepdims=True)
        hit = m_idx == idx
        return jnp.where(hit, -jnp.inf, v), jnp.where(hit, 1.0, sel)

    _, sel = lax.fori_loop(0, min(SEL_TOPK, n_sel), pick, (imp, jnp.zeros((n_sel, tq), F32)))
    bias_ref[...] = jnp.where(jnp.logical_and(sel > 0.5, causal), 0.0, NEG)

    bpt = tk // SEL_LEN

    key_in_tile = lax.broadcasted_iota(jnp.int32, (tk, tq), 0)

    def sel_scores(kt, s_ref, mx_ref):
        start = pl.multiple_of(kt * tk, tk)
        st = _dot_nt(ks_ref[pl.ds(start, tk), :], qr4)
        rows = bias_ref[pl.ds(pl.multiple_of(kt * bpt, bpt), bpt), :]
        bias = jnp.concatenate([jnp.broadcast_to(rows[j:j + 1], (SEL_LEN, tq)) for j in range(bpt)], axis=0)
        bias = jnp.where(key_in_tile <= tok - start, bias, NEG)
        s = jnp.concatenate([_head_cols(st, h) + bias for h in range(hp)], axis=1)
        s_ref[...] = s
        mx_ref[...] = jnp.max(s, axis=0, keepdims=True)

    def sel_accumulate(kt, s_ref, mx_ref):
        start = pl.multiple_of(kt * tk, tk)
        m = m_ref[...]
        m_new = jnp.maximum(m, mx_ref[...])
        p = jnp.exp2(s_ref[...] - m_new)
        acc_ref[...] = jnp.exp2(m - m_new) * acc_ref[...] + _dot(vst_ref[:, pl.ds(start, tk)], p.astype(BF16))
        m_ref[...] = m_new

    def sel_pair(j, carry):
        sel_scores(2 * j + 1, s1_ref, mx1_ref)
        sel_accumulate(2 * j, s0_ref, mx0_ref)
        sel_scores(2 * j + 2, s0_ref, mx0_ref)
        sel_accumulate(2 * j + 1, s1_ref, mx1_ref)
        return carry

    last = ((i + 1) * tq - 1) // tk
    m_ref[...] = jnp.full_like(m_ref, NEG)
    acc_ref[...] = jnp.zeros_like(acc_ref)
    sel_scores(0, s0_ref, mx0_ref)
    pairs = last // 2
    lax.fori_loop(0, pairs, sel_pair, 0)
    odd = last % 2 == 1

    @pl.when(odd)
    def _():
        sel_scores(last, s1_ref, mx1_ref)

    sel_accumulate(2 * pairs, s0_ref, mx0_ref)

    @pl.when(odd)
    def _():
        sel_accumulate(last, s1_ref, mx1_ref)

    acc = acc_ref[...]
    o_sel = acc[:HEAD_DIM] * (1.0 / acc[HEAD_DIM:HEAD_DIM + 1])

    span = WINDOW + tq
    start = pl.multiple_of(jnp.maximum(i * tq - WINDOW, 0), tq)
    kpos = start + lax.broadcasted_iota(jnp.int32, (span, tq), 0)
    diff = tok - kpos
    valid_w = jnp.logical_and(diff >= 0, diff < WINDOW)
    es, sums = _masked_softmax_t(_dot_nt(kw_ref[pl.ds(start, span), :], qr4), valid_w)
    o_win = _dot(vwt_ref[:, pl.ds(start, span)], jnp.concatenate(es, axis=1).astype(BF16))

    for h in range(hp):
        mix = (gates[h:h + 1] * _head_cols(o_cmp, h)
               + gates[hp + h:hp + h + 1] * _head_cols(o_sel, h)
               + (gates[2 * hp + h:2 * hp + h + 1] * (1.0 / sums[h])) * _head_cols(o_win, h))
        o_ref[:, HEAD_DIM * h:HEAD_DIM * (h + 1)] = mix.T.astype(BF16)


def _nsa_attention(q, q_rot, k_cmp, v_cmp_t, overlap_t, k_slc, v_slc_t, k_win, v_win_t, gates):
    s, d = q.shape
    nc = k_cmp.shape[1]
    n_sel = overlap_t.shape[0]
    gw = HEADS_PER_GROUP * HEAD_DIM
    cols = HEADS_PER_GROUP * Q_TILE
    tk = 512
    qspec = pl.BlockSpec((Q_TILE, gw), lambda g, i: (i, g))
    per_group = lambda shape: pl.BlockSpec((None,) + shape, lambda g, i: (g, 0, 0))
    return pl.pallas_call(
        functools.partial(_nsa_kernel, tk),
        grid=(KV_GROUPS, s // Q_TILE),
        in_specs=[qspec, qspec,
                  per_group((nc, HEAD_DIM)), per_group((HEAD_DIM, nc)),
                  pl.BlockSpec((n_sel, nc), lambda g, i: (0, 0)),
                  per_group((s, HEAD_DIM)), per_group((V_ROWS, s)),
                  per_group((s, HEAD_DIM)), per_group((HEAD_DIM, s)),
                  pl.BlockSpec((None, GATE_ROWS, Q_TILE), lambda g, i: (g, 0, i))],
        out_specs=qspec,
        out_shape=jax.ShapeDtypeStruct((s, d), BF16),
        scratch_shapes=[pltpu.VMEM((n_sel, Q_TILE), F32),
                        pltpu.VMEM((tk, cols), F32), pltpu.VMEM((tk, cols), F32),
                        pltpu.VMEM((1, cols), F32), pltpu.VMEM((1, cols), F32),
                        pltpu.VMEM((1, cols), F32), pltpu.VMEM((V_ROWS, cols), F32)],
        compiler_params=_params(("arbitrary", "arbitrary")),
        name="nsa_attention",
    )(q, q_rot, k_cmp, v_cmp_t, overlap_t, k_slc, v_slc_t, k_win, v_win_t, gates)


def _rope_tables(s):
    half = ROT_DIM // 2
    inv_freq = ROPE_THETA ** (-jnp.arange(half, dtype=F32) * 2.0 / ROT_DIM)
    ang = jnp.arange(s, dtype=F32)[:, None] * inv_freq[None, :]
    cos, sin = jnp.cos(ang), jnp.sin(ang)
    rest = HEAD_DIM - ROT_DIM
    c = jnp.concatenate([cos, cos, jnp.ones((s, rest), F32)], axis=1)
    s1 = jnp.concatenate([-sin, jnp.zeros((s, half + rest), F32)], axis=1)
    s2 = jnp.concatenate([jnp.zeros((s, half), F32), sin, jnp.zeros((s, rest), F32)], axis=1)
    return tuple(jnp.tile(a, (1, 2)) for a in (c, s1, s2))


def _overlap_matrix(nch, n_sel):
    cmp_start = jnp.arange(nch)[:, None] * CMP_STRIDE
    sel_start = jnp.arange(n_sel)[None, :] * SEL_LEN
    return ((cmp_start < sel_start + SEL_LEN) & (cmp_start + CMP_LEN > sel_start)).astype(BF16)


def kernel(x, norm_gain, sb_w_qkv, sb_w_o, kv_norm, nsa_w_kv, cmp_pos, cmp_w1, cmp_w2,
           nsa_w_q, nsa_gate_b, nsa_w_o, mlp_w1, mlp_w2, final_norm):
    b, s, d = x.shape
    assert b == 1 and d == N_HEADS * HEAD_DIM
    assert s % 512 == 0 and s >= WINDOW + Q_TILE
    n_cmp = (s - CMP_LEN) // CMP_STRIDE + 1
    nch = s // CMP_STRIDE
    n_sel = s // SEL_LEN
    h0 = x[0]
    rope_tabs = _rope_tables(s)

    qkv = _qkv_proj(h0, norm_gain[0, 0], sb_w_qkv[0].astype(BF16))
    o_sb = _sb_attention(qkv)
    h1 = _attn_out_mlp([o_sb], h0, sb_w_o[0].astype(BF16), norm_gain[0, 1],
                       mlp_w1[0].astype(BF16), mlp_w2[0].astype(BF16))

    cmp_kv, k_slc, v_slc_t, k_win, v_win_t = _kv_proj(h1, kv_norm, nsa_w_kv.astype(BF16), rope_tabs)
    pos_flat = jnp.broadcast_to(cmp_pos.reshape(2, 1, CMP_LEN * HEAD_DIM), (2, 8, CMP_LEN * HEAD_DIM))
    kv_cmp, kv_cmp_t = _cmp_mlp(cmp_kv, pos_flat, cmp_w1.astype(BF16), cmp_w2.astype(BF16), n_cmp)

    n_qk = N_HEADS * HEAD_DIM
    w_q = jnp.concatenate([nsa_w_q[0][:, :n_qk], _group_gate_columns(nsa_w_q[0][:, n_qk:])], axis=1).astype(BF16)
    gate_b = _group_gate_columns(nsa_gate_b[0]).reshape(1, 128)
    q, q_rot, gates = _q_proj(h1, norm_gain[1, 0], w_q, gate_b, rope_tabs)
    o_nsa = _nsa_attention(q, q_rot, kv_cmp[0], kv_cmp_t[1], _overlap_matrix(nch, n_sel).T,
                           k_slc, v_slc_t, k_win, v_win_t, gates)
    out = _attn_out_mlp([o_nsa], h1, nsa_w_o[0].astype(BF16), norm_gain[1, 1],
                        mlp_w1[1].astype(BF16), mlp_w2[1].astype(BF16), final_gain=final_norm)
    return out[None]
```

```python
import functools

import jax
import jax.numpy as jnp
from jax import lax
from jax.experimental import pallas as pl
from jax.experimental.pallas import tpu as pltpu

HEAD_DIM = 64
N_HEADS = 16
KV_GROUPS = 4
HEADS_PER_GROUP = N_HEADS // KV_GROUPS
CMP_LEN = 32
CMP_STRIDE = 16
CMP_HIDDEN = 256
SEL_LEN = 64
SEL_TOPK = 16
WINDOW = 512
ROPE_THETA = 500000.0
ROT_DIM = HEAD_DIM // 4
Q_TILE = 128
GATE_ROWS = 16
V_ROWS = HEAD_DIM + 16
LOG2E = 1.4426950408889634
CMP_CHUNK = 256
NORM_EPS = 1e-5
NEG = -1e30
FORCED_SCORE = 1e6
SB_UNDERFLOW = -110.0
SB_NEAR = 256
VMEM_LIMIT = 56 * 1024 * 1024

BF16 = jnp.bfloat16
F32 = jnp.float32


def _params(semantics, vmem=VMEM_LIMIT):
    return pltpu.CompilerParams(dimension_semantics=semantics, vmem_limit_bytes=vmem)


def _rms(x, g):
    return x * lax.rsqrt(jnp.mean(x * x, axis=-1, keepdims=True) + NORM_EPS) * g


def _dot(a, b):
    return jnp.dot(a, b, preferred_element_type=F32)


def _dot_nt(a, b):
    return lax.dot_general(a, b, (((1,), (1,)), ((), ())), preferred_element_type=F32)


def _rope128(x, c, s1, s2):
    return x * c + pltpu.roll(x, 128 - ROT_DIM // 2, axis=1) * s1 + pltpu.roll(x, ROT_DIM // 2, axis=1) * s2


def _split3(x):
    hi = x.astype(BF16)
    r = x - hi.astype(F32)
    mid = r.astype(BF16)
    lo = (r - mid.astype(F32)).astype(BF16)
    return hi, mid, lo


def _qkv_kernel(x_ref, g_ref, w_ref, o_ref):
    xn = _rms(x_ref[...], g_ref[...]).astype(BF16)
    acc = _dot(xn, w_ref[...])
    d = x_ref.shape[1]
    o_ref[:, :d] = (acc[:, :d] * (HEAD_DIM ** -0.5)).astype(BF16)
    o_ref[:, d:] = acc[:, d:].astype(BF16)


def _qkv_proj(x, gain, w):
    s, d = x.shape
    n = w.shape[1]
    tm = 256
    return pl.pallas_call(
        _qkv_kernel,
        grid=(s // tm,),
        in_specs=[pl.BlockSpec((tm, d), lambda i: (i, 0)),
                  pl.BlockSpec((1, d), lambda i: (0, 0)),
                  pl.BlockSpec((d, n), lambda i: (0, 0))],
        out_specs=pl.BlockSpec((tm, n), lambda i: (i, 0)),
        out_shape=jax.ShapeDtypeStruct((s, n), BF16),
        compiler_params=_params(("arbitrary",)),
        name="qkv_proj",
    )(x, gain.reshape(1, d), w)


def _sb_blocks(qs, ks, vs, tri, carries, mask):
    n = len(qs)
    zs = [_dot_nt(qs[h], ks[h]) for h in range(n)]
    sps = [jnp.maximum(z, 0.0) + jnp.log(1.0 + jnp.exp(-jnp.abs(z))) for z in zs]
    logs = [-sp for sp in sps]
    if mask is not None:
        logs = [jnp.where(mask, x, 0.0) for x in logs]
    excls = [_dot(logs[h].astype(BF16), tri) for h in range(n)]
    ws = [jnp.exp((zs[h] - sps[h]) + (excls[h] + carries[h])) for h in range(n)]
    if mask is not None:
        ws = [jnp.where(mask, w, 0.0) for w in ws]
    outs = [_dot(ws[h].astype(BF16), vs[h]) for h in range(n)]
    new_carries = [carries[h] + jnp.sum(logs[h], axis=1, keepdims=True) for h in range(n)]
    return outs, new_carries


def _sb_kernel(q_ref, k_ref, v_ref, tri_ref, o_ref):
    i = pl.program_id(1)
    tq = q_ref.shape[0]
    nh = q_ref.shape[1] // HEAD_DIM
    near = tri_ref.shape[0]
    tri_near = tri_ref[...]
    tri = tri_near[:tq, :tq]
    q = q_ref[...]
    head = lambda x, h: x[:, HEAD_DIM * h:HEAD_DIM * (h + 1)]

    start = pl.multiple_of(jnp.maximum(i * tq - (near - tq), 0), tq)
    row = lax.broadcasted_iota(jnp.int32, (tq, near), 0)
    col = lax.broadcasted_iota(jnp.int32, (tq, near), 1)
    causal = col - row < i * tq - start
    k = k_ref[pl.ds(start, near), :]
    v = v_ref[pl.ds(start, near), :]
    qs = [head(q, h) for h in range(nh)]
    split = lambda x: [head(x, h) for h in range(nh)]
    outs, carries = _sb_blocks(qs, split(k), split(v), tri_near, [jnp.zeros((tq, 1), F32)] * nh, causal)

    def worst(cs):
        mx = jnp.max(cs[0])
        for c in cs[1:]:
            mx = jnp.maximum(mx, jnp.max(c))
        return mx

    def cond(st):
        return jnp.logical_and(st[0] >= 0, st[1] > SB_UNDERFLOW)

    def body(st):
        kb, _, outs, carries = st
        start = pl.multiple_of(kb * tq, tq)
        k = k_ref[pl.ds(start, tq), :]
        v = v_ref[pl.ds(start, tq), :]
        more, new_c = _sb_blocks(qs, split(k), split(v), tri, list(carries), None)
        return kb - 1, worst(new_c), tuple(outs[h] + more[h] for h in range(nh)), tuple(new_c)

    st = lax.while_loop(cond, body, (start // tq - 1, worst(carries), tuple(outs), tuple(carries)))
    for h in range(nh):
        o_ref[:, HEAD_DIM * h:HEAD_DIM * (h + 1)] = st[2][h].astype(BF16)


def _sb_attention(qkv):
    s = qkv.shape[0]
    d = N_HEADS * HEAD_DIM
    hps = 4
    ngrp = N_HEADS // hps
    w = hps * HEAD_DIM
    near = SB_NEAR + Q_TILE
    idx = jnp.arange(near)
    tri = (idx[:, None] > idx[None, :]).astype(BF16)
    return pl.pallas_call(
        _sb_kernel,
        grid=(ngrp, s // Q_TILE),
        in_specs=[pl.BlockSpec((Q_TILE, w), lambda h, i: (i, h)),
                  pl.BlockSpec((s, w), lambda h, i: (0, ngrp + h)),
                  pl.BlockSpec((s, w), lambda h, i: (0, 2 * ngrp + h)),
                  pl.BlockSpec((near, near), lambda h, i: (0, 0))],
        out_specs=pl.BlockSpec((Q_TILE, w), lambda h, i: (i, h)),
        out_shape=jax.ShapeDtypeStruct((s, d), BF16),
        compiler_params=_params(("arbitrary", "arbitrary")),
        name="sb_attention",
    )(qkv, qkv, qkv, tri)


def _mlp_kernel(n_o, final, *refs):
    o_refs = refs[:n_o]
    res_ref, wo_ref, g_ref, w1_ref, w2_ref = refs[n_o:n_o + 5]
    pos = n_o + 5
    if final:
        fg_ref = refs[pos]
        pos += 1
    out_ref, h_ref, xn_ref, acc_ref = refs[pos:pos + 4]
    j = pl.program_id(1)

    @pl.when(j == 0)
    def _():
        o = o_refs[0][...]
        if n_o > 1:
            of = o.astype(F32)
            for r in o_refs[1:]:
                of = of + r[...].astype(F32)
            o = of.astype(BF16)
        h = res_ref[...] + _dot(o, wo_ref[...])
        h_ref[...] = h
        xn_ref[...] = _rms(h, g_ref[...]).astype(BF16)
        acc_ref[...] = jnp.zeros_like(acc_ref)

    a = jnp.maximum(_dot(xn_ref[...], w1_ref[...]), 0.0)
    acc_ref[...] += _dot((a * a).astype(BF16), w2_ref[...])

    @pl.when(j == pl.num_programs(1) - 1)
    def _():
        y = h_ref[...] + acc_ref[...]
        if final:
            y = _rms(y, fg_ref[...])
        out_ref[...] = y


def _attn_out_mlp(o_list, resid, wo, gain, w1, w2, final_gain=None):
    s, d = resid.shape
    f = w1.shape[1]
    tm, tf = 512, 512
    n_o = len(o_list)
    final = final_gain is not None
    row = lambda i, j: (i, 0)
    fixed = lambda i, j: (0, 0)
    in_specs = [pl.BlockSpec((tm, d), row) for _ in o_list]
    in_specs += [pl.BlockSpec((tm, d), row), pl.BlockSpec((d, d), fixed), pl.BlockSpec((1, d), fixed),
                 pl.BlockSpec((d, tf), lambda i, j: (0, j)), pl.BlockSpec((tf, d), lambda i, j: (j, 0))]
    args = list(o_list) + [resid, wo, gain.reshape(1, d), w1, w2]
    if final:
        in_specs.append(pl.BlockSpec((1, d), fixed))
        args.append(final_gain.reshape(1, d))
    return pl.pallas_call(
        functools.partial(_mlp_kernel, n_o, final),
        grid=(s // tm, f // tf),
        in_specs=in_specs,
        out_specs=pl.BlockSpec((tm, d), row),
        out_shape=jax.ShapeDtypeStruct((s, d), F32),
        scratch_shapes=[pltpu.VMEM((tm, d), F32), pltpu.VMEM((tm, d), BF16), pltpu.VMEM((tm, d), F32)],
        compiler_params=_params(("arbitrary", "arbitrary")),
        name="attn_out_mlp",
    )(*args)


def _kv_kernel(x_ref, g_ref, w_ref, c_ref, s1_ref, s2_ref, cmp_ref, ks_ref, vs_ref, kw_ref, vw_ref):
    xn = _rms(x_ref[...], g_ref[...]).astype(BF16)
    acc = _dot(xn, w_ref[...])
    gw = KV_GROUPS * HEAD_DIM
    cmp_ref[...] = acc[:, :2 * gw]
    c, s1, s2 = c_ref[...], s1_ref[...], s2_ref[...]

    def put_k(ref, x):
        for p in range(gw // 128):
            xp = _rope128(x[:, 128 * p:128 * (p + 1)], c, s1, s2)
            ref[2 * p] = xp[:, :HEAD_DIM].astype(BF16)
            ref[2 * p + 1] = xp[:, HEAD_DIM:].astype(BF16)

    def put_vt(ref, x):
        xt = x.T
        extra = ref.shape[1] - HEAD_DIM
        if extra:
            ones_row = (lax.broadcasted_iota(jnp.int32, (extra, xt.shape[1]), 0) == 0).astype(BF16)
        for g in range(KV_GROUPS):
            ref[g, :HEAD_DIM, :] = xt[HEAD_DIM * g:HEAD_DIM * (g + 1)].astype(BF16)
            if extra:
                ref[g, HEAD_DIM:, :] = ones_row

    put_k(ks_ref, acc[:, 2 * gw:3 * gw])
    put_vt(vs_ref, acc[:, 3 * gw:4 * gw])
    put_k(kw_ref, acc[:, 4 * gw:5 * gw])
    put_vt(vw_ref, acc[:, 5 * gw:6 * gw])


def _kv_proj(h, gain, w, rope_tabs):
    s, d = h.shape
    n = w.shape[1]
    gw = KV_GROUPS * HEAD_DIM
    tm = 256
    row = lambda i: (i, 0)
    fixed = lambda i: (0, 0)
    k_spec = pl.BlockSpec((KV_GROUPS, tm, HEAD_DIM), lambda i: (0, i, 0))
    k_shape = jax.ShapeDtypeStruct((KV_GROUPS, s, HEAD_DIM), BF16)
    vt_spec = pl.BlockSpec((KV_GROUPS, HEAD_DIM, tm), lambda i: (0, 0, i))
    vt_shape = jax.ShapeDtypeStruct((KV_GROUPS, HEAD_DIM, s), BF16)
    return pl.pallas_call(
        _kv_kernel,
        grid=(s // tm,),
        in_specs=[pl.BlockSpec((tm, d), row), pl.BlockSpec((1, d), fixed), pl.BlockSpec((d, n), fixed),
                  pl.BlockSpec((tm, 128), row), pl.BlockSpec((tm, 128), row), pl.BlockSpec((tm, 128), row)],
        out_specs=[pl.BlockSpec((tm, 2 * gw), row), k_spec,
                   pl.BlockSpec((KV_GROUPS, V_ROWS, tm), lambda i: (0, 0, i)), k_spec, vt_spec],
        out_shape=[jax.ShapeDtypeStruct((s, 2 * gw), F32), k_shape,
                   jax.ShapeDtypeStruct((KV_GROUPS, V_ROWS, s), BF16), k_shape, vt_shape],
        compiler_params=_params(("arbitrary",)),
        name="kv_proj",
    )(h, gain.reshape(1, d), w, *rope_tabs)


def _cmp_kernel(n_cmp, x_ref, pos_ref, w1_ref, w2_ref, w2t_ref, o_ref, ot_ref):
    nch = x_ref.shape[0] // CMP_STRIDE
    half = CMP_STRIDE * HEAD_DIM
    w1 = w1_ref[...]
    bias = _dot(pos_ref[...].astype(BF16), w1)[0:1]
    acc = [[jnp.zeros((nch, CMP_HIDDEN), F32) for _ in range(2)] for _ in range(2)]
    for l in range(CMP_STRIDE):
        y = x_ref[pl.ds(l, nch, stride=CMP_STRIDE), :].astype(BF16)
        wa = w1[HEAD_DIM * l:HEAD_DIM * (l + 1)]
        wb = w1[half + HEAD_DIM * l:half + HEAD_DIM * (l + 1)]
        for gg in range(2):
            yg = y[:, HEAD_DIM * gg:HEAD_DIM * (gg + 1)]
            acc[gg][0] = acc[gg][0] + _dot(yg, wa)
            acc[gg][1] = acc[gg][1] + _dot(yg, wb)
    live = lax.broadcasted_iota(jnp.int32, (nch, HEAD_DIM), 0) < n_cmp
    live_t = lax.broadcasted_iota(jnp.int32, (HEAD_DIM, nch), 1) < n_cmp
    for gg in range(2):
        hid = acc[gg][0] + pltpu.roll(acc[gg][1], nch - 1, axis=0) + bias
        hid = jax.nn.gelu(hid, approximate=True).astype(BF16)
        o_ref[gg] = jnp.where(live, _dot(hid, w2_ref[...]), 0.0).astype(BF16)
        ot_ref[gg] = jnp.where(live_t, _dot_nt(w2t_ref[...], hid), 0.0).astype(BF16)


def _cmp_mlp(cmp_kv, pos_flat, w1, w2, n_cmp):
    s = cmp_kv.shape[0]
    nch = s // CMP_STRIDE
    feat = CMP_LEN * HEAD_DIM
    return pl.pallas_call(
        functools.partial(_cmp_kernel, n_cmp),
        grid=(2, KV_GROUPS // 2),
        in_specs=[pl.BlockSpec((s, 2 * HEAD_DIM), lambda c, p: (0, c * (KV_GROUPS // 2) + p)),
                  pl.BlockSpec((None, 8, feat), lambda c, p: (c, 0, 0)),
                  pl.BlockSpec((None, feat, CMP_HIDDEN), lambda c, p: (c, 0, 0)),
                  pl.BlockSpec((None, CMP_HIDDEN, HEAD_DIM), lambda c, p: (c, 0, 0)),
                  pl.BlockSpec((None, HEAD_DIM, CMP_HIDDEN), lambda c, p: (c, 0, 0))],
        out_specs=[pl.BlockSpec((None, 2, nch, HEAD_DIM), lambda c, p: (c, p, 0, 0)),
                   pl.BlockSpec((None, 2, HEAD_DIM, nch), lambda c, p: (c, p, 0, 0))],
        out_shape=[jax.ShapeDtypeStruct((2, KV_GROUPS, nch, HEAD_DIM), BF16),
                   jax.ShapeDtypeStruct((2, KV_GROUPS, HEAD_DIM, nch), BF16)],
        compiler_params=_params(("arbitrary", "arbitrary")),
        name="cmp_mlp",
    )(cmp_kv, pos_flat, w1, w2, jnp.swapaxes(w2, 1, 2))


def _q_kernel(x_ref, g_ref, w_ref, b_ref, c_ref, s1_ref, s2_ref, q_ref, qr_ref, gate_ref):
    xn = _rms(x_ref[...], g_ref[...]).astype(BF16)
    acc = _dot(xn, w_ref[...])
    d = q_ref.shape[1]
    c, s1, s2 = c_ref[...], s1_ref[...], s2_ref[...]
    for p in range(d // 128):
        qp = acc[:, 128 * p:128 * (p + 1)] * (HEAD_DIM ** -0.5 * LOG2E)
        q_ref[:, 128 * p:128 * (p + 1)] = qp.astype(BF16)
        qr_ref[:, 128 * p:128 * (p + 1)] = _rope128(qp, c, s1, s2).astype(BF16)
    gates_t = (1.0 / (1.0 + jnp.exp(-(acc[:, d:] + b_ref[...])))).T
    for g in range(KV_GROUPS):
        gate_ref[g] = gates_t[GATE_ROWS * g:GATE_ROWS * (g + 1)]


def _q_proj(h, gain, w, gate_b, rope_tabs):
    s, d = h.shape
    n = w.shape[1]
    tm = 256
    row = lambda i: (i, 0)
    fixed = lambda i: (0, 0)
    return pl.pallas_call(
        _q_kernel,
        grid=(s // tm,),
        in_specs=[pl.BlockSpec((tm, d), row), pl.BlockSpec((1, d), fixed), pl.BlockSpec((d, n), fixed),
                  pl.BlockSpec((1, 128), fixed),
                  pl.BlockSpec((tm, 128), row), pl.BlockSpec((tm, 128), row), pl.BlockSpec((tm, 128), row)],
        out_specs=[pl.BlockSpec((tm, d), row), pl.BlockSpec((tm, d), row),
                   pl.BlockSpec((KV_GROUPS, GATE_ROWS, tm), lambda i: (0, 0, i))],
        out_shape=[jax.ShapeDtypeStruct((s, d), BF16), jax.ShapeDtypeStruct((s, d), BF16),
                   jax.ShapeDtypeStruct((KV_GROUPS, GATE_ROWS, s), F32)],
        compiler_params=_params(("arbitrary",)),
        name="nsa_q_proj",
    )(h, gain.reshape(1, d), w, gate_b, *rope_tabs)


def _group_gate_columns(a):
    lead = a.shape[:-1]
    a = a.reshape(lead + (3, KV_GROUPS, HEADS_PER_GROUP))
    a = jnp.moveaxis(a, -2, -3).reshape(lead + (KV_GROUPS, 3 * HEADS_PER_GROUP))
    a = jnp.pad(a, [(0, 0)] * len(lead) + [(0, 0), (0, GATE_ROWS - 3 * HEADS_PER_GROUP)])
    a = a.reshape(lead + (KV_GROUPS * GATE_ROWS,))
    return jnp.pad(a, [(0, 0)] * len(lead) + [(0, 128 - KV_GROUPS * GATE_ROWS)])


def _stack_heads(q):
    return jnp.concatenate([q[:, HEAD_DIM * h:HEAD_DIM * (h + 1)] for h in range(HEADS_PER_GROUP)], axis=0)


def _head_cols(x, h):
    return x[:, Q_TILE * h:Q_TILE * (h + 1)]


def _masked_softmax_t(st, valid):
    es, sums = [], []
    for h in range(HEADS_PER_GROUP):
        s = jnp.where(valid, _head_cols(st, h), NEG)
        e = jnp.exp2(s - jnp.max(s, axis=0, keepdims=True))
        es.append(e)
        sums.append(jnp.sum(e, axis=0, keepdims=True))
    return es, sums


def _nsa_kernel(tk, q_ref, qr_ref, kc_ref, vct_ref, ovt_ref, ks_ref, vst_ref, kw_ref, vwt_ref, gate_ref,
                o_ref, ocmp_ref, bias_ref, s0_ref, s1_ref, mx0_ref, mx1_ref, m_ref, acc_ref):
    i = pl.program_id(1)
    tq = q_ref.shape[0]
    nc = kc_ref.shape[0]
    n_sel = ovt_ref.shape[0]
    hp = HEADS_PER_GROUP
    q4 = _stack_heads(q_ref[...])
    qr4 = _stack_heads(qr_ref[...])
    tok = i * tq + lax.broadcasted_iota(jnp.int32, (1, tq), 1)
    gates = gate_ref[...]

    def cmp_and_select(ncv, rows):
        n_idx = lax.broadcasted_iota(jnp.int32, (ncv, tq), 0)
        valid_c = n_idx * CMP_STRIDE + (CMP_LEN - 1) <= tok
        es, sums = _masked_softmax_t(_dot_nt(kc_ref[:ncv, :], q4), valid_c)
        any_c = tok >= CMP_LEN - 1
        ps = [es[h] * jnp.where(any_c, 1.0 / sums[h], 0.0) for h in range(hp)]
        ocmp_ref[...] = _dot(vct_ref[:, :ncv], jnp.concatenate(ps, axis=1).astype(BF16))
        psum = ps[0] + ps[1] + ps[2] + ps[3]
        ovt = ovt_ref[:rows, :ncv]
        imp = sum(_dot(ovt, term) for term in _split3(psum))

        m_idx = lax.broadcasted_iota(jnp.int32, (rows, tq), 0)
        blk_t = tok // SEL_LEN
        forced = (m_idx == 0) | (m_idx == blk_t) | (m_idx == blk_t - 1)
        causal = m_idx <= blk_t
        imp = jnp.where(jnp.logical_and(causal, jnp.logical_not(forced)), imp, -jnp.inf)

        def pick(_, st):
            v, sel = st
            mx = jnp.max(v, axis=0, keepdims=True)
            idx = jnp.min(jnp.where(v == mx, m_idx, rows), axis=0, keepdims=True)
            hit = m_idx == idx
            return jnp.where(hit, -jnp.inf, v), jnp.where(hit, 1.0, sel)

        _, sel = lax.fori_loop(0, SEL_TOPK - 3, pick, (imp, jnp.where(forced, 1.0, 0.0)))
        before = m_idx < (i * tq) // SEL_LEN
        bias_ref[:rows, :] = jnp.where(jnp.logical_and(sel > 0.5, before), 0.0, NEG)
        if rows < n_sel:
            bias_ref[rows:, :] = jnp.full((n_sel - rows, tq), NEG, F32)

    rows_per_chunk = CMP_CHUNK * CMP_STRIDE // SEL_LEN
    n_prefix = -(-nc // CMP_CHUNK)
    need = ((i + 1) * tq - 1) // (CMP_STRIDE * CMP_CHUNK) + 1
    for c in range(1, n_prefix + 1):
        pl.when(need == c)(functools.partial(cmp_and_select, min(c * CMP_CHUNK, nc), min(c * rows_per_chunk, n_sel)))
    o_cmp = ocmp_ref[...]

    q0 = pl.multiple_of(i * tq, tq)
    sd = _dot_nt(ks_ref[pl.ds(q0, tq), :], qr4)
    not_future = lax.broadcasted_iota(jnp.int32, (tq, tq), 0) <= lax.broadcasted_iota(jnp.int32, (tq, tq), 1)
    sd = jnp.concatenate([jnp.where(not_future, _head_cols(sd, h), NEG) for h in range(hp)], axis=1)
    m0 = jnp.max(sd, axis=0, keepdims=True)
    m_ref[...] = m0
    acc_ref[...] = _dot(vst_ref[:, pl.ds(q0, tq)], jnp.exp2(sd - m0).astype(BF16))

    bpt = tk // SEL_LEN

    def sel_scores(kt, s_ref, mx_ref):
        start = pl.multiple_of(kt * tk, tk)
        st = _dot_nt(ks_ref[pl.ds(start, tk), :], qr4)
        rows = bias_ref[pl.ds(pl.multiple_of(kt * bpt, bpt), bpt), :]
        bias = jnp.concatenate([jnp.broadcast_to(rows[j:j + 1], (SEL_LEN, tq)) for j in range(bpt)], axis=0)
        s = jnp.concatenate([_head_cols(st, h) + bias for h in range(hp)], axis=1)
        s_ref[...] = s
        mx_ref[...] = jnp.max(s, axis=0, keepdims=True)

    def sel_accumulate(kt, s_ref, mx_ref):
        start = pl.multiple_of(kt * tk, tk)
        m = m_ref[...]
        m_new = jnp.maximum(m, mx_ref[...])
        p = jnp.exp2(s_ref[...] - m_new)
        acc_ref[...] = jnp.exp2(m - m_new) * acc_ref[...] + _dot(vst_ref[:, pl.ds(start, tk)], p.astype(BF16))
        m_ref[...] = m_new

    def sel_pair(j, carry):
        sel_scores(2 * j + 1, s1_ref, mx1_ref)
        sel_accumulate(2 * j, s0_ref, mx0_ref)
        sel_scores(2 * j + 2, s0_ref, mx0_ref)
        sel_accumulate(2 * j + 1, s1_ref, mx1_ref)
        return carry

    last = jnp.maximum(i * tq - 1, 0) // tk
    sel_scores(0, s0_ref, mx0_ref)
    pairs = last // 2
    lax.fori_loop(0, pairs, sel_pair, 0)
    odd = last % 2 == 1

    @pl.when(odd)
    def _():
        sel_scores(last, s1_ref, mx1_ref)

    sel_accumulate(2 * pairs, s0_ref, mx0_ref)

    @pl.when(odd)
    def _():
        sel_accumulate(last, s1_ref, mx1_ref)

    acc = acc_ref[...]
    o_sel = acc[:HEAD_DIM] * (1.0 / acc[HEAD_DIM:HEAD_DIM + 1])

    span = WINDOW + tq
    start = pl.multiple_of(jnp.maximum(i * tq - WINDOW, 0), tq)
    kpos = start + lax.broadcasted_iota(jnp.int32, (span, tq), 0)
    diff = tok - kpos
    valid_w = jnp.logical_and(diff >= 0, diff < WINDOW)
    es, sums = _masked_softmax_t(_dot_nt(kw_ref[pl.ds(start, span), :], qr4), valid_w)
    o_win = _dot(vwt_ref[:, pl.ds(start, span)], jnp.concatenate(es, axis=1).astype(BF16))

    for h in range(hp):
        mix = (gates[h:h + 1] * _head_cols(o_cmp, h)
               + gates[hp + h:hp + h + 1] * _head_cols(o_sel, h)
               + (gates[2 * hp + h:2 * hp + h + 1] * (1.0 / sums[h])) * _head_cols(o_win, h))
        o_ref[:, HEAD_DIM * h:HEAD_DIM * (h + 1)] = mix.T.astype(BF16)


def _nsa_attention(q, q_rot, k_cmp, v_cmp_t, overlap_t, k_slc, v_slc_t, k_win, v_win_t, gates):
    s, d = q.shape
    nc = k_cmp.shape[1]
    n_sel = overlap_t.shape[0]
    gw = HEADS_PER_GROUP * HEAD_DIM
    cols = HEADS_PER_GROUP * Q_TILE
    tk = 512
    qspec = pl.BlockSpec((Q_TILE, gw), lambda g, i: (i, g))
    per_group = lambda shape: pl.BlockSpec((None,) + shape, lambda g, i: (g, 0, 0))
    return pl.pallas_call(
        functools.partial(_nsa_kernel, tk),
        grid=(KV_GROUPS, s // Q_TILE),
        in_specs=[qspec, qspec,
                  per_group((nc, HEAD_DIM)), per_group((HEAD_DIM, nc)),
                  pl.BlockSpec((n_sel, nc), lambda g, i: (0, 0)),
                  per_group((s, HEAD_DIM)), per_group((V_ROWS, s)),
                  per_group((s, HEAD_DIM)), per_group((HEAD_DIM, s)),
                  pl.BlockSpec((None, GATE_ROWS, Q_TILE), lambda g, i: (g, 0, i))],
        out_specs=qspec,
        out_shape=jax.ShapeDtypeStruct((s, d), BF16),
        scratch_shapes=[pltpu.VMEM((HEAD_DIM, cols), F32), pltpu.VMEM((n_sel, Q_TILE), F32),
                        pltpu.VMEM((tk, cols), F32), pltpu.VMEM((tk, cols), F32),
                        pltpu.VMEM((1, cols), F32), pltpu.VMEM((1, cols), F32),
                        pltpu.VMEM((1, cols), F32), pltpu.VMEM((V_ROWS, cols), F32)],
        compiler_params=_params(("arbitrary", "arbitrary")),
        name="nsa_attention",
    )(q, q_rot, k_cmp, v_cmp_t, overlap_t, k_slc, v_slc_t, k_win, v_win_t, gates)


def _rope_tables(s):
    half = ROT_DIM // 2
    inv_freq = ROPE_THETA ** (-jnp.arange(half, dtype=F32) * 2.0 / ROT_DIM)
    ang = jnp.arange(s, dtype=F32)[:, None] * inv_freq[None, :]
    cos, sin = jnp.cos(ang), jnp.sin(ang)
    rest = HEAD_DIM - ROT_DIM
    c = jnp.concatenate([cos, cos, jnp.ones((s, rest), F32)], axis=1)
    s1 = jnp.concatenate([-sin, jnp.zeros((s, half + rest), F32)], axis=1)
    s2 = jnp.concatenate([jnp.zeros((s, half), F32), sin, jnp.zeros((s, rest), F32)], axis=1)
    return tuple(jnp.tile(a, (1, 2)) for a in (c, s1, s2))


def _overlap_matrix(nch, n_sel):
    cmp_start = jnp.arange(nch)[:, None] * CMP_STRIDE
    sel_start = jnp.arange(n_sel)[None, :] * SEL_LEN
    return ((cmp_start < sel_start + SEL_LEN) & (cmp_start + CMP_LEN > sel_start)).astype(BF16)


def kernel(x, norm_gain, sb_w_qkv, sb_w_o, kv_norm, nsa_w_kv, cmp_pos, cmp_w1, cmp_w2,
           nsa_w_q, nsa_gate_b, nsa_w_o, mlp_w1, mlp_w2, final_norm):
    b, s, d = x.shape
    assert b == 1 and d == N_HEADS * HEAD_DIM
    assert s % 512 == 0 and s >= WINDOW + Q_TILE and s // SEL_LEN >= SEL_TOPK
    n_cmp = (s - CMP_LEN) // CMP_STRIDE + 1
    nch = s // CMP_STRIDE
    n_sel = s // SEL_LEN
    h0 = x[0]
    rope_tabs = _rope_tables(s)

    qkv = _qkv_proj(h0, norm_gain[0, 0], sb_w_qkv[0].astype(BF16))
    o_sb = _sb_attention(qkv)
    h1 = _attn_out_mlp([o_sb], h0, sb_w_o[0].astype(BF16), norm_gain[0, 1],
                       mlp_w1[0].astype(BF16), mlp_w2[0].astype(BF16))

    cmp_kv, k_slc, v_slc_t, k_win, v_win_t = _kv_proj(h1, kv_norm, nsa_w_kv.astype(BF16), rope_tabs)
    pos_flat = jnp.broadcast_to(cmp_pos.reshape(2, 1, CMP_LEN * HEAD_DIM), (2, 8, CMP_LEN * HEAD_DIM))
    kv_cmp, kv_cmp_t = _cmp_mlp(cmp_kv, pos_flat, cmp_w1.astype(BF16), cmp_w2.astype(BF16), n_cmp)

    n_qk = N_HEADS * HEAD_DIM
    w_q = jnp.concatenate([nsa_w_q[0][:, :n_qk], _group_gate_columns(nsa_w_q[0][:, n_qk:])], axis=1).astype(BF16)
    gate_b = _group_gate_columns(nsa_gate_b[0]).reshape(1, 128)
    q, q_rot, gates = _q_proj(h1, norm_gain[1, 0], w_q, gate_b, rope_tabs)
    o_nsa = _nsa_attention(q, q_rot, kv_cmp[0], kv_cmp_t[1], _overlap_matrix(nch, n_sel).T,
                           k_slc, v_slc_t, k_win, v_win_t, gates)
    out = _attn_out_mlp([o_nsa], h1, nsa_w_o[0].astype(BF16), norm_gain[1, 1],
                        mlp_w1[1].astype(BF16), mlp_w2[1].astype(BF16), final_gain=final_norm)
    return out[None]
```

```python
import functools

import jax
import jax.numpy as jnp
from jax import lax
from jax.experimental import pallas as pl
from jax.experimental.pallas import tpu as pltpu

HEAD_DIM = 64
N_HEADS = 16
KV_GROUPS = 4
HEADS_PER_GROUP = N_HEADS // KV_GROUPS
CMP_LEN = 32
CMP_STRIDE = 16
CMP_HIDDEN = 256
SEL_LEN = 64
SEL_TOPK = 16
WINDOW = 512
ROPE_THETA = 500000.0
ROT_DIM = HEAD_DIM // 4
Q_TILE = 128
GATE_ROWS = 16
V_ROWS = HEAD_DIM + 16
LOG2E = 1.4426950408889634
CMP_CHUNK = 256
NORM_EPS = 1e-5
NEG = -1e30
FORCED_SCORE = 1e6
SB_UNDERFLOW = -110.0
SB_NEAR = 256
VMEM_LIMIT = 56 * 1024 * 1024

BF16 = jnp.bfloat16
F32 = jnp.float32


def _params(semantics, vmem=VMEM_LIMIT):
    return pltpu.CompilerParams(dimension_semantics=semantics, vmem_limit_bytes=vmem)


def _rms(x, g):
    return x * lax.rsqrt(jnp.mean(x * x, axis=-1, keepdims=True) + NORM_EPS) * g


def _dot(a, b):
    return jnp.dot(a, b, preferred_element_type=F32)


def _dot_nt(a, b):
    return lax.dot_general(a, b, (((1,), (1,)), ((), ())), preferred_element_type=F32)


def _rope128(x, c, s1, s2):
    return x * c + pltpu.roll(x, 128 - ROT_DIM // 2, axis=1) * s1 + pltpu.roll(x, ROT_DIM // 2, axis=1) * s2


def _split3(x):
    hi = x.astype(BF16)
    r = x - hi.astype(F32)
    mid = r.astype(BF16)
    lo = (r - mid.astype(F32)).astype(BF16)
    return hi, mid, lo


def _qkv_kernel(x_ref, g_ref, w_ref, o_ref):
    xn = _rms(x_ref[...], g_ref[...]).astype(BF16)
    acc = _dot(xn, w_ref[...])
    d = x_ref.shape[1]
    o_ref[:, :d] = (acc[:, :d] * (HEAD_DIM ** -0.5)).astype(BF16)
    o_ref[:, d:] = acc[:, d:].astype(BF16)


def _qkv_proj(x, gain, w):
    s, d = x.shape
    n = w.shape[1]
    tm = 256
    return pl.pallas_call(
        _qkv_kernel,
        grid=(s // tm,),
        in_specs=[pl.BlockSpec((tm, d), lambda i: (i, 0)),
                  pl.BlockSpec((1, d), lambda i: (0, 0)),
                  pl.BlockSpec((d, n), lambda i: (0, 0))],
        out_specs=pl.BlockSpec((tm, n), lambda i: (i, 0)),
        out_shape=jax.ShapeDtypeStruct((s, n), BF16),
        compiler_params=_params(("arbitrary",)),
        name="qkv_proj",
    )(x, gain.reshape(1, d), w)


def _sb_blocks(qs, ks, vs, tri, carries, mask):
    n = len(qs)
    zs = [_dot_nt(qs[h], ks[h]) for h in range(n)]
    sps = [jnp.maximum(z, 0.0) + jnp.log(1.0 + jnp.exp(-jnp.abs(z))) for z in zs]
    logs = [-sp for sp in sps]
    if mask is not None:
        logs = [jnp.where(mask, x, 0.0) for x in logs]
    excls = [_dot(logs[h].astype(BF16), tri) for h in range(n)]
    ws = [jnp.exp((zs[h] - sps[h]) + (excls[h] + carries[h])) for h in range(n)]
    if mask is not None:
        ws = [jnp.where(mask, w, 0.0) for w in ws]
    outs = [_dot(ws[h].astype(BF16), vs[h]) for h in range(n)]
    new_carries = [carries[h] + jnp.sum(logs[h], axis=1, keepdims=True) for h in range(n)]
    return outs, new_carries


def _sb_kernel(q_ref, k_ref, v_ref, tri_ref, o_ref):
    i = pl.program_id(1)
    tq = q_ref.shape[0]
    nh = q_ref.shape[1] // HEAD_DIM
    near = tri_ref.shape[0]
    tri_near = tri_ref[...]
    tri = tri_near[:tq, :tq]
    q = q_ref[...]
    head = lambda x, h: x[:, HEAD_DIM * h:HEAD_DIM * (h + 1)]

    start = pl.multiple_of(jnp.maximum(i * tq - (near - tq), 0), tq)
    row = lax.broadcasted_iota(jnp.int32, (tq, near), 0)
    col = lax.broadcasted_iota(jnp.int32, (tq, near), 1)
    causal = col - row < i * tq - start
    k = k_ref[pl.ds(start, near), :]
    v = v_ref[pl.ds(start, near), :]
    qs = [head(q, h) for h in range(nh)]
    split = lambda x: [head(x, h) for h in range(nh)]
    outs, carries = _sb_blocks(qs, split(k), split(v), tri_near, [jnp.zeros((tq, 1), F32)] * nh, causal)

    def worst(cs):
        mx = jnp.max(cs[0])
        for c in cs[1:]:
            mx = jnp.maximum(mx, jnp.max(c))
        return mx

    def cond(st):
        return jnp.logical_and(st[0] >= 0, st[1] > SB_UNDERFLOW)

    def body(st):
        kb, _, outs, carries = st
        start = pl.multiple_of(kb * tq, tq)
        k = k_ref[pl.ds(start, tq), :]
        v = v_ref[pl.ds(start, tq), :]
        more, new_c = _sb_blocks(qs, split(k), split(v), tri, list(carries), None)
        return kb - 1, worst(new_c), tuple(outs[h] + more[h] for h in range(nh)), tuple(new_c)

    st = lax.while_loop(cond, body, (start // tq - 1, worst(carries), tuple(outs), tuple(carries)))
    for h in range(nh):
        o_ref[:, HEAD_DIM * h:HEAD_DIM * (h + 1)] = st[2][h].astype(BF16)


def _sb_attention(qkv):
    s = qkv.shape[0]
    d = N_HEADS * HEAD_DIM
    hps = 4
    ngrp = N_HEADS // hps
    w = hps * HEAD_DIM
    near = SB_NEAR + Q_TILE
    idx = jnp.arange(near)
    tri = (idx[:, None] > idx[None, :]).astype(BF16)
    return pl.pallas_call(
        _sb_kernel,
        grid=(ngrp, s // Q_TILE),
        in_specs=[pl.BlockSpec((Q_TILE, w), lambda h, i: (i, h)),
                  pl.BlockSpec((s, w), lambda h, i: (0, ngrp + h)),
                  pl.BlockSpec((s, w), lambda h, i: (0, 2 * ngrp + h)),
                  pl.BlockSpec((near, near), lambda h, i: (0, 0))],
        out_specs=pl.BlockSpec((Q_TILE, w), lambda h, i: (i, h)),
        out_shape=jax.ShapeDtypeStruct((s, d), BF16),
        compiler_params=_params(("arbitrary", "arbitrary")),
        name="sb_attention",
    )(qkv, qkv, qkv, tri)


def _mlp_kernel(final, tf, o_ref, res_ref, wo_ref, g_ref, w1_ref, w2_ref, *rest):
    out_ref = rest[-1]
    h = res_ref[...] + _dot(o_ref[...], wo_ref[...])
    xn = _rms(h, g_ref[...]).astype(BF16)
    out_ref[...] = h
    for j in range(w1_ref.shape[1] // tf):
        a = jnp.maximum(_dot(xn, w1_ref[:, tf * j:tf * (j + 1)]), 0.0)
        out_ref[...] += _dot((a * a).astype(BF16), w2_ref[tf * j:tf * (j + 1), :])
    if final:
        out_ref[...] = _rms(out_ref[...], rest[0][...])


def _attn_out_mlp(o, resid, wo, gain, w1, w2, final_gain=None):
    s, d = resid.shape
    f = w1.shape[1]
    tm, tf = 512, 512
    final = final_gain is not None
    row = lambda i: (i, 0)
    resident = lambda shape: pl.BlockSpec(shape, lambda i: (0, 0), pipeline_mode=pl.Buffered(1))
    in_specs = [pl.BlockSpec((tm, d), row), pl.BlockSpec((tm, d), row), resident((d, d)), resident((1, d)),
                resident((d, f)), resident((f, d))]
    args = [o, resid, wo, gain.reshape(1, d), w1, w2]
    if final:
        in_specs.append(resident((1, d)))
        args.append(final_gain.reshape(1, d))
    return pl.pallas_call(
        functools.partial(_mlp_kernel, final, tf),
        grid=(s // tm,),
        in_specs=in_specs,
        out_specs=pl.BlockSpec((tm, d), row),
        out_shape=jax.ShapeDtypeStruct((s, d), F32),
        compiler_params=_params(("arbitrary",)),
        name="attn_out_mlp",
    )(*args)


def _kv_kernel(x_ref, g_ref, w_ref, c_ref, s1_ref, s2_ref, cmp_ref, ks_ref, vs_ref, kw_ref, vw_ref):
    xn = _rms(x_ref[...], g_ref[...]).astype(BF16)
    acc = _dot(xn, w_ref[...])
    gw = KV_GROUPS * HEAD_DIM
    cmp_ref[...] = acc[:, :2 * gw]
    c, s1, s2 = c_ref[...], s1_ref[...], s2_ref[...]

    def put_k(ref, x):
        for p in range(gw // 128):
            xp = _rope128(x[:, 128 * p:128 * (p + 1)], c, s1, s2)
            ref[2 * p] = xp[:, :HEAD_DIM].astype(BF16)
            ref[2 * p + 1] = xp[:, HEAD_DIM:].astype(BF16)

    def put_vt(ref, x):
        xt = x.T
        extra = ref.shape[1] - HEAD_DIM
        if extra:
            ones_row = (lax.broadcasted_iota(jnp.int32, (extra, xt.shape[1]), 0) == 0).astype(BF16)
        for g in range(KV_GROUPS):
            ref[g, :HEAD_DIM, :] = xt[HEAD_DIM * g:HEAD_DIM * (g + 1)].astype(BF16)
            if extra:
                ref[g, HEAD_DIM:, :] = ones_row

    put_k(ks_ref, acc[:, 2 * gw:3 * gw])
    put_vt(vs_ref, acc[:, 3 * gw:4 * gw])
    put_k(kw_ref, acc[:, 4 * gw:5 * gw])
    put_vt(vw_ref, acc[:, 5 * gw:6 * gw])


def _kv_proj(h, gain, w, rope_tabs):
    s, d = h.shape
    n = w.shape[1]
    gw = KV_GROUPS * HEAD_DIM
    tm = 256
    row = lambda i: (i, 0)
    fixed = lambda i: (0, 0)
    k_spec = pl.BlockSpec((KV_GROUPS, tm, HEAD_DIM), lambda i: (0, i, 0))
    k_shape = jax.ShapeDtypeStruct((KV_GROUPS, s, HEAD_DIM), BF16)
    vt_spec = pl.BlockSpec((KV_GROUPS, HEAD_DIM, tm), lambda i: (0, 0, i))
    vt_shape = jax.ShapeDtypeStruct((KV_GROUPS, HEAD_DIM, s), BF16)
    return pl.pallas_call(
        _kv_kernel,
        grid=(s // tm,),
        in_specs=[pl.BlockSpec((tm, d), row), pl.BlockSpec((1, d), fixed), pl.BlockSpec((d, n), fixed),
                  pl.BlockSpec((tm, 128), row), pl.BlockSpec((tm, 128), row), pl.BlockSpec((tm, 128), row)],
        out_specs=[pl.BlockSpec((tm, 2 * gw), row), k_spec,
                   pl.BlockSpec((KV_GROUPS, V_ROWS, tm), lambda i: (0, 0, i)), k_spec, vt_spec],
        out_shape=[jax.ShapeDtypeStruct((s, 2 * gw), F32), k_shape,
                   jax.ShapeDtypeStruct((KV_GROUPS, V_ROWS, s), BF16), k_shape, vt_shape],
        compiler_params=_params(("arbitrary",)),
        name="kv_proj",
    )(h, gain.reshape(1, d), w, *rope_tabs)


def _cmp_kernel(n_cmp, x_ref, pos_ref, w1_ref, w2_ref, w2t_ref, o_ref, ot_ref):
    nch = x_ref.shape[0] // CMP_STRIDE
    half = CMP_STRIDE * HEAD_DIM
    w1 = w1_ref[...]
    bias = _dot(pos_ref[...].astype(BF16), w1)[0:1]
    acc = [[jnp.zeros((nch, CMP_HIDDEN), F32) for _ in range(2)] for _ in range(2)]
    for l in range(CMP_STRIDE):
        y = x_ref[pl.ds(l, nch, stride=CMP_STRIDE), :].astype(BF16)
        wa = w1[HEAD_DIM * l:HEAD_DIM * (l + 1)]
        wb = w1[half + HEAD_DIM * l:half + HEAD_DIM * (l + 1)]
        for gg in range(2):
            yg = y[:, HEAD_DIM * gg:HEAD_DIM * (gg + 1)]
            acc[gg][0] = acc[gg][0] + _dot(yg, wa)
            acc[gg][1] = acc[gg][1] + _dot(yg, wb)
    live = lax.broadcasted_iota(jnp.int32, (nch, HEAD_DIM), 0) < n_cmp
    live_t = lax.broadcasted_iota(jnp.int32, (HEAD_DIM, nch), 1) < n_cmp
    for gg in range(2):
        hid = acc[gg][0] + pltpu.roll(acc[gg][1], nch - 1, axis=0) + bias
        hid = jax.nn.gelu(hid, approximate=True).astype(BF16)
        o_ref[gg] = jnp.where(live, _dot(hid, w2_ref[...]), 0.0).astype(BF16)
        ot_ref[gg] = jnp.where(live_t, _dot_nt(w2t_ref[...], hid), 0.0).astype(BF16)


def _cmp_mlp(cmp_kv, pos_flat, w1, w2, n_cmp):
    s = cmp_kv.shape[0]
    nch = s // CMP_STRIDE
    feat = CMP_LEN * HEAD_DIM
    return pl.pallas_call(
        functools.partial(_cmp_kernel, n_cmp),
        grid=(2, KV_GROUPS // 2),
        in_specs=[pl.BlockSpec((s, 2 * HEAD_DIM), lambda c, p: (0, c * (KV_GROUPS // 2) + p)),
                  pl.BlockSpec((None, 8, feat), lambda c, p: (c, 0, 0)),
                  pl.BlockSpec((None, feat, CMP_HIDDEN), lambda c, p: (c, 0, 0)),
                  pl.BlockSpec((None, CMP_HIDDEN, HEAD_DIM), lambda c, p: (c, 0, 0)),
                  pl.BlockSpec((None, HEAD_DIM, CMP_HIDDEN), lambda c, p: (c, 0, 0))],
        out_specs=[pl.BlockSpec((None, 2, nch, HEAD_DIM), lambda c, p: (c, p, 0, 0)),
                   pl.BlockSpec((None, 2, HEAD_DIM, nch), lambda c, p: (c, p, 0, 0))],
        out_shape=[jax.ShapeDtypeStruct((2, KV_GROUPS, nch, HEAD_DIM), BF16),
                   jax.ShapeDtypeStruct((2, KV_GROUPS, HEAD_DIM, nch), BF16)],
        compiler_params=_params(("arbitrary", "arbitrary")),
        name="cmp_mlp",
    )(cmp_kv, pos_flat, w1, w2, jnp.swapaxes(w2, 1, 2))


def _q_kernel(x_ref, g_ref, w_ref, b_ref, c_ref, s1_ref, s2_ref, q_ref, qr_ref, gate_ref):
    xn = _rms(x_ref[...], g_ref[...]).astype(BF16)
    acc = _dot(xn, w_ref[...])
    d = q_ref.shape[1]
    c, s1, s2 = c_ref[...], s1_ref[...], s2_ref[...]
    for p in range(d // 128):
        qp = acc[:, 128 * p:128 * (p + 1)] * (HEAD_DIM ** -0.5 * LOG2E)
        q_ref[:, 128 * p:128 * (p + 1)] = qp.astype(BF16)
        qr_ref[:, 128 * p:128 * (p + 1)] = _rope128(qp, c, s1, s2).astype(BF16)
    gates_t = (1.0 / (1.0 + jnp.exp(-(acc[:, d:] + b_ref[...])))).T
    for g in range(KV_GROUPS):
        gate_ref[g] = gates_t[GATE_ROWS * g:GATE_ROWS * (g + 1)]


def _q_proj(h, gain, w, gate_b, rope_tabs):
    s, d = h.shape
    n = w.shape[1]
    tm = 256
    row = lambda i: (i, 0)
    fixed = lambda i: (0, 0)
    return pl.pallas_call(
        _q_kernel,
        grid=(s // tm,),
        in_specs=[pl.BlockSpec((tm, d), row), pl.BlockSpec((1, d), fixed), pl.BlockSpec((d, n), fixed),
                  pl.BlockSpec((1, 128), fixed),
                  pl.BlockSpec((tm, 128), row), pl.BlockSpec((tm, 128), row), pl.BlockSpec((tm, 128), row)],
        out_specs=[pl.BlockSpec((tm, d), row), pl.BlockSpec((tm, d), row),
                   pl.BlockSpec((KV_GROUPS, GATE_ROWS, tm), lambda i: (0, 0, i))],
        out_shape=[jax.ShapeDtypeStruct((s, d), BF16), jax.ShapeDtypeStruct((s, d), BF16),
                   jax.ShapeDtypeStruct((KV_GROUPS, GATE_ROWS, s), F32)],
        compiler_params=_params(("arbitrary",)),
        name="nsa_q_proj",
    )(h, gain.reshape(1, d), w, gate_b, *rope_tabs)


def _group_gate_columns(a):
    lead = a.shape[:-1]
    a = a.reshape(lead + (3, KV_GROUPS, HEADS_PER_GROUP))
    a = jnp.moveaxis(a, -2, -3).reshape(lead + (KV_GROUPS, 3 * HEADS_PER_GROUP))
    a = jnp.pad(a, [(0, 0)] * len(lead) + [(0, 0), (0, GATE_ROWS - 3 * HEADS_PER_GROUP)])
    a = a.reshape(lead + (KV_GROUPS * GATE_ROWS,))
    return jnp.pad(a, [(0, 0)] * len(lead) + [(0, 128 - KV_GROUPS * GATE_ROWS)])


def _stack_heads(q):
    return jnp.concatenate([q[:, HEAD_DIM * h:HEAD_DIM * (h + 1)] for h in range(HEADS_PER_GROUP)], axis=0)


def _head_cols(x, h):
    return x[:, Q_TILE * h:Q_TILE * (h + 1)]


def _masked_softmax_t(st, valid):
    es, sums = [], []
    for h in range(HEADS_PER_GROUP):
        s = jnp.where(valid, _head_cols(st, h), NEG)
        e = jnp.exp2(s - jnp.max(s, axis=0, keepdims=True))
        es.append(e)
        sums.append(jnp.sum(e, axis=0, keepdims=True))
    return es, sums


def _nsa_kernel(tk, q_ref, qr_ref, kc_ref, vct_ref, ovt_ref, ks_ref, vst_ref, kw_ref, vwt_ref, gate_ref,
                o_ref, ocmp_ref, owin_ref, bias_ref, s0_ref, s1_ref, mx0_ref, mx1_ref, m_ref, acc_ref):
    i = pl.program_id(1)
    tq = q_ref.shape[0]
    nc = kc_ref.shape[0]
    n_sel = ovt_ref.shape[0]
    hp = HEADS_PER_GROUP
    q4 = _stack_heads(q_ref[...])
    qr4 = _stack_heads(qr_ref[...])
    tok = i * tq + lax.broadcasted_iota(jnp.int32, (1, tq), 1)
    gates = gate_ref[...]

    q0 = pl.multiple_of(i * tq, tq)
    span = WINDOW + tq
    wstart = pl.multiple_of(jnp.maximum(i * tq - WINDOW, 0), tq)

    def cmp_and_select(ncv, rows):
        sc = _dot_nt(kc_ref[:ncv, :], q4)
        sw = _dot_nt(kw_ref[pl.ds(wstart, span), :], qr4)
        sd = _dot_nt(ks_ref[pl.ds(q0, tq), :], qr4)

        n_idx = lax.broadcasted_iota(jnp.int32, (ncv, tq), 0)
        valid_c = n_idx * CMP_STRIDE + (CMP_LEN - 1) <= tok
        es, sums = _masked_softmax_t(sc, valid_c)
        diff = tok - (wstart + lax.broadcasted_iota(jnp.int32, (span, tq), 0))
        ew, wsums = _masked_softmax_t(sw, jnp.logical_and(diff >= 0, diff < WINDOW))
        not_future = lax.broadcasted_iota(jnp.int32, (tq, tq), 0) <= lax.broadcasted_iota(jnp.int32, (tq, tq), 1)
        sd = jnp.concatenate([jnp.where(not_future, _head_cols(sd, h), NEG) for h in range(hp)], axis=1)
        m0 = jnp.max(sd, axis=0, keepdims=True)
        pd = jnp.exp2(sd - m0)
        any_c = tok >= CMP_LEN - 1
        ps = [es[h] * jnp.where(any_c, 1.0 / sums[h], 0.0) for h in range(hp)]

        ocmp_ref[...] = _dot(vct_ref[:, :ncv], jnp.concatenate(ps, axis=1).astype(BF16))
        o_win = _dot(vwt_ref[:, pl.ds(wstart, span)], jnp.concatenate(ew, axis=1).astype(BF16))
        owin_ref[...] = o_win * jnp.concatenate([1.0 / w for w in wsums], axis=1)
        m_ref[...] = m0
        acc_ref[...] = _dot(vst_ref[:, pl.ds(q0, tq)], pd.astype(BF16))

        psum = ps[0] + ps[1] + ps[2] + ps[3]
        ovt = ovt_ref[:rows, :ncv]
        imp = sum(_dot(ovt, term) for term in _split3(psum))

        m_idx = lax.broadcasted_iota(jnp.int32, (rows, tq), 0)
        blk_t = tok // SEL_LEN
        forced = (m_idx == 0) | (m_idx == blk_t) | (m_idx == blk_t - 1)
        causal = m_idx <= blk_t
        imp = jnp.where(jnp.logical_and(causal, jnp.logical_not(forced)), imp, -jnp.inf)

        def pick(_, st):
            v, sel = st
            mx = jnp.max(v, axis=0, keepdims=True)
            idx = jnp.min(jnp.where(v == mx, m_idx, rows), axis=0, keepdims=True)
            hit = m_idx == idx
            return jnp.where(hit, -jnp.inf, v), jnp.where(hit, 1.0, sel)

        _, sel = lax.fori_loop(0, SEL_TOPK - 3, pick, (imp, jnp.where(forced, 1.0, 0.0)))
        before = m_idx < (i * tq) // SEL_LEN
        bias_ref[:rows, :] = jnp.where(jnp.logical_and(sel > 0.5, before), 0.0, NEG)
        if rows < n_sel:
            bias_ref[rows:, :] = jnp.full((n_sel - rows, tq), NEG, F32)

    rows_per_chunk = CMP_CHUNK * CMP_STRIDE // SEL_LEN
    n_prefix = -(-nc // CMP_CHUNK)
    need = ((i + 1) * tq - 1) // (CMP_STRIDE * CMP_CHUNK) + 1
    for c in range(1, n_prefix + 1):
        pl.when(need == c)(functools.partial(cmp_and_select, min(c * CMP_CHUNK, nc), min(c * rows_per_chunk, n_sel)))
    bpt = tk // SEL_LEN

    def sel_scores(kt, s_ref, mx_ref):
        start = pl.multiple_of(kt * tk, tk)
        st = _dot_nt(ks_ref[pl.ds(start, tk), :], qr4)
        rows = bias_ref[pl.ds(pl.multiple_of(kt * bpt, bpt), bpt), :]
        bias = jnp.concatenate([jnp.broadcast_to(rows[j:j + 1], (SEL_LEN, tq)) for j in range(bpt)], axis=0)
        s = jnp.concatenate([_head_cols(st, h) + bias for h in range(hp)], axis=1)
        s_ref[...] = s
        mx_ref[...] = jnp.max(s, axis=0, keepdims=True)

    def sel_accumulate(kt, s_ref, mx_ref):
        start = pl.multiple_of(kt * tk, tk)
        m = m_ref[...]
        m_new = jnp.maximum(m, mx_ref[...])
        p = jnp.exp2(s_ref[...] - m_new)
        acc_ref[...] = jnp.exp2(m - m_new) * acc_ref[...] + _dot(vst_ref[:, pl.ds(start, tk)], p.astype(BF16))
        m_ref[...] = m_new

    def sel_pair(j, carry):
        sel_scores(2 * j + 1, s1_ref, mx1_ref)
        sel_accumulate(2 * j, s0_ref, mx0_ref)
        sel_scores(2 * j + 2, s0_ref, mx0_ref)
        sel_accumulate(2 * j + 1, s1_ref, mx1_ref)
        return carry

    last = jnp.maximum(i * tq - 1, 0) // tk
    sel_scores(0, s0_ref, mx0_ref)
    pairs = last // 2
    lax.fori_loop(0, pairs, sel_pair, 0)
    odd = last % 2 == 1

    @pl.when(odd)
    def _():
        sel_scores(last, s1_ref, mx1_ref)

    sel_accumulate(2 * pairs, s0_ref, mx0_ref)

    @pl.when(odd)
    def _():
        sel_accumulate(last, s1_ref, mx1_ref)

    acc = acc_ref[...]
    o_sel = acc[:HEAD_DIM] * (1.0 / acc[HEAD_DIM:HEAD_DIM + 1])

    o_cmp = ocmp_ref[...]
    o_win = owin_ref[...]
    for h in range(hp):
        mix = (gates[h:h + 1] * _head_cols(o_cmp, h)
               + gates[hp + h:hp + h + 1] * _head_cols(o_sel, h)
               + gates[2 * hp + h:2 * hp + h + 1] * _head_cols(o_win, h))
        o_ref[:, HEAD_DIM * h:HEAD_DIM * (h + 1)] = mix.T.astype(BF16)


def _nsa_attention(q, q_rot, k_cmp, v_cmp_t, overlap_t, k_slc, v_slc_t, k_win, v_win_t, gates):
    s, d = q.shape
    nc = k_cmp.shape[1]
    n_sel = overlap_t.shape[0]
    gw = HEADS_PER_GROUP * HEAD_DIM
    cols = HEADS_PER_GROUP * Q_TILE
    tk = 512
    qspec = pl.BlockSpec((Q_TILE, gw), lambda g, i: (i, g))
    per_group = lambda shape: pl.BlockSpec((None,) + shape, lambda g, i: (g, 0, 0))
    return pl.pallas_call(
        functools.partial(_nsa_kernel, tk),
        grid=(KV_GROUPS, s // Q_TILE),
        in_specs=[qspec, qspec,
                  per_group((nc, HEAD_DIM)), per_group((HEAD_DIM, nc)),
                  pl.BlockSpec((n_sel, nc), lambda g, i: (0, 0)),
                  per_group((s, HEAD_DIM)), per_group((V_ROWS, s)),
                  per_group((s, HEAD_DIM)), per_group((HEAD_DIM, s)),
                  pl.BlockSpec((None, GATE_ROWS, Q_TILE), lambda g, i: (g, 0, i))],
        out_specs=qspec,
        out_shape=jax.ShapeDtypeStruct((s, d), BF16),
        scratch_shapes=[pltpu.VMEM((HEAD_DIM, cols), F32), pltpu.VMEM((HEAD_DIM, cols), F32),
                        pltpu.VMEM((n_sel, Q_TILE), F32),
                        pltpu.VMEM((tk, cols), F32), pltpu.VMEM((tk, cols), F32),
                        pltpu.VMEM((1, cols), F32), pltpu.VMEM((1, cols), F32),
                        pltpu.VMEM((1, cols), F32), pltpu.VMEM((V_ROWS, cols), F32)],
        compiler_params=_params(("arbitrary", "arbitrary")),
        name="nsa_attention",
    )(q, q_rot, k_cmp, v_cmp_t, overlap_t, k_slc, v_slc_t, k_win, v_win_t, gates)


def _rope_tables(s):
    half = ROT_DIM // 2
    inv_freq = ROPE_THETA ** (-jnp.arange(half, dtype=F32) * 2.0 / ROT_DIM)
    ang = jnp.arange(s, dtype=F32)[:, None] * inv_freq[None, :]
    cos, sin = jnp.cos(ang), jnp.sin(ang)
    rest = HEAD_DIM - ROT_DIM
    c = jnp.concatenate([cos, cos, jnp.ones((s, rest), F32)], axis=1)
    s1 = jnp.concatenate([-sin, jnp.zeros((s, half + rest), F32)], axis=1)
    s2 = jnp.concatenate([jnp.zeros((s, half), F32), sin, jnp.zeros((s, rest), F32)], axis=1)
    return tuple(jnp.tile(a, (1, 2)) for a in (c, s1, s2))


def _overlap_matrix(nch, n_sel):
    cmp_start = jnp.arange(nch)[:, None] * CMP_STRIDE
    sel_start = jnp.arange(n_sel)[None, :] * SEL_LEN
    return ((cmp_start < sel_start + SEL_LEN) & (cmp_start + CMP_LEN > sel_start)).astype(BF16)


def kernel(x, norm_gain, sb_w_qkv, sb_w_o, kv_norm, nsa_w_kv, cmp_pos, cmp_w1, cmp_w2,
           nsa_w_q, nsa_gate_b, nsa_w_o, mlp_w1, mlp_w2, final_norm):
    b, s, d = x.shape
    assert b == 1 and d == N_HEADS * HEAD_DIM
    assert s % 512 == 0 and s >= WINDOW + Q_TILE and s // SEL_LEN >= SEL_TOPK
    n_cmp = (s - CMP_LEN) // CMP_STRIDE + 1
    nch = s // CMP_STRIDE
    n_sel = s // SEL_LEN
    h0 = x[0]
    rope_tabs = _rope_tables(s)

    qkv = _qkv_proj(h0, norm_gain[0, 0], sb_w_qkv[0].astype(BF16))
    o_sb = _sb_attention(qkv)
    h1 = _attn_out_mlp(o_sb, h0, sb_w_o[0].astype(BF16), norm_gain[0, 1],
                       mlp_w1[0].astype(BF16), mlp_w2[0].astype(BF16))

    cmp_kv, k_slc, v_slc_t, k_win, v_win_t = _kv_proj(h1, kv_norm, nsa_w_kv.astype(BF16), rope_tabs)
    pos_flat = jnp.broadcast_to(cmp_pos.reshape(2, 1, CMP_LEN * HEAD_DIM), (2, 8, CMP_LEN * HEAD_DIM))
    kv_cmp, kv_cmp_t = _cmp_mlp(cmp_kv, pos_flat, cmp_w1.astype(BF16), cmp_w2.astype(BF16), n_cmp)

    n_qk = N_HEADS * HEAD_DIM
    w_q = jnp.concatenate([nsa_w_q[0][:, :n_qk], _group_gate_columns(nsa_w_q[0][:, n_qk:])], axis=1).astype(BF16)
    gate_b = _group_gate_columns(nsa_gate_b[0]).reshape(1, 128)
    q, q_rot, gates = _q_proj(h1, norm_gain[1, 0], w_q, gate_b, rope_tabs)
    o_nsa = _nsa_attention(q, q_rot, kv_cmp[0], kv_cmp_t[1], _overlap_matrix(nch, n_sel).T,
                           k_slc, v_slc_t, k_win, v_win_t, gates)
    out = _attn_out_mlp(o_nsa, h1, nsa_w_o[0].astype(BF16), norm_gain[1, 1],
                        mlp_w1[1].astype(BF16), mlp_w2[1].astype(BF16), final_gain=final_norm)
    return out[None]
```

```python
import functools

import jax
import jax.numpy as jnp
from jax import lax
from jax.experimental import pallas as pl
from jax.experimental.pallas import tpu as pltpu

HEAD_DIM = 64
N_HEADS = 16
KV_GROUPS = 4
HEADS_PER_GROUP = N_HEADS // KV_GROUPS
CMP_LEN = 32
CMP_STRIDE = 16
CMP_HIDDEN = 256
SEL_LEN = 64
SEL_TOPK = 16
WINDOW = 512
ROPE_THETA = 500000.0
ROT_DIM = HEAD_DIM // 4
Q_TILE = 128
GATE_ROWS = 16
V_ROWS = HEAD_DIM + 16
LOG2E = 1.4426950408889634
CMP_CHUNK = 256
NORM_EPS = 1e-5
NEG = -1e30
FORCED_SCORE = 1e6
SB_UNDERFLOW = -110.0
SB_NEAR = 256
VMEM_LIMIT = 56 * 1024 * 1024

BF16 = jnp.bfloat16
F32 = jnp.float32


def _params(semantics, vmem=VMEM_LIMIT):
    return pltpu.CompilerParams(dimension_semantics=semantics, vmem_limit_bytes=vmem)


def _rms(x, g):
    return x * lax.rsqrt(jnp.mean(x * x, axis=-1, keepdims=True) + NORM_EPS) * g


def _dot(a, b):
    return jnp.dot(a, b, preferred_element_type=F32)


def _dot_nt(a, b):
    return lax.dot_general(a, b, (((1,), (1,)), ((), ())), preferred_element_type=F32)


def _rope128(x, c, s1, s2):
    return x * c + pltpu.roll(x, 128 - ROT_DIM // 2, axis=1) * s1 + pltpu.roll(x, ROT_DIM // 2, axis=1) * s2


def _split3(x):
    hi = x.astype(BF16)
    r = x - hi.astype(F32)
    mid = r.astype(BF16)
    lo = (r - mid.astype(F32)).astype(BF16)
    return hi, mid, lo


def _qkv_kernel(x_ref, g_ref, w_ref, o_ref):
    xn = _rms(x_ref[...], g_ref[...]).astype(BF16)
    acc = _dot(xn, w_ref[...])
    d = x_ref.shape[1]
    o_ref[:, :d] = (acc[:, :d] * (HEAD_DIM ** -0.5)).astype(BF16)
    o_ref[:, d:] = acc[:, d:].astype(BF16)


def _qkv_proj(x, gain, w):
    s, d = x.shape
    n = w.shape[1]
    tm = 256
    return pl.pallas_call(
        _qkv_kernel,
        grid=(s // tm,),
        in_specs=[pl.BlockSpec((tm, d), lambda i: (i, 0)),
                  pl.BlockSpec((1, d), lambda i: (0, 0)),
                  pl.BlockSpec((d, n), lambda i: (0, 0))],
        out_specs=pl.BlockSpec((tm, n), lambda i: (i, 0)),
        out_shape=jax.ShapeDtypeStruct((s, n), BF16),
        compiler_params=_params(("arbitrary",)),
        name="qkv_proj",
    )(x, gain.reshape(1, d), w)


def _sb_blocks(qs, ks, vs, tri, carries, mask):
    n = len(qs)
    zs = [_dot_nt(qs[h], ks[h]) for h in range(n)]
    sps = [jnp.maximum(z, 0.0) + jnp.log(1.0 + jnp.exp(-jnp.abs(z))) for z in zs]
    logs = [-sp for sp in sps]
    if mask is not None:
        logs = [jnp.where(mask, x, 0.0) for x in logs]
    excls = [_dot(logs[h].astype(BF16), tri) for h in range(n)]
    ws = [jnp.exp((zs[h] - sps[h]) + (excls[h] + carries[h])) for h in range(n)]
    if mask is not None:
        ws = [jnp.where(mask, w, 0.0) for w in ws]
    outs = [_dot(ws[h].astype(BF16), vs[h]) for h in range(n)]
    new_carries = [carries[h] + jnp.sum(logs[h], axis=1, keepdims=True) for h in range(n)]
    return outs, new_carries


def _sb_kernel(q_ref, k_ref, v_ref, tri_ref, o_ref):
    i = pl.program_id(1)
    tq = q_ref.shape[0]
    nh = q_ref.shape[1] // HEAD_DIM
    near = tri_ref.shape[0]
    tri_near = tri_ref[...]
    tri = tri_near[:tq, :tq]
    q = q_ref[...]
    head = lambda x, h: x[:, HEAD_DIM * h:HEAD_DIM * (h + 1)]

    start = pl.multiple_of(jnp.maximum(i * tq - (near - tq), 0), tq)
    row = lax.broadcasted_iota(jnp.int32, (tq, near), 0)
    col = lax.broadcasted_iota(jnp.int32, (tq, near), 1)
    causal = col - row < i * tq - start
    k = k_ref[pl.ds(start, near), :]
    v = v_ref[pl.ds(start, near), :]
    qs = [head(q, h) for h in range(nh)]
    split = lambda x: [head(x, h) for h in range(nh)]
    outs, carries = _sb_blocks(qs, split(k), split(v), tri_near, [jnp.zeros((tq, 1), F32)] * nh, causal)

    def worst(cs):
        mx = jnp.max(cs[0])
        for c in cs[1:]:
            mx = jnp.maximum(mx, jnp.max(c))
        return mx

    def cond(st):
        return jnp.logical_and(st[0] >= 0, st[1] > SB_UNDERFLOW)

    def body(st):
        kb, _, outs, carries = st
        start = pl.multiple_of(kb * tq, tq)
        k = k_ref[pl.ds(start, tq), :]
        v = v_ref[pl.ds(start, tq), :]
        more, new_c = _sb_blocks(qs, split(k), split(v), tri, list(carries), None)
        return kb - 1, worst(new_c), tuple(outs[h] + more[h] for h in range(nh)), tuple(new_c)

    st = lax.while_loop(cond, body, (start // tq - 1, worst(carries), tuple(outs), tuple(carries)))
    for h in range(nh):
        o_ref[:, HEAD_DIM * h:HEAD_DIM * (h + 1)] = st[2][h].astype(BF16)


def _sb_attention(qkv):
    s = qkv.shape[0]
    d = N_HEADS * HEAD_DIM
    hps = 8
    ngrp = N_HEADS // hps
    w = hps * HEAD_DIM
    near = SB_NEAR + Q_TILE
    idx = jnp.arange(near)
    tri = (idx[:, None] > idx[None, :]).astype(BF16)
    return pl.pallas_call(
        _sb_kernel,
        grid=(ngrp, s // Q_TILE),
        in_specs=[pl.BlockSpec((Q_TILE, w), lambda h, i: (i, h)),
                  pl.BlockSpec((s, w), lambda h, i: (0, ngrp + h), pipeline_mode=pl.Buffered(1)),
                  pl.BlockSpec((s, w), lambda h, i: (0, 2 * ngrp + h), pipeline_mode=pl.Buffered(1)),
                  pl.BlockSpec((near, near), lambda h, i: (0, 0))],
        out_specs=pl.BlockSpec((Q_TILE, w), lambda h, i: (i, h)),
        out_shape=jax.ShapeDtypeStruct((s, d), BF16),
        compiler_params=_params(("arbitrary", "arbitrary")),
        name="sb_attention",
    )(qkv, qkv, qkv, tri)


def _mlp_kernel(final, tf, o_ref, res_ref, wo_ref, g_ref, w1_ref, w2_ref, *rest):
    out_ref = rest[-1]
    h = res_ref[...] + _dot(o_ref[...], wo_ref[...])
    xn = _rms(h, g_ref[...]).astype(BF16)
    out_ref[...] = h
    for j in range(w1_ref.shape[1] // tf):
        a = jnp.maximum(_dot(xn, w1_ref[:, tf * j:tf * (j + 1)]), 0.0)
        out_ref[...] += _dot((a * a).astype(BF16), w2_ref[tf * j:tf * (j + 1), :])
    if final:
        out_ref[...] = _rms(out_ref[...], rest[0][...])


def _attn_out_mlp(o, resid, wo, gain, w1, w2, final_gain=None):
    s, d = resid.shape
    f = w1.shape[1]
    tm, tf = 512, 512
    final = final_gain is not None
    row = lambda i: (i, 0)
    resident = lambda shape: pl.BlockSpec(shape, lambda i: (0, 0), pipeline_mode=pl.Buffered(1))
    in_specs = [pl.BlockSpec((tm, d), row), pl.BlockSpec((tm, d), row), resident((d, d)), resident((1, d)),
                resident((d, f)), resident((f, d))]
    args = [o, resid, wo, gain.reshape(1, d), w1, w2]
    if final:
        in_specs.append(resident((1, d)))
        args.append(final_gain.reshape(1, d))
    return pl.pallas_call(
        functools.partial(_mlp_kernel, final, tf),
        grid=(s // tm,),
        in_specs=in_specs,
        out_specs=pl.BlockSpec((tm, d), row),
        out_shape=jax.ShapeDtypeStruct((s, d), F32),
        compiler_params=_params(("arbitrary",)),
        name="attn_out_mlp",
    )(*args)


def _kv_kernel(x_ref, g_ref, w_ref, c_ref, s1_ref, s2_ref, cmp_ref, ks_ref, vs_ref, kw_ref, vw_ref):
    xn = _rms(x_ref[...], g_ref[...]).astype(BF16)
    acc = _dot(xn, w_ref[...])
    gw = KV_GROUPS * HEAD_DIM
    cmp_ref[...] = acc[:, :2 * gw]
    c, s1, s2 = c_ref[...], s1_ref[...], s2_ref[...]

    def put_k(ref, x):
        for p in range(gw // 128):
            xp = _rope128(x[:, 128 * p:128 * (p + 1)], c, s1, s2)
            ref[2 * p] = xp[:, :HEAD_DIM].astype(BF16)
            ref[2 * p + 1] = xp[:, HEAD_DIM:].astype(BF16)

    def put_vt(ref, x):
        xt = x.T
        extra = ref.shape[1] - HEAD_DIM
        if extra:
            ones_row = (lax.broadcasted_iota(jnp.int32, (extra, xt.shape[1]), 0) == 0).astype(BF16)
        for g in range(KV_GROUPS):
            ref[g, :HEAD_DIM, :] = xt[HEAD_DIM * g:HEAD_DIM * (g + 1)].astype(BF16)
            if extra:
                ref[g, HEAD_DIM:, :] = ones_row

    put_k(ks_ref, acc[:, 2 * gw:3 * gw])
    put_vt(vs_ref, acc[:, 3 * gw:4 * gw])
    put_k(kw_ref, acc[:, 4 * gw:5 * gw])
    put_vt(vw_ref, acc[:, 5 * gw:6 * gw])


def _kv_proj(h, gain, w, rope_tabs):
    s, d = h.shape
    n = w.shape[1]
    gw = KV_GROUPS * HEAD_DIM
    tm = 256
    row = lambda i: (i, 0)
    fixed = lambda i: (0, 0)
    k_spec = pl.BlockSpec((KV_GROUPS, tm, HEAD_DIM), lambda i: (0, i, 0))
    k_shape = jax.ShapeDtypeStruct((KV_GROUPS, s, HEAD_DIM), BF16)
    vt_spec = pl.BlockSpec((KV_GROUPS, HEAD_DIM, tm), lambda i: (0, 0, i))
    vt_shape = jax.ShapeDtypeStruct((KV_GROUPS, HEAD_DIM, s), BF16)
    return pl.pallas_call(
        _kv_kernel,
        grid=(s // tm,),
        in_specs=[pl.BlockSpec((tm, d), row), pl.BlockSpec((1, d), fixed), pl.BlockSpec((d, n), fixed),
                  pl.BlockSpec((tm, 128), row), pl.BlockSpec((tm, 128), row), pl.BlockSpec((tm, 128), row)],
        out_specs=[pl.BlockSpec((tm, 2 * gw), row), k_spec,
                   pl.BlockSpec((KV_GROUPS, V_ROWS, tm), lambda i: (0, 0, i)), k_spec, vt_spec],
        out_shape=[jax.ShapeDtypeStruct((s, 2 * gw), F32), k_shape,
                   jax.ShapeDtypeStruct((KV_GROUPS, V_ROWS, s), BF16), k_shape, vt_shape],
        compiler_params=_params(("arbitrary",)),
        name="kv_proj",
    )(h, gain.reshape(1, d), w, *rope_tabs)


def _cmp_kernel(n_cmp, x_ref, pos_ref, w1_ref, w2_ref, w2t_ref, o_ref, ot_ref):
    nch = x_ref.shape[0] // CMP_STRIDE
    half = CMP_STRIDE * HEAD_DIM
    w1 = w1_ref[...]
    bias = _dot(pos_ref[...].astype(BF16), w1)[0:1]
    acc = [[jnp.zeros((nch, CMP_HIDDEN), F32) for _ in range(2)] for _ in range(2)]
    for l in range(CMP_STRIDE):
        y = x_ref[pl.ds(l, nch, stride=CMP_STRIDE), :].astype(BF16)
        wa = w1[HEAD_DIM * l:HEAD_DIM * (l + 1)]
        wb = w1[half + HEAD_DIM * l:half + HEAD_DIM * (l + 1)]
        for gg in range(2):
            yg = y[:, HEAD_DIM * gg:HEAD_DIM * (gg + 1)]
            acc[gg][0] = acc[gg][0] + _dot(yg, wa)
            acc[gg][1] = acc[gg][1] + _dot(yg, wb)
    live = lax.broadcasted_iota(jnp.int32, (nch, HEAD_DIM), 0) < n_cmp
    live_t = lax.broadcasted_iota(jnp.int32, (HEAD_DIM, nch), 1) < n_cmp
    for gg in range(2):
        hid = acc[gg][0] + pltpu.roll(acc[gg][1], nch - 1, axis=0) + bias
        hid = jax.nn.gelu(hid, approximate=True).astype(BF16)
        o_ref[gg] = jnp.where(live, _dot(hid, w2_ref[...]), 0.0).astype(BF16)
        ot_ref[gg] = jnp.where(live_t, _dot_nt(w2t_ref[...], hid), 0.0).astype(BF16)


def _cmp_mlp(cmp_kv, pos_flat, w1, w2, n_cmp):
    s = cmp_kv.shape[0]
    nch = s // CMP_STRIDE
    feat = CMP_LEN * HEAD_DIM
    return pl.pallas_call(
        functools.partial(_cmp_kernel, n_cmp),
        grid=(2, KV_GROUPS // 2),
        in_specs=[pl.BlockSpec((s, 2 * HEAD_DIM), lambda c, p: (0, c * (KV_GROUPS // 2) + p)),
                  pl.BlockSpec((None, 8, feat), lambda c, p: (c, 0, 0)),
                  pl.BlockSpec((None, feat, CMP_HIDDEN), lambda c, p: (c, 0, 0)),
                  pl.BlockSpec((None, CMP_HIDDEN, HEAD_DIM), lambda c, p: (c, 0, 0)),
                  pl.BlockSpec((None, HEAD_DIM, CMP_HIDDEN), lambda c, p: (c, 0, 0))],
        out_specs=[pl.BlockSpec((None, 2, nch, HEAD_DIM), lambda c, p: (c, p, 0, 0)),
                   pl.BlockSpec((None, 2, HEAD_DIM, nch), lambda c, p: (c, p, 0, 0))],
        out_shape=[jax.ShapeDtypeStruct((2, KV_GROUPS, nch, HEAD_DIM), BF16),
                   jax.ShapeDtypeStruct((2, KV_GROUPS, HEAD_DIM, nch), BF16)],
        compiler_params=_params(("arbitrary", "arbitrary")),
        name="cmp_mlp",
    )(cmp_kv, pos_flat, w1, w2, jnp.swapaxes(w2, 1, 2))


def _q_kernel(x_ref, g_ref, w_ref, b_ref, c_ref, s1_ref, s2_ref, q_ref, qr_ref, gate_ref):
    xn = _rms(x_ref[...], g_ref[...]).astype(BF16)
    acc = _dot(xn, w_ref[...])
    d = q_ref.shape[1]
    c, s1, s2 = c_ref[...], s1_ref[...], s2_ref[...]
    for p in range(d // 128):
        qp = acc[:, 128 * p:128 * (p + 1)] * (HEAD_DIM ** -0.5 * LOG2E)
        q_ref[:, 128 * p:128 * (p + 1)] = qp.astype(BF16)
        qr_ref[:, 128 * p:128 * (p + 1)] = _rope128(qp, c, s1, s2).astype(BF16)
    gates_t = (1.0 / (1.0 + jnp.exp(-(acc[:, d:] + b_ref[...])))).T
    for g in range(KV_GROUPS):
        gate_ref[g] = gates_t[GATE_ROWS * g:GATE_ROWS * (g + 1)]


def _q_proj(h, gain, w, gate_b, rope_tabs):
    s, d = h.shape
    n = w.shape[1]
    tm = 256
    row = lambda i: (i, 0)
    fixed = lambda i: (0, 0)
    return pl.pallas_call(
        _q_kernel,
        grid=(s // tm,),
        in_specs=[pl.BlockSpec((tm, d), row), pl.BlockSpec((1, d), fixed), pl.BlockSpec((d, n), fixed),
                  pl.BlockSpec((1, 128), fixed),
                  pl.BlockSpec((tm, 128), row), pl.BlockSpec((tm, 128), row), pl.BlockSpec((tm, 128), row)],
        out_specs=[pl.BlockSpec((tm, d), row), pl.BlockSpec((tm, d), row),
                   pl.BlockSpec((KV_GROUPS, GATE_ROWS, tm), lambda i: (0, 0, i))],
        out_shape=[jax.ShapeDtypeStruct((s, d), BF16), jax.ShapeDtypeStruct((s, d), BF16),
                   jax.ShapeDtypeStruct((KV_GROUPS, GATE_ROWS, s), F32)],
        compiler_params=_params(("arbitrary",)),
        name="nsa_q_proj",
    )(h, gain.reshape(1, d), w, gate_b, *rope_tabs)


def _group_gate_columns(a):
    lead = a.shape[:-1]
    a = a.reshape(lead + (3, KV_GROUPS, HEADS_PER_GROUP))
    a = jnp.moveaxis(a, -2, -3).reshape(lead + (KV_GROUPS, 3 * HEADS_PER_GROUP))
    a = jnp.pad(a, [(0, 0)] * len(lead) + [(0, 0), (0, GATE_ROWS - 3 * HEADS_PER_GROUP)])
    a = a.reshape(lead + (KV_GROUPS * GATE_ROWS,))
    return jnp.pad(a, [(0, 0)] * len(lead) + [(0, 128 - KV_GROUPS * GATE_ROWS)])


def _stack_heads(q):
    return jnp.concatenate([q[:, HEAD_DIM * h:HEAD_DIM * (h + 1)] for h in range(HEADS_PER_GROUP)], axis=0)


def _head_cols(x, h):
    return x[:, Q_TILE * h:Q_TILE * (h + 1)]


def _masked_softmax_t(st, valid):
    es, sums = [], []
    for h in range(HEADS_PER_GROUP):
        s = jnp.where(valid, _head_cols(st, h), NEG)
        e = jnp.exp2(s - jnp.max(s, axis=0, keepdims=True))
        es.append(e)
        sums.append(jnp.sum(e, axis=0, keepdims=True))
    return es, sums


def _nsa_kernel(tk, q_ref, qr_ref, kc_ref, vct_ref, ovt_ref, ks_ref, vst_ref, kw_ref, vwt_ref, gate_ref,
                o_ref, ocmp_ref, owin_ref, bias_ref, s0_ref, s1_ref, mx0_ref, mx1_ref, m_ref, acc_ref):
    i = pl.program_id(1)
    tq = q_ref.shape[0]
    nc = kc_ref.shape[0]
    n_sel = ovt_ref.shape[0]
    hp = HEADS_PER_GROUP
    q4 = _stack_heads(q_ref[...])
    qr4 = _stack_heads(qr_ref[...])
    tok = i * tq + lax.broadcasted_iota(jnp.int32, (1, tq), 1)
    gates = gate_ref[...]

    q0 = pl.multiple_of(i * tq, tq)
    span = WINDOW + tq
    wstart = pl.multiple_of(jnp.maximum(i * tq - WINDOW, 0), tq)

    def cmp_and_select(ncv, rows):
        sc = _dot_nt(kc_ref[:ncv, :], q4)
        sw = _dot_nt(kw_ref[pl.ds(wstart, span), :], qr4)
        sd = _dot_nt(ks_ref[pl.ds(q0, tq), :], qr4)

        n_idx = lax.broadcasted_iota(jnp.int32, (ncv, tq), 0)
        valid_c = n_idx * CMP_STRIDE + (CMP_LEN - 1) <= tok
        es, sums = _masked_softmax_t(sc, valid_c)
        diff = tok - (wstart + lax.broadcasted_iota(jnp.int32, (span, tq), 0))
        ew, wsums = _masked_softmax_t(sw, jnp.logical_and(diff >= 0, diff < WINDOW))
        not_future = lax.broadcasted_iota(jnp.int32, (tq, tq), 0) <= lax.broadcasted_iota(jnp.int32, (tq, tq), 1)
        sd = jnp.concatenate([jnp.where(not_future, _head_cols(sd, h), NEG) for h in range(hp)], axis=1)
        m0 = jnp.max(sd, axis=0, keepdims=True)
        pd = jnp.exp2(sd - m0)
        any_c = tok >= CMP_LEN - 1
        ps = [es[h] * jnp.where(any_c, 1.0 / sums[h], 0.0) for h in range(hp)]

        ocmp_ref[...] = _dot(vct_ref[:, :ncv], jnp.concatenate(ps, axis=1).astype(BF16))
        o_win = _dot(vwt_ref[:, pl.ds(wstart, span)], jnp.concatenate(ew, axis=1).astype(BF16))
        owin_ref[...] = o_win * jnp.concatenate([1.0 / w for w in wsums], axis=1)
        m_ref[...] = m0
        acc_ref[...] = _dot(vst_ref[:, pl.ds(q0, tq)], pd.astype(BF16))

        psum = ps[0] + ps[1] + ps[2] + ps[3]
        ovt = ovt_ref[:rows, :ncv]
        imp = sum(_dot(ovt, term) for term in _split3(psum))

        m_idx = lax.broadcasted_iota(jnp.int32, (rows, tq), 0)
        blk_t = tok // SEL_LEN
        forced = (m_idx == 0) | (m_idx == blk_t) | (m_idx == blk_t - 1)
        causal = m_idx <= blk_t
        imp = jnp.where(jnp.logical_and(causal, jnp.logical_not(forced)), imp, -jnp.inf)

        def pick(_, st):
            v, sel = st
            mx = jnp.max(v, axis=0, keepdims=True)
            idx = jnp.min(jnp.where(v == mx, m_idx, rows), axis=0, keepdims=True)
            hit = m_idx == idx
            return jnp.where(hit, -jnp.inf, v), jnp.where(hit, 1.0, sel)

        _, sel = lax.fori_loop(0, SEL_TOPK - 3, pick, (imp, jnp.where(forced, 1.0, 0.0)))
        before = m_idx < (i * tq) // SEL_LEN
        bias_ref[:rows, :] = jnp.where(jnp.logical_and(sel > 0.5, before), 0.0, NEG)
        if rows < n_sel:
            bias_ref[rows:, :] = jnp.full((n_sel - rows, tq), NEG, F32)

    rows_per_chunk = CMP_CHUNK * CMP_STRIDE // SEL_LEN
    n_prefix = -(-nc // CMP_CHUNK)
    need = ((i + 1) * tq - 1) // (CMP_STRIDE * CMP_CHUNK) + 1
    for c in range(1, n_prefix + 1):
        pl.when(need == c)(functools.partial(cmp_and_select, min(c * CMP_CHUNK, nc), min(c * rows_per_chunk, n_sel)))
    bpt = tk // SEL_LEN

    def sel_scores(kt, s_ref, mx_ref):
        start = pl.multiple_of(kt * tk, tk)
        st = _dot_nt(ks_ref[pl.ds(start, tk), :], qr4)
        rows = bias_ref[pl.ds(pl.multiple_of(kt * bpt, bpt), bpt), :]
        bias = jnp.concatenate([jnp.broadcast_to(rows[j:j + 1], (SEL_LEN, tq)) for j in range(bpt)], axis=0)
        s = jnp.concatenate([_head_cols(st, h) + bias for h in range(hp)], axis=1)
        s_ref[...] = s
        mx_ref[...] = jnp.max(s, axis=0, keepdims=True)

    def sel_accumulate(kt, s_ref, mx_ref):
        start = pl.multiple_of(kt * tk, tk)
        m = m_ref[...]
        m_new = jnp.maximum(m, mx_ref[...])
        p = jnp.exp2(s_ref[...] - m_new)
        acc_ref[...] = jnp.exp2(m - m_new) * acc_ref[...] + _dot(vst_ref[:, pl.ds(start, tk)], p.astype(BF16))
        m_ref[...] = m_new

    last = jnp.maximum(i * tq - 1, 0) // tk
    final_tile = ks_ref.shape[0] // tk - 1

    def sel_pair(j):
        sel_scores(2 * j + 1, s1_ref, mx1_ref)
        sel_accumulate(2 * j, s0_ref, mx0_ref)
        sel_scores(jnp.minimum(2 * j + 2, final_tile), s0_ref, mx0_ref)
        sel_accumulate(2 * j + 1, s1_ref, mx1_ref)

    def sel_two_pairs(jj, carry):
        sel_pair(2 * jj)
        sel_pair(2 * jj + 1)
        return carry

    sel_scores(0, s0_ref, mx0_ref)
    pairs = (last + 1) // 2
    lax.fori_loop(0, pairs // 2, sel_two_pairs, 0)

    @pl.when(pairs % 2 == 1)
    def _():
        sel_pair(pairs - 1)

    @pl.when(last % 2 == 0)
    def _():
        sel_accumulate(last, s0_ref, mx0_ref)

    acc = acc_ref[...]
    o_sel = acc[:HEAD_DIM] * (1.0 / acc[HEAD_DIM:HEAD_DIM + 1])

    o_cmp = ocmp_ref[...]
    o_win = owin_ref[...]
    for h in range(hp):
        mix = (gates[h:h + 1] * _head_cols(o_cmp, h)
               + gates[hp + h:hp + h + 1] * _head_cols(o_sel, h)
               + gates[2 * hp + h:2 * hp + h + 1] * _head_cols(o_win, h))
        o_ref[:, HEAD_DIM * h:HEAD_DIM * (h + 1)] = mix.T.astype(BF16)


def _nsa_attention(q, q_rot, k_cmp, v_cmp_t, overlap_t, k_slc, v_slc_t, k_win, v_win_t, gates):
    s, d = q.shape
    nc = k_cmp.shape[1]
    n_sel = overlap_t.shape[0]
    gw = HEADS_PER_GROUP * HEAD_DIM
    cols = HEADS_PER_GROUP * Q_TILE
    tk = 512
    qspec = pl.BlockSpec((Q_TILE, gw), lambda g, i: (i, g))
    per_group = lambda shape: pl.BlockSpec((None,) + shape, lambda g, i: (g, 0, 0))
    return pl.pallas_call(
        functools.partial(_nsa_kernel, tk),
        grid=(KV_GROUPS, s // Q_TILE),
        in_specs=[qspec, qspec,
                  per_group((nc, HEAD_DIM)), per_group((HEAD_DIM, nc)),
                  pl.BlockSpec((n_sel, nc), lambda g, i: (0, 0)),
                  per_group((s, HEAD_DIM)), per_group((V_ROWS, s)),
                  per_group((s, HEAD_DIM)), per_group((HEAD_DIM, s)),
                  pl.BlockSpec((None, GATE_ROWS, Q_TILE), lambda g, i: (g, 0, i))],
        out_specs=qspec,
        out_shape=jax.ShapeDtypeStruct((s, d), BF16),
        scratch_shapes=[pltpu.VMEM((HEAD_DIM, cols), F32), pltpu.VMEM((HEAD_DIM, cols), F32),
                        pltpu.VMEM((n_sel, Q_TILE), F32),
                        pltpu.VMEM((tk, cols), F32), pltpu.VMEM((tk, cols), F32),
                        pltpu.VMEM((1, cols), F32), pltpu.VMEM((1, cols), F32),
                        pltpu.VMEM((1, cols), F32), pltpu.VMEM((V_ROWS, cols), F32)],
        compiler_params=_params(("arbitrary", "arbitrary")),
        name="nsa_attention",
    )(q, q_rot, k_cmp, v_cmp_t, overlap_t, k_slc, v_slc_t, k_win, v_win_t, gates)


def _rope_tables(s):
    half = ROT_DIM // 2
    inv_freq = ROPE_THETA ** (-jnp.arange(half, dtype=F32) * 2.0 / ROT_DIM)
    ang = jnp.arange(s, dtype=F32)[:, None] * inv_freq[None, :]
    cos, sin = jnp.cos(ang), jnp.sin(ang)
    rest = HEAD_DIM - ROT_DIM
    c = jnp.concatenate([cos, cos, jnp.ones((s, rest), F32)], axis=1)
    s1 = jnp.concatenate([-sin, jnp.zeros((s, half + rest), F32)], axis=1)
    s2 = jnp.concatenate([jnp.zeros((s, half), F32), sin, jnp.zeros((s, rest), F32)], axis=1)
    return tuple(jnp.tile(a, (1, 2)) for a in (c, s1, s2))


def _overlap_matrix(nch, n_sel):
    cmp_start = jnp.arange(nch)[:, None] * CMP_STRIDE
    sel_start = jnp.arange(n_sel)[None, :] * SEL_LEN
    return ((cmp_start < sel_start + SEL_LEN) & (cmp_start + CMP_LEN > sel_start)).astype(BF16)


def kernel(x, norm_gain, sb_w_qkv, sb_w_o, kv_norm, nsa_w_kv, cmp_pos, cmp_w1, cmp_w2,
           nsa_w_q, nsa_gate_b, nsa_w_o, mlp_w1, mlp_w2, final_norm):
    b, s, d = x.shape
    assert b == 1 and d == N_HEADS * HEAD_DIM
    assert s % 512 == 0 and s >= WINDOW + Q_TILE and s // SEL_LEN >= SEL_TOPK
    n_cmp = (s - CMP_LEN) // CMP_STRIDE + 1
    nch = s // CMP_STRIDE
    n_sel = s // SEL_LEN
    h0 = x[0]
    rope_tabs = _rope_tables(s)

    qkv = _qkv_proj(h0, norm_gain[0, 0], sb_w_qkv[0].astype(BF16))
    o_sb = _sb_attention(qkv)
    h1 = _attn_out_mlp(o_sb, h0, sb_w_o[0].astype(BF16), norm_gain[0, 1],
                       mlp_w1[0].astype(BF16), mlp_w2[0].astype(BF16))

    cmp_kv, k_slc, v_slc_t, k_win, v_win_t = _kv_proj(h1, kv_norm, nsa_w_kv.astype(BF16), rope_tabs)
    pos_flat = jnp.broadcast_to(cmp_pos.reshape(2, 1, CMP_LEN * HEAD_DIM), (2, 8, CMP_LEN * HEAD_DIM))
    kv_cmp, kv_cmp_t = _cmp_mlp(cmp_kv, pos_flat, cmp_w1.astype(BF16), cmp_w2.astype(BF16), n_cmp)

    n_qk = N_HEADS * HEAD_DIM
    w_q = jnp.concatenate([nsa_w_q[0][:, :n_qk], _group_gate_columns(nsa_w_q[0][:, n_qk:])], axis=1).astype(BF16)
    gate_b = _group_gate_columns(nsa_gate_b[0]).reshape(1, 128)
    q, q_rot, gates = _q_proj(h1, norm_gain[1, 0], w_q, gate_b, rope_tabs)
    o_nsa = _nsa_attention(q, q_rot, kv_cmp[0], kv_cmp_t[1], _overlap_matrix(nch, n_sel).T,
                           k_slc, v_slc_t, k_win, v_win_t, gates)
    out = _attn_out_mlp(o_nsa, h1, nsa_w_o[0].astype(BF16), norm_gain[1, 1],
                        mlp_w1[1].astype(BF16), mlp_w2[1].astype(BF16), final_gain=final_norm)
    return out[None]
```

```python
import functools

import jax
import jax.numpy as jnp
from jax import lax
from jax.experimental import pallas as pl
from jax.experimental.pallas import tpu as pltpu

HEAD_DIM = 64
N_HEADS = 16
KV_GROUPS = 4
HEADS_PER_GROUP = N_HEADS // KV_GROUPS
CMP_LEN = 32
CMP_STRIDE = 16
CMP_HIDDEN = 256
SEL_LEN = 64
SEL_TOPK = 16
WINDOW = 512
ROPE_THETA = 500000.0
ROT_DIM = HEAD_DIM // 4
Q_TILE = 128
GATE_ROWS = 16
V_ROWS = HEAD_DIM + 16
LOG2E = 1.4426950408889634
CMP_CHUNK = 256
SEL_UNROLL = 4
NORM_EPS = 1e-5
NEG = -1e30
FORCED_SCORE = 1e6
SB_UNDERFLOW = -110.0
SB_NEAR = 256
VMEM_LIMIT = 56 * 1024 * 1024

BF16 = jnp.bfloat16
F32 = jnp.float32


def _params(semantics, vmem=VMEM_LIMIT):
    return pltpu.CompilerParams(dimension_semantics=semantics, vmem_limit_bytes=vmem)


def _rms(x, g):
    return x * lax.rsqrt(jnp.mean(x * x, axis=-1, keepdims=True) + NORM_EPS) * g


def _dot(a, b):
    return jnp.dot(a, b, preferred_element_type=F32)


def _dot_nt(a, b):
    return lax.dot_general(a, b, (((1,), (1,)), ((), ())), preferred_element_type=F32)


def _rope128(x, c, s1, s2):
    return x * c + pltpu.roll(x, 128 - ROT_DIM // 2, axis=1) * s1 + pltpu.roll(x, ROT_DIM // 2, axis=1) * s2


def _split3(x):
    hi = x.astype(BF16)
    r = x - hi.astype(F32)
    mid = r.astype(BF16)
    lo = (r - mid.astype(F32)).astype(BF16)
    return hi, mid, lo


def _qkv_kernel(x_ref, g_ref, w_ref, o_ref):
    xn = _rms(x_ref[...], g_ref[...]).astype(BF16)
    acc = _dot(xn, w_ref[...])
    d = x_ref.shape[1]
    o_ref[:, :d] = (acc[:, :d] * (HEAD_DIM ** -0.5)).astype(BF16)
    o_ref[:, d:] = acc[:, d:].astype(BF16)


def _qkv_proj(x, gain, w):
    s, d = x.shape
    n = w.shape[1]
    tm = 256
    return pl.pallas_call(
        _qkv_kernel,
        grid=(s // tm,),
        in_specs=[pl.BlockSpec((tm, d), lambda i: (i, 0)),
                  pl.BlockSpec((1, d), lambda i: (0, 0)),
                  pl.BlockSpec((d, n), lambda i: (0, 0))],
        out_specs=pl.BlockSpec((tm, n), lambda i: (i, 0)),
        out_shape=jax.ShapeDtypeStruct((s, n), BF16),
        compiler_params=_params(("arbitrary",)),
        name="qkv_proj",
    )(x, gain.reshape(1, d), w)


def _sb_blocks(qs, ks, vs, tri, carries, mask):
    n = len(qs)
    zs = [_dot_nt(qs[h], ks[h]) for h in range(n)]
    sps = [jnp.maximum(z, 0.0) + jnp.log(1.0 + jnp.exp(-jnp.abs(z))) for z in zs]
    logs = [-sp for sp in sps]
    if mask is not None:
        logs = [jnp.where(mask, x, 0.0) for x in logs]
    excls = [_dot(logs[h].astype(BF16), tri) for h in range(n)]
    ws = [jnp.exp((zs[h] - sps[h]) + (excls[h] + carries[h])) for h in range(n)]
    if mask is not None:
        ws = [jnp.where(mask, w, 0.0) for w in ws]
    outs = [_dot(ws[h].astype(BF16), vs[h]) for h in range(n)]
    new_carries = [carries[h] + jnp.sum(logs[h], axis=1, keepdims=True) for h in range(n)]
    return outs, new_carries


def _sb_kernel(q_ref, k_ref, v_ref, tri_ref, o_ref):
    i = pl.program_id(1)
    tq = q_ref.shape[0]
    nh = q_ref.shape[1] // HEAD_DIM
    near = tri_ref.shape[0]
    tri_near = tri_ref[...]
    tri = tri_near[:tq, :tq]
    q = q_ref[...]
    head = lambda x, h: x[:, HEAD_DIM * h:HEAD_DIM * (h + 1)]

    start = pl.multiple_of(jnp.maximum(i * tq - (near - tq), 0), tq)
    row = lax.broadcasted_iota(jnp.int32, (tq, near), 0)
    col = lax.broadcasted_iota(jnp.int32, (tq, near), 1)
    causal = col - row < i * tq - start
    k = k_ref[pl.ds(start, near), :]
    v = v_ref[pl.ds(start, near), :]
    qs = [head(q, h) for h in range(nh)]
    split = lambda x: [head(x, h) for h in range(nh)]
    outs, carries = _sb_blocks(qs, split(k), split(v), tri_near, [jnp.zeros((tq, 1), F32)] * nh, causal)

    def worst(cs):
        mx = jnp.max(cs[0])
        for c in cs[1:]:
            mx = jnp.maximum(mx, jnp.max(c))
        return mx

    def cond(st):
        return jnp.logical_and(st[0] >= 0, st[1] > SB_UNDERFLOW)

    def body(st):
        kb, _, outs, carries = st
        start = pl.multiple_of(kb * tq, tq)
        k = k_ref[pl.ds(start, tq), :]
        v = v_ref[pl.ds(start, tq), :]
        more, new_c = _sb_blocks(qs, split(k), split(v), tri, list(carries), None)
        return kb - 1, worst(new_c), tuple(outs[h] + more[h] for h in range(nh)), tuple(new_c)

    st = lax.while_loop(cond, body, (start // tq - 1, worst(carries), tuple(outs), tuple(carries)))
    for h in range(nh):
        o_ref[:, HEAD_DIM * h:HEAD_DIM * (h + 1)] = st[2][h].astype(BF16)


def _sb_attention(qkv):
    s = qkv.shape[0]
    d = N_HEADS * HEAD_DIM
    hps = 8
    ngrp = N_HEADS // hps
    w = hps * HEAD_DIM
    near = SB_NEAR + Q_TILE
    idx = jnp.arange(near)
    tri = (idx[:, None] > idx[None, :]).astype(BF16)
    return pl.pallas_call(
        _sb_kernel,
        grid=(ngrp, s // Q_TILE),
        in_specs=[pl.BlockSpec((Q_TILE, w), lambda h, i: (i, h)),
                  pl.BlockSpec((s, w), lambda h, i: (0, ngrp + h), pipeline_mode=pl.Buffered(1)),
                  pl.BlockSpec((s, w), lambda h, i: (0, 2 * ngrp + h), pipeline_mode=pl.Buffered(1)),
                  pl.BlockSpec((near, near), lambda h, i: (0, 0))],
        out_specs=pl.BlockSpec((Q_TILE, w), lambda h, i: (i, h)),
        out_shape=jax.ShapeDtypeStruct((s, d), BF16),
        compiler_params=_params(("arbitrary", "arbitrary")),
        name="sb_attention",
    )(qkv, qkv, qkv, tri)


def _mlp_kernel(final, tf, o_ref, res_ref, wo_ref, g_ref, w1_ref, w2_ref, *rest):
    out_ref = rest[-1]
    h = res_ref[...] + _dot(o_ref[...], wo_ref[...])
    xn = _rms(h, g_ref[...]).astype(BF16)
    out_ref[...] = h
    for j in range(w1_ref.shape[1] // tf):
        a = jnp.maximum(_dot(xn, w1_ref[:, tf * j:tf * (j + 1)]), 0.0)
        out_ref[...] += _dot((a * a).astype(BF16), w2_ref[tf * j:tf * (j + 1), :])
    if final:
        out_ref[...] = _rms(out_ref[...], rest[0][...])


def _attn_out_mlp(o, resid, wo, gain, w1, w2, final_gain=None):
    s, d = resid.shape
    f = w1.shape[1]
    tm, tf = 512, 512
    final = final_gain is not None
    row = lambda i: (i, 0)
    resident = lambda shape: pl.BlockSpec(shape, lambda i: (0, 0), pipeline_mode=pl.Buffered(1))
    in_specs = [pl.BlockSpec((tm, d), row), pl.BlockSpec((tm, d), row), resident((d, d)), resident((1, d)),
                resident((d, f)), resident((f, d))]
    args = [o, resid, wo, gain.reshape(1, d), w1, w2]
    if final:
        in_specs.append(resident((1, d)))
        args.append(final_gain.reshape(1, d))
    return pl.pallas_call(
        functools.partial(_mlp_kernel, final, tf),
        grid=(s // tm,),
        in_specs=in_specs,
        out_specs=pl.BlockSpec((tm, d), row),
        out_shape=jax.ShapeDtypeStruct((s, d), F32),
        compiler_params=_params(("arbitrary",)),
        name="attn_out_mlp",
    )(*args)


def _kv_kernel(x_ref, g_ref, w_ref, c_ref, s1_ref, s2_ref, cmp_ref, ks_ref, vs_ref, kw_ref, vw_ref):
    xn = _rms(x_ref[...], g_ref[...]).astype(BF16)
    acc = _dot(xn, w_ref[...])
    gw = KV_GROUPS * HEAD_DIM
    cmp_ref[...] = acc[:, :2 * gw]
    c, s1, s2 = c_ref[...], s1_ref[...], s2_ref[...]

    def put_k(ref, x):
        for p in range(gw // 128):
            xp = _rope128(x[:, 128 * p:128 * (p + 1)], c, s1, s2)
            ref[2 * p] = xp[:, :HEAD_DIM].astype(BF16)
            ref[2 * p + 1] = xp[:, HEAD_DIM:].astype(BF16)

    def put_vt(ref, x):
        xt = x.T
        ones_row = (lax.broadcasted_iota(jnp.int32, (V_ROWS - HEAD_DIM, xt.shape[1]), 0) == 0).astype(BF16)
        for g in range(KV_GROUPS):
            ref[g, :HEAD_DIM, :] = xt[HEAD_DIM * g:HEAD_DIM * (g + 1)].astype(BF16)
            ref[g, HEAD_DIM:, :] = ones_row

    put_k(ks_ref, acc[:, 2 * gw:3 * gw])
    put_vt(vs_ref, acc[:, 3 * gw:4 * gw])
    put_k(kw_ref, acc[:, 4 * gw:5 * gw])
    put_vt(vw_ref, acc[:, 5 * gw:6 * gw])


def _kv_proj(h, gain, w, rope_tabs):
    s, d = h.shape
    n = w.shape[1]
    gw = KV_GROUPS * HEAD_DIM
    tm = 256
    row = lambda i: (i, 0)
    fixed = lambda i: (0, 0)
    k_spec = pl.BlockSpec((KV_GROUPS, tm, HEAD_DIM), lambda i: (0, i, 0))
    k_shape = jax.ShapeDtypeStruct((KV_GROUPS, s, HEAD_DIM), BF16)
    vt_spec = pl.BlockSpec((KV_GROUPS, V_ROWS, tm), lambda i: (0, 0, i))
    vt_shape = jax.ShapeDtypeStruct((KV_GROUPS, V_ROWS, s), BF16)
    return pl.pallas_call(
        _kv_kernel,
        grid=(s // tm,),
        in_specs=[pl.BlockSpec((tm, d), row), pl.BlockSpec((1, d), fixed), pl.BlockSpec((d, n), fixed),
                  pl.BlockSpec((tm, 128), row), pl.BlockSpec((tm, 128), row), pl.BlockSpec((tm, 128), row)],
        out_specs=[pl.BlockSpec((tm, 2 * gw), row), k_spec, vt_spec, k_spec, vt_spec],
        out_shape=[jax.ShapeDtypeStruct((s, 2 * gw), F32), k_shape, vt_shape, k_shape, vt_shape],
        compiler_params=_params(("arbitrary",)),
        name="kv_proj",
    )(h, gain.reshape(1, d), w, *rope_tabs)


def _cmp_kernel(n_cmp, x_ref, pos_ref, w1_ref, w2_ref, w2t_ref, o_ref, ot_ref):
    nch = x_ref.shape[0] // CMP_STRIDE
    half = CMP_STRIDE * HEAD_DIM
    w1 = w1_ref[...]
    bias = _dot(pos_ref[...].astype(BF16), w1)[0:1]
    acc = [[jnp.zeros((nch, CMP_HIDDEN), F32) for _ in range(2)] for _ in range(2)]
    for l in range(CMP_STRIDE):
        y = x_ref[pl.ds(l, nch, stride=CMP_STRIDE), :].astype(BF16)
        wa = w1[HEAD_DIM * l:HEAD_DIM * (l + 1)]
        wb = w1[half + HEAD_DIM * l:half + HEAD_DIM * (l + 1)]
        for gg in range(2):
            yg = y[:, HEAD_DIM * gg:HEAD_DIM * (gg + 1)]
            acc[gg][0] = acc[gg][0] + _dot(yg, wa)
            acc[gg][1] = acc[gg][1] + _dot(yg, wb)
    live = lax.broadcasted_iota(jnp.int32, (nch, HEAD_DIM), 0) < n_cmp
    live_t = lax.broadcasted_iota(jnp.int32, (HEAD_DIM, nch), 1) < n_cmp
    ones_row = (lax.broadcasted_iota(jnp.int32, (V_ROWS - HEAD_DIM, nch), 0) == 0).astype(BF16)
    for gg in range(2):
        hid = acc[gg][0] + pltpu.roll(acc[gg][1], nch - 1, axis=0) + bias
        hid = jax.nn.gelu(hid, approximate=True).astype(BF16)
        o_ref[gg] = jnp.where(live, _dot(hid, w2_ref[...]), 0.0).astype(BF16)
        ot_ref[gg, :HEAD_DIM, :] = jnp.where(live_t, _dot_nt(w2t_ref[...], hid), 0.0).astype(BF16)
        ot_ref[gg, HEAD_DIM:, :] = ones_row


def _cmp_mlp(cmp_kv, pos_flat, w1, w2, n_cmp):
    s = cmp_kv.shape[0]
    nch = s // CMP_STRIDE
    feat = CMP_LEN * HEAD_DIM
    return pl.pallas_call(
        functools.partial(_cmp_kernel, n_cmp),
        grid=(2, KV_GROUPS // 2),
        in_specs=[pl.BlockSpec((s, 2 * HEAD_DIM), lambda c, p: (0, c * (KV_GROUPS // 2) + p)),
                  pl.BlockSpec((None, 8, feat), lambda c, p: (c, 0, 0)),
                  pl.BlockSpec((None, feat, CMP_HIDDEN), lambda c, p: (c, 0, 0)),
                  pl.BlockSpec((None, CMP_HIDDEN, HEAD_DIM), lambda c, p: (c, 0, 0)),
                  pl.BlockSpec((None, HEAD_DIM, CMP_HIDDEN), lambda c, p: (c, 0, 0))],
        out_specs=[pl.BlockSpec((None, 2, nch, HEAD_DIM), lambda c, p: (c, p, 0, 0)),
                   pl.BlockSpec((None, 2, V_ROWS, nch), lambda c, p: (c, p, 0, 0))],
        out_shape=[jax.ShapeDtypeStruct((2, KV_GROUPS, nch, HEAD_DIM), BF16),
                   jax.ShapeDtypeStruct((2, KV_GROUPS, V_ROWS, nch), BF16)],
        compiler_params=_params(("arbitrary", "arbitrary")),
        name="cmp_mlp",
    )(cmp_kv, pos_flat, w1, w2, jnp.swapaxes(w2, 1, 2))


def _q_kernel(x_ref, g_ref, w_ref, b_ref, c_ref, s1_ref, s2_ref, q_ref, qr_ref, gate_ref):
    xn = _rms(x_ref[...], g_ref[...]).astype(BF16)
    acc = _dot(xn, w_ref[...])
    d = q_ref.shape[1]
    c, s1, s2 = c_ref[...], s1_ref[...], s2_ref[...]
    for p in range(d // 128):
        qp = acc[:, 128 * p:128 * (p + 1)] * (HEAD_DIM ** -0.5 * LOG2E)
        q_ref[:, 128 * p:128 * (p + 1)] = qp.astype(BF16)
        qr_ref[:, 128 * p:128 * (p + 1)] = _rope128(qp, c, s1, s2).astype(BF16)
    gates_t = (1.0 / (1.0 + jnp.exp(-(acc[:, d:] + b_ref[...])))).T
    for g in range(KV_GROUPS):
        gate_ref[g] = gates_t[GATE_ROWS * g:GATE_ROWS * (g + 1)]


def _q_proj(h, gain, w, gate_b, rope_tabs):
    s, d = h.shape
    n = w.shape[1]
    tm = 256
    row = lambda i: (i, 0)
    fixed = lambda i: (0, 0)
    return pl.pallas_call(
        _q_kernel,
        grid=(s // tm,),
        in_specs=[pl.BlockSpec((tm, d), row), pl.BlockSpec((1, d), fixed), pl.BlockSpec((d, n), fixed),
                  pl.BlockSpec((1, 128), fixed),
                  pl.BlockSpec((tm, 128), row), pl.BlockSpec((tm, 128), row), pl.BlockSpec((tm, 128), row)],
        out_specs=[pl.BlockSpec((tm, d), row), pl.BlockSpec((tm, d), row),
                   pl.BlockSpec((KV_GROUPS, GATE_ROWS, tm), lambda i: (0, 0, i))],
        out_shape=[jax.ShapeDtypeStruct((s, d), BF16), jax.ShapeDtypeStruct((s, d), BF16),
                   jax.ShapeDtypeStruct((KV_GROUPS, GATE_ROWS, s), F32)],
        compiler_params=_params(("arbitrary",)),
        name="nsa_q_proj",
    )(h, gain.reshape(1, d), w, gate_b, *rope_tabs)


def _group_gate_columns(a):
    lead = a.shape[:-1]
    a = a.reshape(lead + (3, KV_GROUPS, HEADS_PER_GROUP))
    a = jnp.moveaxis(a, -2, -3).reshape(lead + (KV_GROUPS, 3 * HEADS_PER_GROUP))
    a = jnp.pad(a, [(0, 0)] * len(lead) + [(0, 0), (0, GATE_ROWS - 3 * HEADS_PER_GROUP)])
    a = a.reshape(lead + (KV_GROUPS * GATE_ROWS,))
    return jnp.pad(a, [(0, 0)] * len(lead) + [(0, 128 - KV_GROUPS * GATE_ROWS)])


def _stack_heads(q):
    return jnp.concatenate([q[:, HEAD_DIM * h:HEAD_DIM * (h + 1)] for h in range(HEADS_PER_GROUP)], axis=0)


def _head_cols(x, h):
    return x[:, Q_TILE * h:Q_TILE * (h + 1)]


def _softmax_numerators_t(st, mask):
    es = []
    for h in range(HEADS_PER_GROUP):
        s = mask(_head_cols(st, h))
        es.append(jnp.exp2(s - jnp.max(s, axis=0, keepdims=True)))
    return es


def _nsa_kernel(tk, q_ref, qr_ref, kc_ref, vct_ref, ovt_ref, ks_ref, vst_ref, kw_ref, vwt_ref, wbias_ref, gate_ref,
                o_ref, ocmp_ref, owin_ref, bias_ref, s0_ref, s1_ref, mx0_ref, mx1_ref, m_ref, acc_ref):
    i = pl.program_id(1)
    tq = q_ref.shape[0]
    nc = kc_ref.shape[0]
    n_sel = ovt_ref.shape[0]
    hp = HEADS_PER_GROUP
    q4 = _stack_heads(q_ref[...])
    qr4 = _stack_heads(qr_ref[...])
    tok = i * tq + lax.broadcasted_iota(jnp.int32, (1, tq), 1)
    gates = gate_ref[...]

    q0 = pl.multiple_of(i * tq, tq)
    span = WINDOW + tq
    wstart = pl.multiple_of(jnp.maximum(i * tq - WINDOW, 0), tq)

    def cmp_and_select(ncv, rows):
        sc = _dot_nt(kc_ref[:ncv, :], q4)
        sw = _dot_nt(kw_ref[pl.ds(wstart, span), :], qr4)
        sd = _dot_nt(ks_ref[pl.ds(q0, tq), :], qr4)

        n_idx = lax.broadcasted_iota(jnp.int32, (ncv, tq), 0)
        last_complete = lax.shift_right_arithmetic(tok - (CMP_LEN - 1), CMP_STRIDE.bit_length() - 1)
        es = _softmax_numerators_t(sc, lambda s: jnp.where(n_idx <= last_complete, s, NEG))
        wbias = wbias_ref[...]
        ew = _softmax_numerators_t(sw, lambda s: s + wbias)
        not_future = lax.broadcasted_iota(jnp.int32, (tq, tq), 0) <= lax.broadcasted_iota(jnp.int32, (tq, tq), 1)
        sd = jnp.concatenate([jnp.where(not_future, _head_cols(sd, h), NEG) for h in range(hp)], axis=1)
        m0 = jnp.max(sd, axis=0, keepdims=True)
        pd = jnp.exp2(sd - m0)

        o_cmp = _dot(vct_ref[:, :ncv], jnp.concatenate(es, axis=1).astype(BF16))
        o_win = _dot(vwt_ref[:, pl.ds(wstart, span)], jnp.concatenate(ew, axis=1).astype(BF16))
        m_ref[...] = m0
        acc_ref[...] = _dot(vst_ref[:, pl.ds(q0, tq)], pd.astype(BF16))
        inv_c = jnp.where(jnp.concatenate([tok >= CMP_LEN - 1] * hp, axis=1), 1.0 / o_cmp[HEAD_DIM:HEAD_DIM + 1], 0.0)
        ocmp_ref[...] = o_cmp[:HEAD_DIM] * inv_c
        owin_ref[...] = o_win[:HEAD_DIM] * (1.0 / o_win[HEAD_DIM:HEAD_DIM + 1])

        psum = sum(es[h] * _head_cols(inv_c, h) for h in range(hp))
        ovt = ovt_ref[:rows, :ncv]
        imp = sum(_dot(ovt, term) for term in _split3(psum))

        m_idx = lax.broadcasted_iota(jnp.int32, (rows, tq), 0)
        blk_t = tok // SEL_LEN
        forced = (m_idx == 0) | (m_idx == blk_t) | (m_idx == blk_t - 1)
        candidate = jnp.logical_and(m_idx <= blk_t, jnp.logical_not(forced))
        imp = jnp.where(candidate, imp, -jnp.inf)

        def pick(_, v):
            mx = jnp.max(v, axis=0, keepdims=True)
            idx = jnp.min(jnp.where(v == mx, m_idx, rows), axis=0, keepdims=True)
            return jnp.where(m_idx == idx, -jnp.inf, v)

        picked = lax.fori_loop(0, SEL_TOPK - 3, pick, imp) == -jnp.inf
        before = m_idx < (i * tq) // SEL_LEN
        bias_ref[:rows, :] = jnp.where(jnp.logical_and(picked, before), 0.0, NEG)
        if rows < n_sel:
            bias_ref[rows:, :] = jnp.full((n_sel - rows, tq), NEG, F32)

    rows_per_chunk = CMP_CHUNK * CMP_STRIDE // SEL_LEN
    n_prefix = -(-nc // CMP_CHUNK)
    need = ((i + 1) * tq - 1) // (CMP_STRIDE * CMP_CHUNK) + 1
    for c in range(1, n_prefix + 1):
        pl.when(need == c)(functools.partial(cmp_and_select, min(c * CMP_CHUNK, nc), min(c * rows_per_chunk, n_sel)))
    bpt = tk // SEL_LEN

    def sel_scores(kt, s_ref, mx_ref):
        start = pl.multiple_of(kt * tk, tk)
        st = _dot_nt(ks_ref[pl.ds(start, tk), :], qr4)
        rows = bias_ref[pl.ds(pl.multiple_of(kt * bpt, bpt), bpt), :]
        bias = jnp.concatenate([jnp.broadcast_to(rows[j:j + 1], (SEL_LEN, tq)) for j in range(bpt)], axis=0)
        s = jnp.concatenate([_head_cols(st, h) + bias for h in range(hp)], axis=1)
        s_ref[...] = s
        mx_ref[...] = jnp.max(s, axis=0, keepdims=True)

    def sel_accumulate(kt, s_ref, mx_ref):
        start = pl.multiple_of(kt * tk, tk)
        m = m_ref[...]
        m_new = jnp.maximum(m, mx_ref[...])
        p = jnp.exp2(s_ref[...] - m_new)
        acc_ref[...] = jnp.exp2(m - m_new) * acc_ref[...] + _dot(vst_ref[:, pl.ds(start, tk)], p.astype(BF16))
        m_ref[...] = m_new

    last = jnp.maximum(i * tq - 1, 0) // tk
    final_tile = ks_ref.shape[0] // tk - 1

    def sel_pair(j):
        sel_scores(2 * j + 1, s1_ref, mx1_ref)
        sel_accumulate(2 * j, s0_ref, mx0_ref)
        sel_scores(jnp.minimum(2 * j + 2, final_tile), s0_ref, mx0_ref)
        sel_accumulate(2 * j + 1, s1_ref, mx1_ref)

    def sel_pairs(jj, carry):
        for u in range(SEL_UNROLL):
            sel_pair(SEL_UNROLL * jj + u)
        return carry

    sel_scores(0, s0_ref, mx0_ref)
    pairs = (last + 1) // 2
    lax.fori_loop(0, pairs // SEL_UNROLL, sel_pairs, 0)
    for u in range(SEL_UNROLL - 1):
        pl.when(pairs % SEL_UNROLL > u)(functools.partial(sel_pair, pairs // SEL_UNROLL * SEL_UNROLL + u))

    @pl.when(last % 2 == 0)
    def _():
        sel_accumulate(last, s0_ref, mx0_ref)

    acc = acc_ref[...]
    o_sel = acc[:HEAD_DIM] * (1.0 / acc[HEAD_DIM:HEAD_DIM + 1])

    o_cmp = ocmp_ref[...]
    o_win = owin_ref[...]
    for h in range(hp):
        mix = (gates[h:h + 1] * _head_cols(o_cmp, h)
               + gates[hp + h:hp + h + 1] * _head_cols(o_sel, h)
               + gates[2 * hp + h:2 * hp + h + 1] * _head_cols(o_win, h))
        o_ref[:, HEAD_DIM * h:HEAD_DIM * (h + 1)] = mix.T.astype(BF16)


def _nsa_attention(q, q_rot, k_cmp, v_cmp_t, overlap_t, k_slc, v_slc_t, k_win, v_win_t, gates):
    s, d = q.shape
    nc = k_cmp.shape[1]
    n_sel = overlap_t.shape[0]
    gw = HEADS_PER_GROUP * HEAD_DIM
    cols = HEADS_PER_GROUP * Q_TILE
    tk = 512
    n_wcase = WINDOW // Q_TILE + 1
    lead = (jnp.arange(n_wcase) * Q_TILE)[:, None, None]
    dist = lead + jnp.arange(Q_TILE)[None, None, :] - jnp.arange(WINDOW + Q_TILE)[None, :, None]
    wbias = jnp.where((dist >= 0) & (dist < WINDOW), 0.0, NEG).astype(F32)
    qspec = pl.BlockSpec((Q_TILE, gw), lambda g, i: (i, g))
    per_group = lambda shape: pl.BlockSpec((None,) + shape, lambda g, i: (g, 0, 0))
    return pl.pallas_call(
        functools.partial(_nsa_kernel, tk),
        grid=(KV_GROUPS, s // Q_TILE),
        in_specs=[qspec, qspec,
                  per_group((nc, HEAD_DIM)), per_group((V_ROWS, nc)),
                  pl.BlockSpec((n_sel, nc), lambda g, i: (0, 0)),
                  per_group((s, HEAD_DIM)), per_group((V_ROWS, s)),
                  per_group((s, HEAD_DIM)), per_group((V_ROWS, s)),
                  pl.BlockSpec((None, WINDOW + Q_TILE, Q_TILE), lambda g, i: (jnp.minimum(i, n_wcase - 1), 0, 0)),
                  pl.BlockSpec((None, GATE_ROWS, Q_TILE), lambda g, i: (g, 0, i))],
        out_specs=qspec,
        out_shape=jax.ShapeDtypeStruct((s, d), BF16),
        scratch_shapes=[pltpu.VMEM((HEAD_DIM, cols), F32), pltpu.VMEM((HEAD_DIM, cols), F32),
                        pltpu.VMEM((n_sel, Q_TILE), F32),
                        pltpu.VMEM((tk, cols), F32), pltpu.VMEM((tk, cols), F32),
                        pltpu.VMEM((1, cols), F32), pltpu.VMEM((1, cols), F32),
                        pltpu.VMEM((1, cols), F32), pltpu.VMEM((V_ROWS, cols), F32)],
        compiler_params=_params(("arbitrary", "arbitrary")),
        name="nsa_attention",
    )(q, q_rot, k_cmp, v_cmp_t, overlap_t, k_slc, v_slc_t, k_win, v_win_t, wbias, gates)


def _rope_tables(s):
    half = ROT_DIM // 2
    inv_freq = ROPE_THETA ** (-jnp.arange(half, dtype=F32) * 2.0 / ROT_DIM)
    ang = jnp.arange(s, dtype=F32)[:, None] * inv_freq[None, :]
    cos, sin = jnp.cos(ang), jnp.sin(ang)
    rest = HEAD_DIM - ROT_DIM
    c = jnp.concatenate([cos, cos, jnp.ones((s, rest), F32)], axis=1)
    s1 = jnp.concatenate([-sin, jnp.zeros((s, half + rest), F32)], axis=1)
    s2 = jnp.concatenate([jnp.zeros((s, half), F32), sin, jnp.zeros((s, rest), F32)], axis=1)
    return tuple(jnp.tile(a, (1, 2)) for a in (c, s1, s2))


def _overlap_matrix(nch, n_sel):
    cmp_start = jnp.arange(nch)[:, None] * CMP_STRIDE
    sel_start = jnp.arange(n_sel)[None, :] * SEL_LEN
    return ((cmp_start < sel_start + SEL_LEN) & (cmp_start + CMP_LEN > sel_start)).astype(BF16)


def kernel(x, norm_gain, sb_w_qkv, sb_w_o, kv_norm, nsa_w_kv, cmp_pos, cmp_w1, cmp_w2,
           nsa_w_q, nsa_gate_b, nsa_w_o, mlp_w1, mlp_w2, final_norm):
    b, s, d = x.shape
    assert b == 1 and d == N_HEADS * HEAD_DIM
    assert s % 512 == 0 and s >= WINDOW + Q_TILE and s // SEL_LEN >= SEL_TOPK
    n_cmp = (s - CMP_LEN) // CMP_STRIDE + 1
    nch = s // CMP_STRIDE
    n_sel = s // SEL_LEN
    h0 = x[0]
    rope_tabs = _rope_tables(s)

    qkv = _qkv_proj(h0, norm_gain[0, 0], sb_w_qkv[0].astype(BF16))
    o_sb = _sb_attention(qkv)
    h1 = _attn_out_mlp(o_sb, h0, sb_w_o[0].astype(BF16), norm_gain[0, 1],
                       mlp_w1[0].astype(BF16), mlp_w2[0].astype(BF16))

    cmp_kv, k_slc, v_slc_t, k_win, v_win_t = _kv_proj(h1, kv_norm, nsa_w_kv.astype(BF16), rope_tabs)
    pos_flat = jnp.broadcast_to(cmp_pos.reshape(2, 1, CMP_LEN * HEAD_DIM), (2, 8, CMP_LEN * HEAD_DIM))
    kv_cmp, kv_cmp_t = _cmp_mlp(cmp_kv, pos_flat, cmp_w1.astype(BF16), cmp_w2.astype(BF16), n_cmp)

    n_qk = N_HEADS * HEAD_DIM
    w_q = jnp.concatenate([nsa_w_q[0][:, :n_qk], _group_gate_columns(nsa_w_q[0][:, n_qk:])], axis=1).astype(BF16)
    gate_b = _group_gate_columns(nsa_gate_b[0]).reshape(1, 128)
    q, q_rot, gates = _q_proj(h1, norm_gain[1, 0], w_q, gate_b, rope_tabs)
    o_nsa = _nsa_attention(q, q_rot, kv_cmp[0], kv_cmp_t[1], _overlap_matrix(nch, n_sel).T,
                           k_slc, v_slc_t, k_win, v_win_t, gates)
    out = _attn_out_mlp(o_nsa, h1, nsa_w_o[0].astype(BF16), norm_gain[1, 1],
                        mlp_w1[1].astype(BF16), mlp_w2[1].astype(BF16), final_gain=final_norm)
    return out[None]
```

```python
import functools

import jax
import jax.numpy as jnp
from jax import lax
from jax.experimental import pallas as pl
from jax.experimental.pallas import tpu as pltpu

HEAD_DIM = 64
N_HEADS = 16
KV_GROUPS = 4
HEADS_PER_GROUP = N_HEADS // KV_GROUPS
CMP_LEN = 32
CMP_STRIDE = 16
CMP_HIDDEN = 256
SEL_LEN = 64
SEL_TOPK = 16
WINDOW = 512
ROPE_THETA = 500000.0
ROT_DIM = HEAD_DIM // 4
Q_TILE = 128
GATE_ROWS = 16
V_ROWS = HEAD_DIM + 16
LOG2E = 1.4426950408889634
CMP_CHUNK = 256
SEL_UNROLL = 4
PROJ_SUB = 256
NORM_EPS = 1e-5
NEG = -1e30
FORCED_SCORE = 1e6
SB_UNDERFLOW = -110.0
SB_NEAR = 256
VMEM_LIMIT = 56 * 1024 * 1024

BF16 = jnp.bfloat16
F32 = jnp.float32


def _params(semantics, vmem=VMEM_LIMIT):
    return pltpu.CompilerParams(dimension_semantics=semantics, vmem_limit_bytes=vmem)


def _rms(x, g):
    return x * lax.rsqrt(jnp.mean(x * x, axis=-1, keepdims=True) + NORM_EPS) * g


def _dot(a, b):
    return jnp.dot(a, b, preferred_element_type=F32)


def _dot_nt(a, b):
    return lax.dot_general(a, b, (((1,), (1,)), ((), ())), preferred_element_type=F32)


def _rope128(x, c, s1, s2):
    return x * c + pltpu.roll(x, 128 - ROT_DIM // 2, axis=1) * s1 + pltpu.roll(x, ROT_DIM // 2, axis=1) * s2


def _split3(x):
    hi = x.astype(BF16)
    r = x - hi.astype(F32)
    mid = r.astype(BF16)
    lo = (r - mid.astype(F32)).astype(BF16)
    return hi, mid, lo


def _qkv_kernel(x_ref, g_ref, w_ref, o_ref):
    xn = _rms(x_ref[...], g_ref[...]).astype(BF16)
    acc = _dot(xn, w_ref[...])
    d = x_ref.shape[1]
    o_ref[:, :d] = (acc[:, :d] * (HEAD_DIM ** -0.5)).astype(BF16)
    o_ref[:, d:] = acc[:, d:].astype(BF16)


def _qkv_proj(x, gain, w):
    s, d = x.shape
    n = w.shape[1]
    tm = 256
    return pl.pallas_call(
        _qkv_kernel,
        grid=(s // tm,),
        in_specs=[pl.BlockSpec((tm, d), lambda i: (i, 0)),
                  pl.BlockSpec((1, d), lambda i: (0, 0)),
                  pl.BlockSpec((d, n), lambda i: (0, 0))],
        out_specs=pl.BlockSpec((tm, n), lambda i: (i, 0)),
        out_shape=jax.ShapeDtypeStruct((s, n), BF16),
        compiler_params=_params(("arbitrary",)),
        name="qkv_proj",
    )(x, gain.reshape(1, d), w)


def _sb_blocks(qs, ks, vs, tri, carries, mask):
    n = len(qs)
    zs = [_dot_nt(qs[h], ks[h]) for h in range(n)]
    sps = [jnp.maximum(z, 0.0) + jnp.log(1.0 + jnp.exp(-jnp.abs(z))) for z in zs]
    logs = [-sp for sp in sps]
    if mask is not None:
        logs = [jnp.where(mask, x, 0.0) for x in logs]
    excls = [_dot(logs[h].astype(BF16), tri) for h in range(n)]
    ws = [jnp.exp((zs[h] - sps[h]) + (excls[h] + carries[h])) for h in range(n)]
    if mask is not None:
        ws = [jnp.where(mask, w, 0.0) for w in ws]
    outs = [_dot(ws[h].astype(BF16), vs[h]) for h in range(n)]
    new_carries = [carries[h] + jnp.sum(logs[h], axis=1, keepdims=True) for h in range(n)]
    return outs, new_carries


def _sb_kernel(q_ref, k_ref, v_ref, tri_ref, o_ref):
    i = pl.program_id(1)
    tq = q_ref.shape[0]
    nh = q_ref.shape[1] // HEAD_DIM
    near = tri_ref.shape[0]
    tri_near = tri_ref[...]
    tri = tri_near[:tq, :tq]
    q = q_ref[...]
    head = lambda x, h: x[:, HEAD_DIM * h:HEAD_DIM * (h + 1)]

    start = pl.multiple_of(jnp.maximum(i * tq - (near - tq), 0), tq)
    row = lax.broadcasted_iota(jnp.int32, (tq, near), 0)
    col = lax.broadcasted_iota(jnp.int32, (tq, near), 1)
    causal = col - row < i * tq - start
    k = k_ref[pl.ds(start, near), :]
    v = v_ref[pl.ds(start, near), :]
    qs = [head(q, h) for h in range(nh)]
    split = lambda x: [head(x, h) for h in range(nh)]
    outs, carries = _sb_blocks(qs, split(k), split(v), tri_near, [jnp.zeros((tq, 1), F32)] * nh, causal)

    def worst(cs):
        mx = jnp.max(cs[0])
        for c in cs[1:]:
            mx = jnp.maximum(mx, jnp.max(c))
        return mx

    def cond(st):
        return jnp.logical_and(st[0] >= 0, st[1] > SB_UNDERFLOW)

    def body(st):
        kb, _, outs, carries = st
        start = pl.multiple_of(kb * tq, tq)
        k = k_ref[pl.ds(start, tq), :]
        v = v_ref[pl.ds(start, tq), :]
        more, new_c = _sb_blocks(qs, split(k), split(v), tri, list(carries), None)
        return kb - 1, worst(new_c), tuple(outs[h] + more[h] for h in range(nh)), tuple(new_c)

    st = lax.while_loop(cond, body, (start // tq - 1, worst(carries), tuple(outs), tuple(carries)))
    for h in range(nh):
        o_ref[:, HEAD_DIM * h:HEAD_DIM * (h + 1)] = st[2][h].astype(BF16)


def _sb_attention(qkv):
    s = qkv.shape[0]
    d = N_HEADS * HEAD_DIM
    hps = 8
    ngrp = N_HEADS // hps
    w = hps * HEAD_DIM
    near = SB_NEAR + Q_TILE
    idx = jnp.arange(near)
    tri = (idx[:, None] > idx[None, :]).astype(BF16)
    return pl.pallas_call(
        _sb_kernel,
        grid=(ngrp, s // Q_TILE),
        in_specs=[pl.BlockSpec((Q_TILE, w), lambda h, i: (i, h)),
                  pl.BlockSpec((s, w), lambda h, i: (0, ngrp + h), pipeline_mode=pl.Buffered(1)),
                  pl.BlockSpec((s, w), lambda h, i: (0, 2 * ngrp + h), pipeline_mode=pl.Buffered(1)),
                  pl.BlockSpec((near, near), lambda h, i: (0, 0))],
        out_specs=pl.BlockSpec((Q_TILE, w), lambda h, i: (i, h)),
        out_shape=jax.ShapeDtypeStruct((s, d), BF16),
        compiler_params=_params(("arbitrary", "arbitrary")),
        name="sb_attention",
    )(qkv, qkv, qkv, tri)


def _mlp_kernel(final, tf, o_ref, res_ref, wo_ref, g_ref, w1_ref, w2_ref, *rest):
    out_ref = rest[-1]
    h = res_ref[...] + _dot(o_ref[...], wo_ref[...])
    xn = _rms(h, g_ref[...]).astype(BF16)
    out_ref[...] = h
    for j in range(w1_ref.shape[1] // tf):
        a = jnp.maximum(_dot(xn, w1_ref[:, tf * j:tf * (j + 1)]), 0.0)
        out_ref[...] += _dot((a * a).astype(BF16), w2_ref[tf * j:tf * (j + 1), :])
    if final:
        out_ref[...] = _rms(out_ref[...], rest[0][...])


def _attn_out_mlp(o, resid, wo, gain, w1, w2, final_gain=None):
    s, d = resid.shape
    f = w1.shape[1]
    tm, tf = 512, 512
    final = final_gain is not None
    row = lambda i: (i, 0)
    resident = lambda shape: pl.BlockSpec(shape, lambda i: (0, 0), pipeline_mode=pl.Buffered(1))
    in_specs = [pl.BlockSpec((tm, d), row), pl.BlockSpec((tm, d), row), resident((d, d)), resident((1, d)),
                resident((d, f)), resident((f, d))]
    args = [o, resid, wo, gain.reshape(1, d), w1, w2]
    if final:
        in_specs.append(resident((1, d)))
        args.append(final_gain.reshape(1, d))
    return pl.pallas_call(
        functools.partial(_mlp_kernel, final, tf),
        grid=(s // tm,),
        in_specs=in_specs,
        out_specs=pl.BlockSpec((tm, d), row),
        out_shape=jax.ShapeDtypeStruct((s, d), F32),
        compiler_params=_params(("arbitrary",)),
        name="attn_out_mlp",
    )(*args)


def _row_chunks(tm):
    return [slice(r, r + PROJ_SUB) for r in range(0, tm, PROJ_SUB)]


def _kv_kernel(x_ref, g_ref, w_ref, c_ref, s1_ref, s2_ref, cmp_ref, ks_ref, vs_ref, kw_ref, vw_ref):
    gw = KV_GROUPS * HEAD_DIM
    chunks = _row_chunks(x_ref.shape[0])
    accs = [_dot(_rms(x_ref[rows, :], g_ref[...]).astype(BF16), w_ref[...]) for rows in chunks]
    ones_row = (lax.broadcasted_iota(jnp.int32, (V_ROWS - HEAD_DIM, PROJ_SUB), 0) == 0).astype(BF16)

    def put_k(ref, x, rows):
        c, s1, s2 = c_ref[rows, :], s1_ref[rows, :], s2_ref[rows, :]
        for p in range(gw // 128):
            xp = _rope128(x[:, 128 * p:128 * (p + 1)], c, s1, s2)
            ref[2 * p, rows, :] = xp[:, :HEAD_DIM].astype(BF16)
            ref[2 * p + 1, rows, :] = xp[:, HEAD_DIM:].astype(BF16)

    def put_vt(ref, x, rows):
        xt = x.T
        for g in range(KV_GROUPS):
            ref[g, :HEAD_DIM, rows] = xt[HEAD_DIM * g:HEAD_DIM * (g + 1)].astype(BF16)
            ref[g, HEAD_DIM:, rows] = ones_row

    for rows, acc in zip(chunks, accs):
        cmp_ref[rows, :] = acc[:, :2 * gw]
        put_k(ks_ref, acc[:, 2 * gw:3 * gw], rows)
        put_vt(vs_ref, acc[:, 3 * gw:4 * gw], rows)
        put_k(kw_ref, acc[:, 4 * gw:5 * gw], rows)
        put_vt(vw_ref, acc[:, 5 * gw:6 * gw], rows)


def _kv_proj(h, gain, w, rope_tabs):
    s, d = h.shape
    n = w.shape[1]
    gw = KV_GROUPS * HEAD_DIM
    tm = 2 * PROJ_SUB
    row = lambda i: (i, 0)
    fixed = lambda i: (0, 0)
    k_spec = pl.BlockSpec((KV_GROUPS, tm, HEAD_DIM), lambda i: (0, i, 0))
    k_shape = jax.ShapeDtypeStruct((KV_GROUPS, s, HEAD_DIM), BF16)
    vt_spec = pl.BlockSpec((KV_GROUPS, V_ROWS, tm), lambda i: (0, 0, i))
    vt_shape = jax.ShapeDtypeStruct((KV_GROUPS, V_ROWS, s), BF16)
    return pl.pallas_call(
        _kv_kernel,
        grid=(s // tm,),
        in_specs=[pl.BlockSpec((tm, d), row), pl.BlockSpec((1, d), fixed), pl.BlockSpec((d, n), fixed),
                  pl.BlockSpec((tm, 128), row), pl.BlockSpec((tm, 128), row), pl.BlockSpec((tm, 128), row)],
        out_specs=[pl.BlockSpec((tm, 2 * gw), row), k_spec, vt_spec, k_spec, vt_spec],
        out_shape=[jax.ShapeDtypeStruct((s, 2 * gw), F32), k_shape, vt_shape, k_shape, vt_shape],
        compiler_params=_params(("arbitrary",)),
        name="kv_proj",
    )(h, gain.reshape(1, d), w, *rope_tabs)


def _cmp_kernel(n_cmp, x_ref, pos_ref, w1_ref, w2_ref, w2t_ref, o_ref, ot_ref):
    nch = x_ref.shape[0] // CMP_STRIDE
    half = CMP_STRIDE * HEAD_DIM
    w1 = w1_ref[...]
    bias = _dot(pos_ref[...].astype(BF16), w1)[0:1]
    acc = [[jnp.zeros((nch, CMP_HIDDEN), F32) for _ in range(2)] for _ in range(2)]
    for l in range(CMP_STRIDE):
        y = x_ref[pl.ds(l, nch, stride=CMP_STRIDE), :].astype(BF16)
        wa = w1[HEAD_DIM * l:HEAD_DIM * (l + 1)]
        wb = w1[half + HEAD_DIM * l:half + HEAD_DIM * (l + 1)]
        for gg in range(2):
            yg = y[:, HEAD_DIM * gg:HEAD_DIM * (gg + 1)]
            acc[gg][0] = acc[gg][0] + _dot(yg, wa)
            acc[gg][1] = acc[gg][1] + _dot(yg, wb)
    live = lax.broadcasted_iota(jnp.int32, (nch, HEAD_DIM), 0) < n_cmp
    live_t = lax.broadcasted_iota(jnp.int32, (HEAD_DIM, nch), 1) < n_cmp
    ones_row = (lax.broadcasted_iota(jnp.int32, (V_ROWS - HEAD_DIM, nch), 0) == 0).astype(BF16)
    for gg in range(2):
        hid = acc[gg][0] + pltpu.roll(acc[gg][1], nch - 1, axis=0) + bias
        hid = jax.nn.gelu(hid, approximate=True).astype(BF16)
        o_ref[gg] = jnp.where(live, _dot(hid, w2_ref[...]), 0.0).astype(BF16)
        ot_ref[gg, :HEAD_DIM, :] = jnp.where(live_t, _dot_nt(w2t_ref[...], hid), 0.0).astype(BF16)
        ot_ref[gg, HEAD_DIM:, :] = ones_row


def _cmp_mlp(cmp_kv, pos_flat, w1, w2, n_cmp):
    s = cmp_kv.shape[0]
    nch = s // CMP_STRIDE
    feat = CMP_LEN * HEAD_DIM
    return pl.pallas_call(
        functools.partial(_cmp_kernel, n_cmp),
        grid=(2, KV_GROUPS // 2),
        in_specs=[pl.BlockSpec((s, 2 * HEAD_DIM), lambda c, p: (0, c * (KV_GROUPS // 2) + p)),
                  pl.BlockSpec((None, 8, feat), lambda c, p: (c, 0, 0)),
                  pl.BlockSpec((None, feat, CMP_HIDDEN), lambda c, p: (c, 0, 0)),
                  pl.BlockSpec((None, CMP_HIDDEN, HEAD_DIM), lambda c, p: (c, 0, 0)),
                  pl.BlockSpec((None, HEAD_DIM, CMP_HIDDEN), lambda c, p: (c, 0, 0))],
        out_specs=[pl.BlockSpec((None, 2, nch, HEAD_DIM), lambda c, p: (c, p, 0, 0)),
                   pl.BlockSpec((None, 2, V_ROWS, nch), lambda c, p: (c, p, 0, 0))],
        out_shape=[jax.ShapeDtypeStruct((2, KV_GROUPS, nch, HEAD_DIM), BF16),
                   jax.ShapeDtypeStruct((2, KV_GROUPS, V_ROWS, nch), BF16)],
        compiler_params=_params(("arbitrary", "arbitrary")),
        name="cmp_mlp",
    )(cmp_kv, pos_flat, w1, w2, jnp.swapaxes(w2, 1, 2))


def _q_kernel(x_ref, g_ref, w_ref, b_ref, c_ref, s1_ref, s2_ref, q_ref, qr_ref, gate_ref):
    d = q_ref.shape[1]
    chunks = _row_chunks(x_ref.shape[0])
    accs = [_dot(_rms(x_ref[rows, :], g_ref[...]).astype(BF16), w_ref[...]) for rows in chunks]
    for rows, acc in zip(chunks, accs):
        c, s1, s2 = c_ref[rows, :], s1_ref[rows, :], s2_ref[rows, :]
        for p in range(d // 128):
            qp = acc[:, 128 * p:128 * (p + 1)] * (HEAD_DIM ** -0.5 * LOG2E)
            q_ref[rows, 128 * p:128 * (p + 1)] = qp.astype(BF16)
            qr_ref[rows, 128 * p:128 * (p + 1)] = _rope128(qp, c, s1, s2).astype(BF16)
        gates_t = (1.0 / (1.0 + jnp.exp(-(acc[:, d:] + b_ref[...])))).T
        for g in range(KV_GROUPS):
            gate_ref[g, :, rows] = gates_t[GATE_ROWS * g:GATE_ROWS * (g + 1)]


def _q_proj(h, gain, w, gate_b, rope_tabs):
    s, d = h.shape
    n = w.shape[1]
    tm = 2 * PROJ_SUB
    row = lambda i: (i, 0)
    fixed = lambda i: (0, 0)
    return pl.pallas_call(
        _q_kernel,
        grid=(s // tm,),
        in_specs=[pl.BlockSpec((tm, d), row), pl.BlockSpec((1, d), fixed), pl.BlockSpec((d, n), fixed),
                  pl.BlockSpec((1, 128), fixed),
                  pl.BlockSpec((tm, 128), row), pl.BlockSpec((tm, 128), row), pl.BlockSpec((tm, 128), row)],
        out_specs=[pl.BlockSpec((tm, d), row), pl.BlockSpec((tm, d), row),
                   pl.BlockSpec((KV_GROUPS, GATE_ROWS, tm), lambda i: (0, 0, i))],
        out_shape=[jax.ShapeDtypeStruct((s, d), BF16), jax.ShapeDtypeStruct((s, d), BF16),
                   jax.ShapeDtypeStruct((KV_GROUPS, GATE_ROWS, s), F32)],
        compiler_params=_params(("arbitrary",)),
        name="nsa_q_proj",
    )(h, gain.reshape(1, d), w, gate_b, *rope_tabs)


def _group_gate_columns(a):
    lead = a.shape[:-1]
    a = a.reshape(lead + (3, KV_GROUPS, HEADS_PER_GROUP))
    a = jnp.moveaxis(a, -2, -3).reshape(lead + (KV_GROUPS, 3 * HEADS_PER_GROUP))
    a = jnp.pad(a, [(0, 0)] * len(lead) + [(0, 0), (0, GATE_ROWS - 3 * HEADS_PER_GROUP)])
    a = a.reshape(lead + (KV_GROUPS * GATE_ROWS,))
    return jnp.pad(a, [(0, 0)] * len(lead) + [(0, 128 - KV_GROUPS * GATE_ROWS)])


def _stack_heads(q):
    return jnp.concatenate([q[:, HEAD_DIM * h:HEAD_DIM * (h + 1)] for h in range(HEADS_PER_GROUP)], axis=0)


def _head_cols(x, h):
    return x[:, Q_TILE * h:Q_TILE * (h + 1)]


def _softmax_numerators_t(st, mask):
    es = []
    for h in range(HEADS_PER_GROUP):
        s = mask(_head_cols(st, h))
        es.append(jnp.exp2(s - jnp.max(s, axis=0, keepdims=True)))
    return es


def _nsa_kernel(tk, q_ref, qr_ref, kc_ref, vct_ref, ovt_ref, ks_ref, vst_ref, kw_ref, vwt_ref, wbias_ref, gate_ref,
                o_ref, ocmp_ref, owin_ref, bias_ref, s0_ref, s1_ref, mx0_ref, mx1_ref, m_ref, acc_ref):
    i = pl.program_id(1)
    tq = q_ref.shape[0]
    nc = kc_ref.shape[0]
    n_sel = ovt_ref.shape[0]
    hp = HEADS_PER_GROUP
    q4 = _stack_heads(q_ref[...])
    qr4 = _stack_heads(qr_ref[...])
    tok = i * tq + lax.broadcasted_iota(jnp.int32, (1, tq), 1)
    gates = gate_ref[...]

    q0 = pl.multiple_of(i * tq, tq)
    span = WINDOW + tq
    wstart = pl.multiple_of(jnp.maximum(i * tq - WINDOW, 0), tq)

    def cmp_and_select(ncv, rows):
        sc = _dot_nt(kc_ref[:ncv, :], q4)
        sw = _dot_nt(kw_ref[pl.ds(wstart, span), :], qr4)
        sd = _dot_nt(ks_ref[pl.ds(q0, tq), :], qr4)

        n_idx = lax.broadcasted_iota(jnp.int32, (ncv, tq), 0)
        last_complete = lax.shift_right_arithmetic(tok - (CMP_LEN - 1), CMP_STRIDE.bit_length() - 1)
        es = _softmax_numerators_t(sc, lambda s: jnp.where(n_idx <= last_complete, s, NEG))
        wbias = wbias_ref[...]
        ew = _softmax_numerators_t(sw, lambda s: s + wbias)
        not_future = lax.broadcasted_iota(jnp.int32, (tq, tq), 0) <= lax.broadcasted_iota(jnp.int32, (tq, tq), 1)
        sd = jnp.concatenate([jnp.where(not_future, _head_cols(sd, h), NEG) for h in range(hp)], axis=1)
        m0 = jnp.max(sd, axis=0, keepdims=True)
        pd = jnp.exp2(sd - m0)

        o_cmp = _dot(vct_ref[:, :ncv], jnp.concatenate(es, axis=1).astype(BF16))
        o_win = _dot(vwt_ref[:, pl.ds(wstart, span)], jnp.concatenate(ew, axis=1).astype(BF16))
        m_ref[...] = m0
        acc_ref[...] = _dot(vst_ref[:, pl.ds(q0, tq)], pd.astype(BF16))
        inv_c = jnp.where(jnp.concatenate([tok >= CMP_LEN - 1] * hp, axis=1), 1.0 / o_cmp[HEAD_DIM:HEAD_DIM + 1], 0.0)
        ocmp_ref[...] = o_cmp[:HEAD_DIM] * inv_c
        owin_ref[...] = o_win[:HEAD_DIM] * (1.0 / o_win[HEAD_DIM:HEAD_DIM + 1])

        psum = sum(es[h] * _head_cols(inv_c, h) for h in range(hp))
        ovt = ovt_ref[:rows, :ncv]
        imp = sum(_dot(ovt, term) for term in _split3(psum))

        m_idx = lax.broadcasted_iota(jnp.int32, (rows, tq), 0)
        blk_t = tok // SEL_LEN
        forced = (m_idx == 0) | (m_idx == blk_t) | (m_idx == blk_t - 1)
        candidate = jnp.logical_and(m_idx <= blk_t, jnp.logical_not(forced))
        imp = jnp.where(candidate, imp, -jnp.inf)

        def pick(_, v):
            mx = jnp.max(v, axis=0, keepdims=True)
            idx = jnp.min(jnp.where(v == mx, m_idx, rows), axis=0, keepdims=True)
            return jnp.where(m_idx == idx, -jnp.inf, v)

        picked = lax.fori_loop(0, SEL_TOPK - 3, pick, imp, unroll=True) == -jnp.inf
        before = m_idx < (i * tq) // SEL_LEN
        bias_ref[:rows, :] = jnp.where(jnp.logical_and(picked, before), 0.0, NEG)
        if rows < n_sel:
            bias_ref[rows:, :] = jnp.full((n_sel - rows, tq), NEG, F32)

    rows_per_chunk = CMP_CHUNK * CMP_STRIDE // SEL_LEN
    n_prefix = -(-nc // CMP_CHUNK)
    need = ((i + 1) * tq - 1) // (CMP_STRIDE * CMP_CHUNK) + 1
    for c in range(1, n_prefix + 1):
        pl.when(need == c)(functools.partial(cmp_and_select, min(c * CMP_CHUNK, nc), min(c * rows_per_chunk, n_sel)))
    bpt = tk // SEL_LEN

    def sel_scores(kt, s_ref, mx_ref):
        start = pl.multiple_of(kt * tk, tk)
        st = _dot_nt(ks_ref[pl.ds(start, tk), :], qr4)
        rows = bias_ref[pl.ds(pl.multiple_of(kt * bpt, bpt), bpt), :]
        bias = jnp.concatenate([jnp.broadcast_to(rows[j:j + 1], (SEL_LEN, tq)) for j in range(bpt)], axis=0)
        s = jnp.concatenate([_head_cols(st, h) + bias for h in range(hp)], axis=1)
        s_ref[...] = s
        mx_ref[...] = jnp.max(s, axis=0, keepdims=True)

    def sel_accumulate(kt, s_ref, mx_ref):
        start = pl.multiple_of(kt * tk, tk)
        m = m_ref[...]
        m_new = jnp.maximum(m, mx_ref[...])
        p = jnp.exp2(s_ref[...] - m_new)
        acc_ref[...] = jnp.exp2(m - m_new) * acc_ref[...] + _dot(vst_ref[:, pl.ds(start, tk)], p.astype(BF16))
        m_ref[...] = m_new

    last = jnp.maximum(i * tq - 1, 0) // tk
    final_tile = ks_ref.shape[0] // tk - 1

    def sel_pair(j):
        sel_scores(2 * j + 1, s1_ref, mx1_ref)
        sel_accumulate(2 * j, s0_ref, mx0_ref)
        sel_scores(jnp.minimum(2 * j + 2, final_tile), s0_ref, mx0_ref)
        sel_accumulate(2 * j + 1, s1_ref, mx1_ref)

    def sel_pairs(jj, carry):
        for u in range(SEL_UNROLL):
            sel_pair(SEL_UNROLL * jj + u)
        return carry

    sel_scores(0, s0_ref, mx0_ref)
    pairs = (last + 1) // 2
    lax.fori_loop(0, pairs // SEL_UNROLL, sel_pairs, 0)
    for u in range(SEL_UNROLL - 1):
        pl.when(pairs % SEL_UNROLL > u)(functools.partial(sel_pair, pairs // SEL_UNROLL * SEL_UNROLL + u))

    @pl.when(last % 2 == 0)
    def _():
        sel_accumulate(last, s0_ref, mx0_ref)

    acc = acc_ref[...]
    o_sel = acc[:HEAD_DIM] * (1.0 / acc[HEAD_DIM:HEAD_DIM + 1])

    o_cmp = ocmp_ref[...]
    o_win = owin_ref[...]
    for h in range(hp):
        mix = (gates[h:h + 1] * _head_cols(o_cmp, h)
               + gates[hp + h:hp + h + 1] * _head_cols(o_sel, h)
               + gates[2 * hp + h:2 * hp + h + 1] * _head_cols(o_win, h))
        o_ref[:, HEAD_DIM * h:HEAD_DIM * (h + 1)] = mix.T.astype(BF16)


def _nsa_attention(q, q_rot, k_cmp, v_cmp_t, overlap_t, k_slc, v_slc_t, k_win, v_win_t, gates):
    s, d = q.shape
    nc = k_cmp.shape[1]
    n_sel = overlap_t.shape[0]
    gw = HEADS_PER_GROUP * HEAD_DIM
    cols = HEADS_PER_GROUP * Q_TILE
    tk = 512
    n_wcase = WINDOW // Q_TILE + 1
    lead = (jnp.arange(n_wcase) * Q_TILE)[:, None, None]
    dist = lead + jnp.arange(Q_TILE)[None, None, :] - jnp.arange(WINDOW + Q_TILE)[None, :, None]
    wbias = jnp.where((dist >= 0) & (dist < WINDOW), 0.0, NEG).astype(F32)
    qspec = pl.BlockSpec((Q_TILE, gw), lambda g, i: (i, g))
    per_group = lambda shape: pl.BlockSpec((None,) + shape, lambda g, i: (g, 0, 0))
    return pl.pallas_call(
        functools.partial(_nsa_kernel, tk),
        grid=(KV_GROUPS, s // Q_TILE),
        in_specs=[qspec, qspec,
                  per_group((nc, HEAD_DIM)), per_group((V_ROWS, nc)),
                  pl.BlockSpec((n_sel, nc), lambda g, i: (0, 0)),
                  per_group((s, HEAD_DIM)), per_group((V_ROWS, s)),
                  per_group((s, HEAD_DIM)), per_group((V_ROWS, s)),
                  pl.BlockSpec((None, WINDOW + Q_TILE, Q_TILE), lambda g, i: (jnp.minimum(i, n_wcase - 1), 0, 0)),
                  pl.BlockSpec((None, GATE_ROWS, Q_TILE), lambda g, i: (g, 0, i))],
        out_specs=qspec,
        out_shape=jax.ShapeDtypeStruct((s, d), BF16),
        scratch_shapes=[pltpu.VMEM((HEAD_DIM, cols), F32), pltpu.VMEM((HEAD_DIM, cols), F32),
                        pltpu.VMEM((n_sel, Q_TILE), F32),
                        pltpu.VMEM((tk, cols), F32), pltpu.VMEM((tk, cols), F32),
                        pltpu.VMEM((1, cols), F32), pltpu.VMEM((1, cols), F32),
                        pltpu.VMEM((1, cols), F32), pltpu.VMEM((V_ROWS, cols), F32)],
        compiler_params=_params(("arbitrary", "arbitrary")),
        name="nsa_attention",
    )(q, q_rot, k_cmp, v_cmp_t, overlap_t, k_slc, v_slc_t, k_win, v_win_t, wbias, gates)


def _rope_tables(s):
    half = ROT_DIM // 2
    inv_freq = ROPE_THETA ** (-jnp.arange(half, dtype=F32) * 2.0 / ROT_DIM)
    ang = jnp.arange(s, dtype=F32)[:, None] * inv_freq[None, :]
    cos, sin = jnp.cos(ang), jnp.sin(ang)
    rest = HEAD_DIM - ROT_DIM
    c = jnp.concatenate([cos, cos, jnp.ones((s, rest), F32)], axis=1)
    s1 = jnp.concatenate([-sin, jnp.zeros((s, half + rest), F32)], axis=1)
    s2 = jnp.concatenate([jnp.zeros((s, half), F32), sin, jnp.zeros((s, rest), F32)], axis=1)
    return tuple(jnp.tile(a, (1, 2)) for a in (c, s1, s2))


def _overlap_matrix(nch, n_sel):
    cmp_start = jnp.arange(nch)[:, None] * CMP_STRIDE
    sel_start = jnp.arange(n_sel)[None, :] * SEL_LEN
    return ((cmp_start < sel_start + SEL_LEN) & (cmp_start + CMP_LEN > sel_start)).astype(BF16)


def kernel(x, norm_gain, sb_w_qkv, sb_w_o, kv_norm, nsa_w_kv, cmp_pos, cmp_w1, cmp_w2,
           nsa_w_q, nsa_gate_b, nsa_w_o, mlp_w1, mlp_w2, final_norm):
    b, s, d = x.shape
    assert b == 1 and d == N_HEADS * HEAD_DIM
    assert s % 512 == 0 and s >= WINDOW + Q_TILE and s // SEL_LEN >= SEL_TOPK
    n_cmp = (s - CMP_LEN) // CMP_STRIDE + 1
    nch = s // CMP_STRIDE
    n_sel = s // SEL_LEN
    h0 = x[0]
    rope_tabs = _rope_tables(s)

    qkv = _qkv_proj(h0, norm_gain[0, 0], sb_w_qkv[0].astype(BF16))
    o_sb = _sb_attention(qkv)
    h1 = _attn_out_mlp(o_sb, h0, sb_w_o[0].astype(BF16), norm_gain[0, 1],
                       mlp_w1[0].astype(BF16), mlp_w2[0].astype(BF16))

    cmp_kv, k_slc, v_slc_t, k_win, v_win_t = _kv_proj(h1, kv_norm, nsa_w_kv.astype(BF16), rope_tabs)
    pos_flat = jnp.broadcast_to(cmp_pos.reshape(2, 1, CMP_LEN * HEAD_DIM), (2, 8, CMP_LEN * HEAD_DIM))
    kv_cmp, kv_cmp_t = _cmp_mlp(cmp_kv, pos_flat, cmp_w1.astype(BF16), cmp_w2.astype(BF16), n_cmp)

    n_qk = N_HEADS * HEAD_DIM
    w_q = jnp.concatenate([nsa_w_q[0][:, :n_qk], _group_gate_columns(nsa_w_q[0][:, n_qk:])], axis=1).astype(BF16)
    gate_b = _group_gate_columns(nsa_gate_b[0]).reshape(1, 128)
    q, q_rot, gates = _q_proj(h1, norm_gain[1, 0], w_q, gate_b, rope_tabs)
    o_nsa = _nsa_attention(q, q_rot, kv_cmp[0], kv_cmp_t[1], _overlap_matrix(nch, n_sel).T,
                           k_slc, v_slc_t, k_win, v_win_t, gates)
    out = _attn_out_mlp(o_nsa, h1, nsa_w_o[0].astype(BF16), norm_gain[1, 1],
                        mlp_w1[1].astype(BF16), mlp_w2[1].astype(BF16), final_gain=final_norm)
    return out[None]
```

```python
import functools

import jax
import jax.numpy as jnp
from jax import lax
from jax.experimental import pallas as pl
from jax.experimental.pallas import tpu as pltpu

HEAD_DIM = 64
N_HEADS = 16
KV_GROUPS = 4
HEADS_PER_GROUP = N_HEADS // KV_GROUPS
CMP_LEN = 32
CMP_STRIDE = 16
CMP_HIDDEN = 256
SEL_LEN = 64
SEL_TOPK = 16
WINDOW = 512
ROPE_THETA = 500000.0
ROT_DIM = HEAD_DIM // 4
Q_TILE = 128
GATE_ROWS = 16
V_ROWS = HEAD_DIM + 16
LOG2E = 1.4426950408889634
CMP_CHUNK = 256
SEL_UNROLL = 4
PROJ_SUB = 256
NORM_EPS = 1e-5
NEG = -1e30
FORCED_SCORE = 1e6
SB_UNDERFLOW = -110.0
SB_NEAR = 256
VMEM_LIMIT = 56 * 1024 * 1024

BF16 = jnp.bfloat16
F32 = jnp.float32


def _params(semantics, vmem=VMEM_LIMIT):
    return pltpu.CompilerParams(dimension_semantics=semantics, vmem_limit_bytes=vmem)


def _rms(x, g):
    return x * lax.rsqrt(jnp.mean(x * x, axis=-1, keepdims=True) + NORM_EPS) * g


def _dot(a, b):
    return jnp.dot(a, b, preferred_element_type=F32)


def _dot_nt(a, b):
    return lax.dot_general(a, b, (((1,), (1,)), ((), ())), preferred_element_type=F32)


def _rope128(x, c, s1, s2):
    return x * c + pltpu.roll(x, 128 - ROT_DIM // 2, axis=1) * s1 + pltpu.roll(x, ROT_DIM // 2, axis=1) * s2


def _split3(x):
    hi = x.astype(BF16)
    r = x - hi.astype(F32)
    mid = r.astype(BF16)
    lo = (r - mid.astype(F32)).astype(BF16)
    return hi, mid, lo


def _qkv_kernel(x_ref, g_ref, w_ref, o_ref):
    xn = _rms(x_ref[...], g_ref[...]).astype(BF16)
    acc = _dot(xn, w_ref[...])
    d = x_ref.shape[1]
    o_ref[:, :d] = (acc[:, :d] * (HEAD_DIM ** -0.5)).astype(BF16)
    o_ref[:, d:] = acc[:, d:].astype(BF16)


def _qkv_proj(x, gain, w):
    s, d = x.shape
    n = w.shape[1]
    tm = 256
    return pl.pallas_call(
        _qkv_kernel,
        grid=(s // tm,),
        in_specs=[pl.BlockSpec((tm, d), lambda i: (i, 0)),
                  pl.BlockSpec((1, d), lambda i: (0, 0)),
                  pl.BlockSpec((d, n), lambda i: (0, 0))],
        out_specs=pl.BlockSpec((tm, n), lambda i: (i, 0)),
        out_shape=jax.ShapeDtypeStruct((s, n), BF16),
        compiler_params=_params(("arbitrary",)),
        name="qkv_proj",
    )(x, gain.reshape(1, d), w)


def _sb_blocks(qs, ks, vs, tri, carries, mask):
    n = len(qs)
    zs = [_dot_nt(qs[h], ks[h]) for h in range(n)]
    sps = [jnp.maximum(z, 0.0) + jnp.log(1.0 + jnp.exp(-jnp.abs(z))) for z in zs]
    logs = [-sp for sp in sps]
    if mask is not None:
        logs = [jnp.where(mask, x, 0.0) for x in logs]
    excls = [_dot(logs[h].astype(BF16), tri) for h in range(n)]
    ws = [jnp.exp((zs[h] - sps[h]) + (excls[h] + carries[h])) for h in range(n)]
    if mask is not None:
        ws = [jnp.where(mask, w, 0.0) for w in ws]
    outs = [_dot(ws[h].astype(BF16), vs[h]) for h in range(n)]
    new_carries = [carries[h] + jnp.sum(logs[h], axis=1, keepdims=True) for h in range(n)]
    return outs, new_carries


def _sb_kernel(q_ref, k_ref, v_ref, tri_ref, o_ref):
    i = pl.program_id(1)
    tq = q_ref.shape[0]
    nh = q_ref.shape[1] // HEAD_DIM
    near = tri_ref.shape[0]
    tri_near = tri_ref[...]
    tri = tri_near[:tq, :tq]
    q = q_ref[...]
    head = lambda x, h: x[:, HEAD_DIM * h:HEAD_DIM * (h + 1)]

    start = pl.multiple_of(jnp.maximum(i * tq - (near - tq), 0), tq)
    row = lax.broadcasted_iota(jnp.int32, (tq, near), 0)
    col = lax.broadcasted_iota(jnp.int32, (tq, near), 1)
    causal = col - row < i * tq - start
    k = k_ref[pl.ds(start, near), :]
    v = v_ref[pl.ds(start, near), :]
    qs = [head(q, h) for h in range(nh)]
    split = lambda x: [head(x, h) for h in range(nh)]
    outs, carries = _sb_blocks(qs, split(k), split(v), tri_near, [jnp.zeros((tq, 1), F32)] * nh, causal)

    def worst(cs):
        mx = jnp.max(cs[0])
        for c in cs[1:]:
            mx = jnp.maximum(mx, jnp.max(c))
        return mx

    def cond(st):
        return jnp.logical_and(st[0] >= 0, st[1] > SB_UNDERFLOW)

    def body(st):
        kb, _, outs, carries = st
        start = pl.multiple_of(kb * tq, tq)
        k = k_ref[pl.ds(start, tq), :]
        v = v_ref[pl.ds(start, tq), :]
        more, new_c = _sb_blocks(qs, split(k), split(v), tri, list(carries), None)
        return kb - 1, worst(new_c), tuple(outs[h] + more[h] for h in range(nh)), tuple(new_c)

    st = lax.while_loop(cond, body, (start // tq - 1, worst(carries), tuple(outs), tuple(carries)))
    for h in range(nh):
        o_ref[:, HEAD_DIM * h:HEAD_DIM * (h + 1)] = st[2][h].astype(BF16)


def _sb_attention(qkv):
    s = qkv.shape[0]
    d = N_HEADS * HEAD_DIM
    hps = 8
    ngrp = N_HEADS // hps
    w = hps * HEAD_DIM
    near = SB_NEAR + Q_TILE
    idx = jnp.arange(near)
    tri = (idx[:, None] > idx[None, :]).astype(BF16)
    return pl.pallas_call(
        _sb_kernel,
        grid=(ngrp, s // Q_TILE),
        in_specs=[pl.BlockSpec((Q_TILE, w), lambda h, i: (i, h)),
                  pl.BlockSpec((s, w), lambda h, i: (0, ngrp + h), pipeline_mode=pl.Buffered(1)),
                  pl.BlockSpec((s, w), lambda h, i: (0, 2 * ngrp + h), pipeline_mode=pl.Buffered(1)),
                  pl.BlockSpec((near, near), lambda h, i: (0, 0))],
        out_specs=pl.BlockSpec((Q_TILE, w), lambda h, i: (i, h)),
        out_shape=jax.ShapeDtypeStruct((s, d), BF16),
        compiler_params=_params(("arbitrary", "arbitrary")),
        name="sb_attention",
    )(qkv, qkv, qkv, tri)


def _mlp_kernel(final, tf, o_ref, res_ref, wo_ref, g_ref, w1_ref, w2_ref, *rest):
    out_ref = rest[-1]
    h = res_ref[...] + _dot(o_ref[...], wo_ref[...])
    xn = _rms(h, g_ref[...]).astype(BF16)
    out_ref[...] = h
    for j in range(w1_ref.shape[1] // tf):
        a = jnp.maximum(_dot(xn, w1_ref[:, tf * j:tf * (j + 1)]), 0.0)
        out_ref[...] += _dot((a * a).astype(BF16), w2_ref[tf * j:tf * (j + 1), :])
    if final:
        out_ref[...] = _rms(out_ref[...], rest[0][...])


def _attn_out_mlp(o, resid, wo, gain, w1, w2, final_gain=None):
    s, d = resid.shape
    f = w1.shape[1]
    tm, tf = 512, 512
    final = final_gain is not None
    row = lambda i: (i, 0)
    resident = lambda shape: pl.BlockSpec(shape, lambda i: (0, 0), pipeline_mode=pl.Buffered(1))
    in_specs = [pl.BlockSpec((tm, d), row), pl.BlockSpec((tm, d), row), resident((d, d)), resident((1, d)),
                resident((d, f)), resident((f, d))]
    args = [o, resid, wo, gain.reshape(1, d), w1, w2]
    if final:
        in_specs.append(resident((1, d)))
        args.append(final_gain.reshape(1, d))
    return pl.pallas_call(
        functools.partial(_mlp_kernel, final, tf),
        grid=(s // tm,),
        in_specs=in_specs,
        out_specs=pl.BlockSpec((tm, d), row),
        out_shape=jax.ShapeDtypeStruct((s, d), F32),
        compiler_params=_params(("arbitrary",)),
        name="attn_out_mlp",
    )(*args)


def _row_chunks(tm):
    return [slice(r, r + PROJ_SUB) for r in range(0, tm, PROJ_SUB)]


def _kv_kernel(x_ref, g_ref, w_ref, c_ref, s1_ref, s2_ref, cmp_ref, ks_ref, vs_ref, kw_ref, vw_ref):
    gw = KV_GROUPS * HEAD_DIM
    chunks = _row_chunks(x_ref.shape[0])
    accs = [_dot(_rms(x_ref[rows, :], g_ref[...]).astype(BF16), w_ref[...]) for rows in chunks]
    ones_row = (lax.broadcasted_iota(jnp.int32, (V_ROWS - HEAD_DIM, PROJ_SUB), 0) == 0).astype(BF16)

    def put_k(ref, x, rows):
        c, s1, s2 = c_ref[rows, :], s1_ref[rows, :], s2_ref[rows, :]
        for p in range(gw // 128):
            xp = _rope128(x[:, 128 * p:128 * (p + 1)], c, s1, s2)
            ref[2 * p, rows, :] = xp[:, :HEAD_DIM].astype(BF16)
            ref[2 * p + 1, rows, :] = xp[:, HEAD_DIM:].astype(BF16)

    def put_vt(ref, x, rows):
        xt = x.T
        for g in range(KV_GROUPS):
            ref[g, :HEAD_DIM, rows] = xt[HEAD_DIM * g:HEAD_DIM * (g + 1)].astype(BF16)
            ref[g, HEAD_DIM:, rows] = ones_row

    for rows, acc in zip(chunks, accs):
        cmp_ref[rows, :] = acc[:, :2 * gw]
        put_k(ks_ref, acc[:, 2 * gw:3 * gw], rows)
        put_vt(vs_ref, acc[:, 3 * gw:4 * gw], rows)
        put_k(kw_ref, acc[:, 4 * gw:5 * gw], rows)
        put_vt(vw_ref, acc[:, 5 * gw:6 * gw], rows)


def _kv_proj(h, gain, w, rope_tabs):
    s, d = h.shape
    n = w.shape[1]
    gw = KV_GROUPS * HEAD_DIM
    tm = 2 * PROJ_SUB
    row = lambda i: (i, 0)
    fixed = lambda i: (0, 0)
    k_spec = pl.BlockSpec((KV_GROUPS, tm, HEAD_DIM), lambda i: (0, i, 0))
    k_shape = jax.ShapeDtypeStruct((KV_GROUPS, s, HEAD_DIM), BF16)
    vt_spec = pl.BlockSpec((KV_GROUPS, V_ROWS, tm), lambda i: (0, 0, i))
    vt_shape = jax.ShapeDtypeStruct((KV_GROUPS, V_ROWS, s), BF16)
    return pl.pallas_call(
        _kv_kernel,
        grid=(s // tm,),
        in_specs=[pl.BlockSpec((tm, d), row), pl.BlockSpec((1, d), fixed), pl.BlockSpec((d, n), fixed),
                  pl.BlockSpec((tm, 128), row), pl.BlockSpec((tm, 128), row), pl.BlockSpec((tm, 128), row)],
        out_specs=[pl.BlockSpec((tm, 2 * gw), row), k_spec, vt_spec, k_spec, vt_spec],
        out_shape=[jax.ShapeDtypeStruct((s, 2 * gw), F32), k_shape, vt_shape, k_shape, vt_shape],
        compiler_params=_params(("arbitrary",)),
        name="kv_proj",
    )(h, gain.reshape(1, d), w, *rope_tabs)


def _cmp_kernel(n_cmp, x_ref, pos_ref, w1_ref, w2_ref, w2t_ref, o_ref, ot_ref):
    nch = x_ref.shape[0] // CMP_STRIDE
    half = CMP_STRIDE * HEAD_DIM
    w1 = w1_ref[...]
    bias = _dot(pos_ref[...].astype(BF16), w1)[0:1]
    acc = [[jnp.zeros((nch, CMP_HIDDEN), F32) for _ in range(2)] for _ in range(2)]
    for l in range(CMP_STRIDE):
        y = x_ref[pl.ds(l, nch, stride=CMP_STRIDE), :].astype(BF16)
        wa = w1[HEAD_DIM * l:HEAD_DIM * (l + 1)]
        wb = w1[half + HEAD_DIM * l:half + HEAD_DIM * (l + 1)]
        for gg in range(2):
            yg = y[:, HEAD_DIM * gg:HEAD_DIM * (gg + 1)]
            acc[gg][0] = acc[gg][0] + _dot(yg, wa)
            acc[gg][1] = acc[gg][1] + _dot(yg, wb)
    live = lax.broadcasted_iota(jnp.int32, (nch, HEAD_DIM), 0) < n_cmp
    live_t = lax.broadcasted_iota(jnp.int32, (HEAD_DIM, nch), 1) < n_cmp
    ones_row = (lax.broadcasted_iota(jnp.int32, (V_ROWS - HEAD_DIM, nch), 0) == 0).astype(BF16)
    for gg in range(2):
        hid = acc[gg][0] + pltpu.roll(acc[gg][1], nch - 1, axis=0) + bias
        hid = jax.nn.gelu(hid, approximate=True).astype(BF16)
        o_ref[gg] = jnp.where(live, _dot(hid, w2_ref[...]), 0.0).astype(BF16)
        ot_ref[gg, :HEAD_DIM, :] = jnp.where(live_t, _dot_nt(w2t_ref[...], hid), 0.0).astype(BF16)
        ot_ref[gg, HEAD_DIM:, :] = ones_row


def _cmp_mlp(cmp_kv, pos_flat, w1, w2, n_cmp):
    s = cmp_kv.shape[0]
    nch = s // CMP_STRIDE
    feat = CMP_LEN * HEAD_DIM
    return pl.pallas_call(
        functools.partial(_cmp_kernel, n_cmp),
        grid=(2, KV_GROUPS // 2),
        in_specs=[pl.BlockSpec((s, 2 * HEAD_DIM), lambda c, p: (0, c * (KV_GROUPS // 2) + p)),
                  pl.BlockSpec((None, 8, feat), lambda c, p: (c, 0, 0)),
                  pl.BlockSpec((None, feat, CMP_HIDDEN), lambda c, p: (c, 0, 0)),
                  pl.BlockSpec((None, CMP_HIDDEN, HEAD_DIM), lambda c, p: (c, 0, 0)),
                  pl.BlockSpec((None, HEAD_DIM, CMP_HIDDEN), lambda c, p: (c, 0, 0))],
        out_specs=[pl.BlockSpec((None, 2, nch, HEAD_DIM), lambda c, p: (c, p, 0, 0)),
                   pl.BlockSpec((None, 2, V_ROWS, nch), lambda c, p: (c, p, 0, 0))],
        out_shape=[jax.ShapeDtypeStruct((2, KV_GROUPS, nch, HEAD_DIM), BF16),
                   jax.ShapeDtypeStruct((2, KV_GROUPS, V_ROWS, nch), BF16)],
        compiler_params=_params(("arbitrary", "arbitrary")),
        name="cmp_mlp",
    )(cmp_kv, pos_flat, w1, w2, jnp.swapaxes(w2, 1, 2))


def _q_kernel(x_ref, g_ref, w_ref, b_ref, c_ref, s1_ref, s2_ref, q_ref, qr_ref, gate_ref):
    d = q_ref.shape[1]
    chunks = _row_chunks(x_ref.shape[0])
    accs = [_dot(_rms(x_ref[rows, :], g_ref[...]).astype(BF16), w_ref[...]) for rows in chunks]
    for rows, acc in zip(chunks, accs):
        c, s1, s2 = c_ref[rows, :], s1_ref[rows, :], s2_ref[rows, :]
        for p in range(d // 128):
            qp = acc[:, 128 * p:128 * (p + 1)] * (HEAD_DIM ** -0.5 * LOG2E)
            q_ref[rows, 128 * p:128 * (p + 1)] = qp.astype(BF16)
            qr_ref[rows, 128 * p:128 * (p + 1)] = _rope128(qp, c, s1, s2).astype(BF16)
        gates_t = (1.0 / (1.0 + jnp.exp(-(acc[:, d:] + b_ref[...])))).T
        for g in range(KV_GROUPS):
            gate_ref[g, :, rows] = gates_t[GATE_ROWS * g:GATE_ROWS * (g + 1)]


def _q_proj(h, gain, w, gate_b, rope_tabs):
    s, d = h.shape
    n = w.shape[1]
    tm = 2 * PROJ_SUB
    row = lambda i: (i, 0)
    fixed = lambda i: (0, 0)
    return pl.pallas_call(
        _q_kernel,
        grid=(s // tm,),
        in_specs=[pl.BlockSpec((tm, d), row), pl.BlockSpec((1, d), fixed), pl.BlockSpec((d, n), fixed),
                  pl.BlockSpec((1, 128), fixed),
                  pl.BlockSpec((tm, 128), row), pl.BlockSpec((tm, 128), row), pl.BlockSpec((tm, 128), row)],
        out_specs=[pl.BlockSpec((tm, d), row), pl.BlockSpec((tm, d), row),
                   pl.BlockSpec((KV_GROUPS, GATE_ROWS, tm), lambda i: (0, 0, i))],
        out_shape=[jax.ShapeDtypeStruct((s, d), BF16), jax.ShapeDtypeStruct((s, d), BF16),
                   jax.ShapeDtypeStruct((KV_GROUPS, GATE_ROWS, s), F32)],
        compiler_params=_params(("arbitrary",)),
        name="nsa_q_proj",
    )(h, gain.reshape(1, d), w, gate_b, *rope_tabs)


def _group_gate_columns(a):
    lead = a.shape[:-1]
    a = a.reshape(lead + (3, KV_GROUPS, HEADS_PER_GROUP))
    a = jnp.moveaxis(a, -2, -3).reshape(lead + (KV_GROUPS, 3 * HEADS_PER_GROUP))
    a = jnp.pad(a, [(0, 0)] * len(lead) + [(0, 0), (0, GATE_ROWS - 3 * HEADS_PER_GROUP)])
    a = a.reshape(lead + (KV_GROUPS * GATE_ROWS,))
    return jnp.pad(a, [(0, 0)] * len(lead) + [(0, 128 - KV_GROUPS * GATE_ROWS)])


def _stack_heads_t(q):
    qt = q.astype(F32).T
    return jnp.concatenate([qt[HEAD_DIM * h:HEAD_DIM * (h + 1)] for h in range(HEADS_PER_GROUP)], axis=1).astype(BF16)


def _head_cols(x, h):
    return x[:, Q_TILE * h:Q_TILE * (h + 1)]


def _softmax_numerators_t(st, mask):
    es = []
    for h in range(HEADS_PER_GROUP):
        s = mask(_head_cols(st, h))
        es.append(jnp.exp2(s - jnp.max(s, axis=0, keepdims=True)))
    return es


def _nsa_kernel(tk, q_ref, qr_ref, kc_ref, vct_ref, ovt_ref, ks_ref, vst_ref, kw_ref, vwt_ref, wbias_ref, gate_ref,
                o_ref, ocmp_ref, owin_ref, bias_ref, s0_ref, s1_ref, mx0_ref, mx1_ref, m_ref, acc_ref):
    i = pl.program_id(1)
    tq = q_ref.shape[0]
    nc = kc_ref.shape[0]
    n_sel = ovt_ref.shape[0]
    hp = HEADS_PER_GROUP
    q4 = _stack_heads_t(q_ref[...])
    qr4 = _stack_heads_t(qr_ref[...])
    tok = i * tq + lax.broadcasted_iota(jnp.int32, (1, tq), 1)
    gates = gate_ref[...]

    q0 = pl.multiple_of(i * tq, tq)
    span = WINDOW + tq
    wstart = pl.multiple_of(jnp.maximum(i * tq - WINDOW, 0), tq)

    def cmp_and_select(ncv, rows):
        sc = _dot(kc_ref[:ncv, :], q4)
        sw = _dot(kw_ref[pl.ds(wstart, span), :], qr4)
        sd = _dot(ks_ref[pl.ds(q0, tq), :], qr4)

        n_idx = lax.broadcasted_iota(jnp.int32, (ncv, tq), 0)
        last_complete = lax.shift_right_arithmetic(tok - (CMP_LEN - 1), CMP_STRIDE.bit_length() - 1)
        es = _softmax_numerators_t(sc, lambda s: jnp.where(n_idx <= last_complete, s, NEG))
        wbias = wbias_ref[...]
        ew = _softmax_numerators_t(sw, lambda s: s + wbias)
        not_future = lax.broadcasted_iota(jnp.int32, (tq, tq), 0) <= lax.broadcasted_iota(jnp.int32, (tq, tq), 1)
        sd = jnp.concatenate([jnp.where(not_future, _head_cols(sd, h), NEG) for h in range(hp)], axis=1)
        m0 = jnp.max(sd, axis=0, keepdims=True)
        pd = jnp.exp2(sd - m0)

        o_cmp = _dot(vct_ref[:, :ncv], jnp.concatenate(es, axis=1).astype(BF16))
        o_win = _dot(vwt_ref[:, pl.ds(wstart, span)], jnp.concatenate(ew, axis=1).astype(BF16))
        m_ref[...] = m0
        acc_ref[...] = _dot(vst_ref[:, pl.ds(q0, tq)], pd.astype(BF16))
        inv_c = jnp.where(jnp.concatenate([tok >= CMP_LEN - 1] * hp, axis=1), 1.0 / o_cmp[HEAD_DIM:HEAD_DIM + 1], 0.0)
        ocmp_ref[...] = o_cmp[:HEAD_DIM] * inv_c
        owin_ref[...] = o_win[:HEAD_DIM] * (1.0 / o_win[HEAD_DIM:HEAD_DIM + 1])

        psum = sum(es[h] * _head_cols(inv_c, h) for h in range(hp))
        ovt = ovt_ref[:rows, :ncv]
        imp = sum(_dot(ovt, term) for term in _split3(psum))

        m_idx = lax.broadcasted_iota(jnp.int32, (rows, tq), 0)
        blk_t = tok // SEL_LEN
        forced = (m_idx == 0) | (m_idx == blk_t) | (m_idx == blk_t - 1)
        candidate = jnp.logical_and(m_idx <= blk_t, jnp.logical_not(forced))
        imp = jnp.where(candidate, imp, -jnp.inf)

        def pick(_, v):
            mx = jnp.max(v, axis=0, keepdims=True)
            idx = jnp.min(jnp.where(v == mx, m_idx, rows), axis=0, keepdims=True)
            return jnp.where(m_idx == idx, -jnp.inf, v)

        picked = lax.fori_loop(0, SEL_TOPK - 3, pick, imp, unroll=True) == -jnp.inf
        before = m_idx < (i * tq) // SEL_LEN
        bias_ref[:rows, :] = jnp.where(jnp.logical_and(picked, before), 0.0, NEG)
        if rows < n_sel:
            bias_ref[rows:, :] = jnp.full((n_sel - rows, tq), NEG, F32)

    rows_per_chunk = CMP_CHUNK * CMP_STRIDE // SEL_LEN
    n_prefix = -(-nc // CMP_CHUNK)
    need = ((i + 1) * tq - 1) // (CMP_STRIDE * CMP_CHUNK) + 1
    for c in range(1, n_prefix + 1):
        pl.when(need == c)(functools.partial(cmp_and_select, min(c * CMP_CHUNK, nc), min(c * rows_per_chunk, n_sel)))
    bpt = tk // SEL_LEN

    def sel_scores(kt, s_ref, mx_ref):
        start = pl.multiple_of(kt * tk, tk)
        st = _dot(ks_ref[pl.ds(start, tk), :], qr4)
        rows = bias_ref[pl.ds(pl.multiple_of(kt * bpt, bpt), bpt), :]
        bias = jnp.concatenate([jnp.broadcast_to(rows[j:j + 1], (SEL_LEN, tq)) for j in range(bpt)], axis=0)
        s = jnp.concatenate([_head_cols(st, h) + bias for h in range(hp)], axis=1)
        s_ref[...] = s
        mx_ref[...] = jnp.max(s, axis=0, keepdims=True)

    def sel_accumulate(kt, s_ref, mx_ref):
        start = pl.multiple_of(kt * tk, tk)
        m = m_ref[...]
        m_new = jnp.maximum(m, mx_ref[...])
        p = jnp.exp2(s_ref[...] - m_new)
        acc_ref[...] = jnp.exp2(m - m_new) * acc_ref[...] + _dot(vst_ref[:, pl.ds(start, tk)], p.astype(BF16))
        m_ref[...] = m_new

    last = jnp.maximum(i * tq - 1, 0) // tk
    final_tile = ks_ref.shape[0] // tk - 1

    def sel_pair(j):
        sel_scores(2 * j + 1, s1_ref, mx1_ref)
        sel_accumulate(2 * j, s0_ref, mx0_ref)
        sel_scores(jnp.minimum(2 * j + 2, final_tile), s0_ref, mx0_ref)
        sel_accumulate(2 * j + 1, s1_ref, mx1_ref)

    def sel_pairs(jj, carry):
        for u in range(SEL_UNROLL):
            sel_pair(SEL_UNROLL * jj + u)
        return carry

    sel_scores(0, s0_ref, mx0_ref)
    pairs = (last + 1) // 2
    lax.fori_loop(0, pairs // SEL_UNROLL, sel_pairs, 0)
    done = pairs // SEL_UNROLL * SEL_UNROLL
    run = SEL_UNROLL // 2
    while run >= 1:
        def sel_run(done=done, run=run):
            for u in range(run):
                sel_pair(done + u)

        take = (pairs - done) >= run
        pl.when(take)(sel_run)
        done = done + jnp.where(take, run, 0)
        run //= 2

    @pl.when(last % 2 == 0)
    def _():
        sel_accumulate(last, s0_ref, mx0_ref)

    acc = acc_ref[...]
    o_sel = acc[:HEAD_DIM] * (1.0 / acc[HEAD_DIM:HEAD_DIM + 1])

    o_cmp = ocmp_ref[...]
    o_win = owin_ref[...]
    for h in range(hp):
        mix = (gates[h:h + 1] * _head_cols(o_cmp, h)
               + gates[hp + h:hp + h + 1] * _head_cols(o_sel, h)
               + gates[2 * hp + h:2 * hp + h + 1] * _head_cols(o_win, h))
        o_ref[:, HEAD_DIM * h:HEAD_DIM * (h + 1)] = mix.T.astype(BF16)


def _nsa_attention(q, q_rot, k_cmp, v_cmp_t, overlap_t, k_slc, v_slc_t, k_win, v_win_t, gates):
    s, d = q.shape
    nc = k_cmp.shape[1]
    n_sel = overlap_t.shape[0]
    gw = HEADS_PER_GROUP * HEAD_DIM
    cols = HEADS_PER_GROUP * Q_TILE
    tk = 512
    n_wcase = WINDOW // Q_TILE + 1
    lead = (jnp.arange(n_wcase) * Q_TILE)[:, None, None]
    dist = lead + jnp.arange(Q_TILE)[None, None, :] - jnp.arange(WINDOW + Q_TILE)[None, :, None]
    wbias = jnp.where((dist >= 0) & (dist < WINDOW), 0.0, NEG).astype(F32)
    qspec = pl.BlockSpec((Q_TILE, gw), lambda g, i: (i, g))
    per_group = lambda shape: pl.BlockSpec((None,) + shape, lambda g, i: (g, 0, 0))
    return pl.pallas_call(
        functools.partial(_nsa_kernel, tk),
        grid=(KV_GROUPS, s // Q_TILE),
        in_specs=[qspec, qspec,
                  per_group((nc, HEAD_DIM)), per_group((V_ROWS, nc)),
                  pl.BlockSpec((n_sel, nc), lambda g, i: (0, 0)),
                  per_group((s, HEAD_DIM)), per_group((V_ROWS, s)),
                  per_group((s, HEAD_DIM)), per_group((V_ROWS, s)),
                  pl.BlockSpec((None, WINDOW + Q_TILE, Q_TILE), lambda g, i: (jnp.minimum(i, n_wcase - 1), 0, 0)),
                  pl.BlockSpec((None, GATE_ROWS, Q_TILE), lambda g, i: (g, 0, i))],
        out_specs=qspec,
        out_shape=jax.ShapeDtypeStruct((s, d), BF16),
        scratch_shapes=[pltpu.VMEM((HEAD_DIM, cols), F32), pltpu.VMEM((HEAD_DIM, cols), F32),
                        pltpu.VMEM((n_sel, Q_TILE), F32),
                        pltpu.VMEM((tk, cols), F32), pltpu.VMEM((tk, cols), F32),
                        pltpu.VMEM((1, cols), F32), pltpu.VMEM((1, cols), F32),
                        pltpu.VMEM((1, cols), F32), pltpu.VMEM((V_ROWS, cols), F32)],
        compiler_params=_params(("arbitrary", "arbitrary")),
        name="nsa_attention",
    )(q, q_rot, k_cmp, v_cmp_t, overlap_t, k_slc, v_slc_t, k_win, v_win_t, wbias, gates)


def _rope_tables(s):
    half = ROT_DIM // 2
    inv_freq = ROPE_THETA ** (-jnp.arange(half, dtype=F32) * 2.0 / ROT_DIM)
    ang = jnp.arange(s, dtype=F32)[:, None] * inv_freq[None, :]
    cos, sin = jnp.cos(ang), jnp.sin(ang)
    rest = HEAD_DIM - ROT_DIM
    c = jnp.concatenate([cos, cos, jnp.ones((s, rest), F32)], axis=1)
    s1 = jnp.concatenate([-sin, jnp.zeros((s, half + rest), F32)], axis=1)
    s2 = jnp.concatenate([jnp.zeros((s, half), F32), sin, jnp.zeros((s, rest), F32)], axis=1)
    return tuple(jnp.tile(a, (1, 2)) for a in (c, s1, s2))


def _overlap_matrix(nch, n_sel):
    cmp_start = jnp.arange(nch)[:, None] * CMP_STRIDE
    sel_start = jnp.arange(n_sel)[None, :] * SEL_LEN
    return ((cmp_start < sel_start + SEL_LEN) & (cmp_start + CMP_LEN > sel_start)).astype(BF16)


def kernel(x, norm_gain, sb_w_qkv, sb_w_o, kv_norm, nsa_w_kv, cmp_pos, cmp_w1, cmp_w2,
           nsa_w_q, nsa_gate_b, nsa_w_o, mlp_w1, mlp_w2, final_norm):
    b, s, d = x.shape
    assert b == 1 and d == N_HEADS * HEAD_DIM
    assert s % 512 == 0 and s >= WINDOW + Q_TILE and s // SEL_LEN >= SEL_TOPK
    n_cmp = (s - CMP_LEN) // CMP_STRIDE + 1
    nch = s // CMP_STRIDE
    n_sel = s // SEL_LEN
    h0 = x[0]
    rope_tabs = _rope_tables(s)

    qkv = _qkv_proj(h0, norm_gain[0, 0], sb_w_qkv[0].astype(BF16))
    o_sb = _sb_attention(qkv)
    h1 = _attn_out_mlp(o_sb, h0, sb_w_o[0].astype(BF16), norm_gain[0, 1],
                       mlp_w1[0].astype(BF16), mlp_w2[0].astype(BF16))

    cmp_kv, k_slc, v_slc_t, k_win, v_win_t = _kv_proj(h1, kv_norm, nsa_w_kv.astype(BF16), rope_tabs)
    pos_flat = jnp.broadcast_to(cmp_pos.reshape(2, 1, CMP_LEN * HEAD_DIM), (2, 8, CMP_LEN * HEAD_DIM))
    kv_cmp, kv_cmp_t = _cmp_mlp(cmp_kv, pos_flat, cmp_w1.astype(BF16), cmp_w2.astype(BF16), n_cmp)

    n_qk = N_HEADS * HEAD_DIM
    w_q = jnp.concatenate([nsa_w_q[0][:, :n_qk], _group_gate_columns(nsa_w_q[0][:, n_qk:])], axis=1).astype(BF16)
    gate_b = _group_gate_columns(nsa_gate_b[0]).reshape(1, 128)
    q, q_rot, gates = _q_proj(h1, norm_gain[1, 0], w_q, gate_b, rope_tabs)
    o_nsa = _nsa_attention(q, q_rot, kv_cmp[0], kv_cmp_t[1], _overlap_matrix(nch, n_sel).T,
                           k_slc, v_slc_t, k_win, v_win_t, gates)
    out = _attn_out_mlp(o_nsa, h1, nsa_w_o[0].astype(BF16), norm_gain[1, 1],
                        mlp_w1[1].astype(BF16), mlp_w2[1].astype(BF16), final_gain=final_norm)
    return out[None]
```

```python
import functools

import jax
import jax.numpy as jnp
from jax import lax
from jax.experimental import pallas as pl
from jax.experimental.pallas import tpu as pltpu

HEAD_DIM = 64
N_HEADS = 16
KV_GROUPS = 4
HEADS_PER_GROUP = N_HEADS // KV_GROUPS
CMP_LEN = 32
CMP_STRIDE = 16
CMP_HIDDEN = 256
SEL_LEN = 64
SEL_TOPK = 16
WINDOW = 512
ROPE_THETA = 500000.0
ROT_DIM = HEAD_DIM // 4
Q_TILE = 128
GATE_ROWS = 16
V_ROWS = HEAD_DIM + 16
LOG2E = 1.4426950408889634
CMP_CHUNK = 256
SEL_UNROLL = 4
PROJ_SUB = 256
NORM_EPS = 1e-5
NEG = -1e30
FORCED_SCORE = 1e6
SB_UNDERFLOW = -110.0
SB_NEAR = 256
VMEM_LIMIT = 56 * 1024 * 1024

BF16 = jnp.bfloat16
F32 = jnp.float32


def _params(semantics, vmem=VMEM_LIMIT):
    return pltpu.CompilerParams(dimension_semantics=semantics, vmem_limit_bytes=vmem)


def _rms(x, g):
    return x * lax.rsqrt(jnp.mean(x * x, axis=-1, keepdims=True) + NORM_EPS) * g


def _dot(a, b):
    return jnp.dot(a, b, preferred_element_type=F32)


def _dot_nt(a, b):
    return lax.dot_general(a, b, (((1,), (1,)), ((), ())), preferred_element_type=F32)


def _rope128(x, c, s1, s2):
    return x * c + pltpu.roll(x, 128 - ROT_DIM // 2, axis=1) * s1 + pltpu.roll(x, ROT_DIM // 2, axis=1) * s2


def _split3(x):
    hi = x.astype(BF16)
    r = x - hi.astype(F32)
    mid = r.astype(BF16)
    lo = (r - mid.astype(F32)).astype(BF16)
    return hi, mid, lo


def _qkv_kernel(x_ref, g_ref, w_ref, q_ref, kt_ref, v_ref):
    xn = _rms(x_ref[...], g_ref[...]).astype(BF16)
    acc = _dot(xn, w_ref[...])
    d = x_ref.shape[1]
    q_ref[...] = (acc[:, :d] * (HEAD_DIM ** -0.5)).astype(BF16)
    kt_ref[...] = acc[:, d:2 * d].T.astype(BF16)
    v_ref[...] = acc[:, 2 * d:].astype(BF16)


def _qkv_proj(x, gain, w):
    s, d = x.shape
    n = w.shape[1]
    tm = 256
    row = pl.BlockSpec((tm, d), lambda i: (i, 0))
    return pl.pallas_call(
        _qkv_kernel,
        grid=(s // tm,),
        in_specs=[row, pl.BlockSpec((1, d), lambda i: (0, 0)), pl.BlockSpec((d, n), lambda i: (0, 0))],
        out_specs=[row, pl.BlockSpec((d, tm), lambda i: (0, i)), row],
        out_shape=[jax.ShapeDtypeStruct((s, d), BF16), jax.ShapeDtypeStruct((d, s), BF16),
                   jax.ShapeDtypeStruct((s, d), BF16)],
        compiler_params=_params(("arbitrary",)),
        name="qkv_proj",
    )(x, gain.reshape(1, d), w)


def _sb_blocks(qs, kts, vs, tri, carries, mask):
    n = len(qs)
    zs = [_dot(qs[h], kts[h]) for h in range(n)]
    sps = [jnp.maximum(z, 0.0) + jnp.log(1.0 + jnp.exp(-jnp.abs(z))) for z in zs]
    logs = [-sp for sp in sps]
    if mask is not None:
        logs = [jnp.where(mask, x, 0.0) for x in logs]
    excls = [_dot(logs[h].astype(BF16), tri) for h in range(n)]
    ws = [jnp.exp((zs[h] - sps[h]) + (excls[h] + carries[h])) for h in range(n)]
    if mask is not None:
        ws = [jnp.where(mask, w, 0.0) for w in ws]
    outs = [_dot(ws[h].astype(BF16), vs[h]) for h in range(n)]
    new_carries = [carries[h] + jnp.sum(logs[h], axis=1, keepdims=True) for h in range(n)]
    return outs, new_carries


def _sb_kernel(q_ref, kt_ref, v_ref, tri_ref, o_ref):
    i = pl.program_id(1)
    tq = q_ref.shape[0]
    nh = q_ref.shape[1] // HEAD_DIM
    near = tri_ref.shape[0]
    tri_near = tri_ref[...]
    tri = tri_near[:tq, :tq]
    q = q_ref[...]
    head = lambda x, h: x[:, HEAD_DIM * h:HEAD_DIM * (h + 1)]

    start = pl.multiple_of(jnp.maximum(i * tq - (near - tq), 0), tq)
    row = lax.broadcasted_iota(jnp.int32, (tq, near), 0)
    col = lax.broadcasted_iota(jnp.int32, (tq, near), 1)
    causal = col - row < i * tq - start
    kt = kt_ref[:, pl.ds(start, near)]
    v = v_ref[pl.ds(start, near), :]
    qs = [head(q, h) for h in range(nh)]
    split = lambda x: [head(x, h) for h in range(nh)]
    split_t = lambda x: [x[HEAD_DIM * h:HEAD_DIM * (h + 1)] for h in range(nh)]
    outs, carries = _sb_blocks(qs, split_t(kt), split(v), tri_near, [jnp.zeros((tq, 1), F32)] * nh, causal)

    def worst(cs):
        mx = jnp.max(cs[0])
        for c in cs[1:]:
            mx = jnp.maximum(mx, jnp.max(c))
        return mx

    def cond(st):
        return jnp.logical_and(st[0] >= 0, st[1] > SB_UNDERFLOW)

    def body(st):
        kb, _, outs, carries = st
        start = pl.multiple_of(kb * tq, tq)
        kt = kt_ref[:, pl.ds(start, tq)]
        v = v_ref[pl.ds(start, tq), :]
        more, new_c = _sb_blocks(qs, split_t(kt), split(v), tri, list(carries), None)
        return kb - 1, worst(new_c), tuple(outs[h] + more[h] for h in range(nh)), tuple(new_c)

    st = lax.while_loop(cond, body, (start // tq - 1, worst(carries), tuple(outs), tuple(carries)))
    for h in range(nh):
        o_ref[:, HEAD_DIM * h:HEAD_DIM * (h + 1)] = st[2][h].astype(BF16)


def _sb_attention(q, k_t, v):
    s, d = q.shape
    hps = 8
    ngrp = N_HEADS // hps
    w = hps * HEAD_DIM
    near = SB_NEAR + Q_TILE
    idx = jnp.arange(near)
    tri = (idx[:, None] > idx[None, :]).astype(BF16)
    return pl.pallas_call(
        _sb_kernel,
        grid=(ngrp, s // Q_TILE),
        in_specs=[pl.BlockSpec((Q_TILE, w), lambda h, i: (i, h)),
                  pl.BlockSpec((w, s), lambda h, i: (h, 0), pipeline_mode=pl.Buffered(1)),
                  pl.BlockSpec((s, w), lambda h, i: (0, h), pipeline_mode=pl.Buffered(1)),
                  pl.BlockSpec((near, near), lambda h, i: (0, 0))],
        out_specs=pl.BlockSpec((Q_TILE, w), lambda h, i: (i, h)),
        out_shape=jax.ShapeDtypeStruct((s, d), BF16),
        compiler_params=_params(("arbitrary", "arbitrary")),
        name="sb_attention",
    )(q, k_t, v, tri)


def _mlp_kernel(final, tf, o_ref, res_ref, wo_ref, g_ref, w1_ref, w2_ref, *rest):
    out_ref = rest[-1]
    h = res_ref[...] + _dot(o_ref[...], wo_ref[...])
    xn = _rms(h, g_ref[...]).astype(BF16)
    out_ref[...] = h
    for j in range(w1_ref.shape[1] // tf):
        a = jnp.maximum(_dot(xn, w1_ref[:, tf * j:tf * (j + 1)]), 0.0)
        out_ref[...] += _dot((a * a).astype(BF16), w2_ref[tf * j:tf * (j + 1), :])
    if final:
        out_ref[...] = _rms(out_ref[...], rest[0][...])


def _attn_out_mlp(o, resid, wo, gain, w1, w2, final_gain=None):
    s, d = resid.shape
    f = w1.shape[1]
    tm, tf = 512, 512
    final = final_gain is not None
    row = lambda i: (i, 0)
    resident = lambda shape: pl.BlockSpec(shape, lambda i: (0, 0), pipeline_mode=pl.Buffered(1))
    in_specs = [pl.BlockSpec((tm, d), row), pl.BlockSpec((tm, d), row), resident((d, d)), resident((1, d)),
                resident((d, f)), resident((f, d))]
    args = [o, resid, wo, gain.reshape(1, d), w1, w2]
    if final:
        in_specs.append(resident((1, d)))
        args.append(final_gain.reshape(1, d))
    return pl.pallas_call(
        functools.partial(_mlp_kernel, final, tf),
        grid=(s // tm,),
        in_specs=in_specs,
        out_specs=pl.BlockSpec((tm, d), row),
        out_shape=jax.ShapeDtypeStruct((s, d), F32),
        compiler_params=_params(("arbitrary",)),
        name="attn_out_mlp",
    )(*args)


def _row_chunks(tm):
    return [slice(r, r + PROJ_SUB) for r in range(0, tm, PROJ_SUB)]


def _kv_kernel(x_ref, g_ref, w_ref, c_ref, s1_ref, s2_ref, cmp_ref, ks_ref, vs_ref, kw_ref, vw_ref):
    gw = KV_GROUPS * HEAD_DIM
    chunks = _row_chunks(x_ref.shape[0])
    accs = [_dot(_rms(x_ref[rows, :], g_ref[...]).astype(BF16), w_ref[...]) for rows in chunks]
    ones_row = (lax.broadcasted_iota(jnp.int32, (V_ROWS - HEAD_DIM, PROJ_SUB), 0) == 0).astype(BF16)

    def put_k(ref, x, rows):
        c, s1, s2 = c_ref[rows, :], s1_ref[rows, :], s2_ref[rows, :]
        for p in range(gw // 128):
            xp = _rope128(x[:, 128 * p:128 * (p + 1)], c, s1, s2)
            ref[2 * p, rows, :] = xp[:, :HEAD_DIM].astype(BF16)
            ref[2 * p + 1, rows, :] = xp[:, HEAD_DIM:].astype(BF16)

    def put_vt(ref, x, rows):
        xt = x.T
        for g in range(KV_GROUPS):
            ref[g, :HEAD_DIM, rows] = xt[HEAD_DIM * g:HEAD_DIM * (g + 1)].astype(BF16)
            ref[g, HEAD_DIM:, rows] = ones_row

    for rows, acc in zip(chunks, accs):
        cmp_ref[rows, :] = acc[:, :2 * gw]
        put_k(ks_ref, acc[:, 2 * gw:3 * gw], rows)
        put_vt(vs_ref, acc[:, 3 * gw:4 * gw], rows)
        put_k(kw_ref, acc[:, 4 * gw:5 * gw], rows)
        put_vt(vw_ref, acc[:, 5 * gw:6 * gw], rows)


def _kv_proj(h, gain, w, rope_tabs):
    s, d = h.shape
    n = w.shape[1]
    gw = KV_GROUPS * HEAD_DIM
    tm = 2 * PROJ_SUB
    row = lambda i: (i, 0)
    fixed = lambda i: (0, 0)
    k_spec = pl.BlockSpec((KV_GROUPS, tm, HEAD_DIM), lambda i: (0, i, 0))
    k_shape = jax.ShapeDtypeStruct((KV_GROUPS, s, HEAD_DIM), BF16)
    vt_spec = pl.BlockSpec((KV_GROUPS, V_ROWS, tm), lambda i: (0, 0, i))
    vt_shape = jax.ShapeDtypeStruct((KV_GROUPS, V_ROWS, s), BF16)
    return pl.pallas_call(
        _kv_kernel,
        grid=(s // tm,),
        in_specs=[pl.BlockSpec((tm, d), row), pl.BlockSpec((1, d), fixed), pl.BlockSpec((d, n), fixed),
                  pl.BlockSpec((tm, 128), row), pl.BlockSpec((tm, 128), row), pl.BlockSpec((tm, 128), row)],
        out_specs=[pl.BlockSpec((tm, 2 * gw), row), k_spec, vt_spec, k_spec, vt_spec],
        out_shape=[jax.ShapeDtypeStruct((s, 2 * gw), F32), k_shape, vt_shape, k_shape, vt_shape],
        compiler_params=_params(("arbitrary",)),
        name="kv_proj",
    )(h, gain.reshape(1, d), w, *rope_tabs)


def _cmp_kernel(n_cmp, x_ref, pos_ref, w1_ref, w2_ref, w2t_ref, o_ref, ot_ref):
    nch = x_ref.shape[0] // CMP_STRIDE
    half = CMP_STRIDE * HEAD_DIM
    w1 = w1_ref[...]
    bias = _dot(pos_ref[...].astype(BF16), w1)[0:1]
    acc = [[jnp.zeros((nch, CMP_HIDDEN), F32) for _ in range(2)] for _ in range(2)]
    for l in range(CMP_STRIDE):
        y = x_ref[pl.ds(l, nch, stride=CMP_STRIDE), :].astype(BF16)
        wa = w1[HEAD_DIM * l:HEAD_DIM * (l + 1)]
        wb = w1[half + HEAD_DIM * l:half + HEAD_DIM * (l + 1)]
        for gg in range(2):
            yg = y[:, HEAD_DIM * gg:HEAD_DIM * (gg + 1)]
            acc[gg][0] = acc[gg][0] + _dot(yg, wa)
            acc[gg][1] = acc[gg][1] + _dot(yg, wb)
    live = lax.broadcasted_iota(jnp.int32, (nch, HEAD_DIM), 0) < n_cmp
    live_t = lax.broadcasted_iota(jnp.int32, (HEAD_DIM, nch), 1) < n_cmp
    ones_row = (lax.broadcasted_iota(jnp.int32, (V_ROWS - HEAD_DIM, nch), 0) == 0).astype(BF16)
    for gg in range(2):
        hid = acc[gg][0] + pltpu.roll(acc[gg][1], nch - 1, axis=0) + bias
        hid = jax.nn.gelu(hid, approximate=True).astype(BF16)
        o_ref[gg] = jnp.where(live, _dot(hid, w2_ref[...]), 0.0).astype(BF16)
        ot_ref[gg, :HEAD_DIM, :] = jnp.where(live_t, _dot_nt(w2t_ref[...], hid), 0.0).astype(BF16)
        ot_ref[gg, HEAD_DIM:, :] = ones_row


def _cmp_mlp(cmp_kv, pos_flat, w1, w2, n_cmp):
    s = cmp_kv.shape[0]
    nch = s // CMP_STRIDE
    feat = CMP_LEN * HEAD_DIM
    return pl.pallas_call(
        functools.partial(_cmp_kernel, n_cmp),
        grid=(2, KV_GROUPS // 2),
        in_specs=[pl.BlockSpec((s, 2 * HEAD_DIM), lambda c, p: (0, c * (KV_GROUPS // 2) + p)),
                  pl.BlockSpec((None, 8, feat), lambda c, p: (c, 0, 0)),
                  pl.BlockSpec((None, feat, CMP_HIDDEN), lambda c, p: (c, 0, 0)),
                  pl.BlockSpec((None, CMP_HIDDEN, HEAD_DIM), lambda c, p: (c, 0, 0)),
                  pl.BlockSpec((None, HEAD_DIM, CMP_HIDDEN), lambda c, p: (c, 0, 0))],
        out_specs=[pl.BlockSpec((None, 2, nch, HEAD_DIM), lambda c, p: (c, p, 0, 0)),
                   pl.BlockSpec((None, 2, V_ROWS, nch), lambda c, p: (c, p, 0, 0))],
        out_shape=[jax.ShapeDtypeStruct((2, KV_GROUPS, nch, HEAD_DIM), BF16),
                   jax.ShapeDtypeStruct((2, KV_GROUPS, V_ROWS, nch), BF16)],
        compiler_params=_params(("arbitrary", "arbitrary")),
        name="cmp_mlp",
    )(cmp_kv, pos_flat, w1, w2, jnp.swapaxes(w2, 1, 2))


def _q_kernel(x_ref, g_ref, w_ref, b_ref, c_ref, s1_ref, s2_ref, q_ref, qr_ref, gate_ref):
    d = q_ref.shape[1]
    chunks = _row_chunks(x_ref.shape[0])
    accs = [_dot(_rms(x_ref[rows, :], g_ref[...]).astype(BF16), w_ref[...]) for rows in chunks]
    for rows, acc in zip(chunks, accs):
        c, s1, s2 = c_ref[rows, :], s1_ref[rows, :], s2_ref[rows, :]
        for p in range(d // 128):
            qp = acc[:, 128 * p:128 * (p + 1)] * (HEAD_DIM ** -0.5 * LOG2E)
            q_ref[rows, 128 * p:128 * (p + 1)] = qp.astype(BF16)
            qr_ref[rows, 128 * p:128 * (p + 1)] = _rope128(qp, c, s1, s2).astype(BF16)
        gates_t = (1.0 / (1.0 + jnp.exp(-(acc[:, d:] + b_ref[...])))).T
        for g in range(KV_GROUPS):
            gate_ref[g, :, rows] = gates_t[GATE_ROWS * g:GATE_ROWS * (g + 1)]


def _q_proj(h, gain, w, gate_b, rope_tabs):
    s, d = h.shape
    n = w.shape[1]
    tm = 2 * PROJ_SUB
    row = lambda i: (i, 0)
    fixed = lambda i: (0, 0)
    return pl.pallas_call(
        _q_kernel,
        grid=(s // tm,),
        in_specs=[pl.BlockSpec((tm, d), row), pl.BlockSpec((1, d), fixed), pl.BlockSpec((d, n), fixed),
                  pl.BlockSpec((1, 128), fixed),
                  pl.BlockSpec((tm, 128), row), pl.BlockSpec((tm, 128), row), pl.BlockSpec((tm, 128), row)],
        out_specs=[pl.BlockSpec((tm, d), row), pl.BlockSpec((tm, d), row),
                   pl.BlockSpec((KV_GROUPS, GATE_ROWS, tm), lambda i: (0, 0, i))],
        out_shape=[jax.ShapeDtypeStruct((s, d), BF16), jax.ShapeDtypeStruct((s, d), BF16),
                   jax.ShapeDtypeStruct((KV_GROUPS, GATE_ROWS, s), F32)],
        compiler_params=_params(("arbitrary",)),
        name="nsa_q_proj",
    )(h, gain.reshape(1, d), w, gate_b, *rope_tabs)


def _group_gate_columns(a):
    lead = a.shape[:-1]
    a = a.reshape(lead + (3, KV_GROUPS, HEADS_PER_GROUP))
    a = jnp.moveaxis(a, -2, -3).reshape(lead + (KV_GROUPS, 3 * HEADS_PER_GROUP))
    a = jnp.pad(a, [(0, 0)] * len(lead) + [(0, 0), (0, GATE_ROWS - 3 * HEADS_PER_GROUP)])
    a = a.reshape(lead + (KV_GROUPS * GATE_ROWS,))
    return jnp.pad(a, [(0, 0)] * len(lead) + [(0, 128 - KV_GROUPS * GATE_ROWS)])


def _stack_heads_t(q):
    qt = q.astype(F32).T
    return jnp.concatenate([qt[HEAD_DIM * h:HEAD_DIM * (h + 1)] for h in range(HEADS_PER_GROUP)], axis=1).astype(BF16)


def _head_cols(x, h):
    return x[:, Q_TILE * h:Q_TILE * (h + 1)]


def _softmax_numerators_t(st, mask):
    es = []
    for h in range(HEADS_PER_GROUP):
        s = mask(_head_cols(st, h))
        es.append(jnp.exp2(s - jnp.max(s, axis=0, keepdims=True)))
    return es


def _nsa_kernel(tk, q_ref, qr_ref, kc_ref, vct_ref, ovt_ref, ks_ref, vst_ref, kw_ref, vwt_ref, wbias_ref, gate_ref,
                o_ref, ocmp_ref, owin_ref, bias_ref, s0_ref, s1_ref, mx0_ref, mx1_ref, m_ref, acc_ref):
    i = pl.program_id(1)
    tq = q_ref.shape[0]
    nc = kc_ref.shape[0]
    n_sel = ovt_ref.shape[0]
    hp = HEADS_PER_GROUP
    q4 = _stack_heads_t(q_ref[...])
    qr4 = _stack_heads_t(qr_ref[...])
    tok = i * tq + lax.broadcasted_iota(jnp.int32, (1, tq), 1)
    gates = gate_ref[...]

    q0 = pl.multiple_of(i * tq, tq)
    span = WINDOW + tq
    wstart = pl.multiple_of(jnp.maximum(i * tq - WINDOW, 0), tq)

    def cmp_and_select(ncv, rows):
        sc = _dot(kc_ref[:ncv, :], q4)
        sw = _dot(kw_ref[pl.ds(wstart, span), :], qr4)
        sd = _dot(ks_ref[pl.ds(q0, tq), :], qr4)

        n_idx = lax.broadcasted_iota(jnp.int32, (ncv, tq), 0)
        last_complete = lax.shift_right_arithmetic(tok - (CMP_LEN - 1), CMP_STRIDE.bit_length() - 1)
        es = _softmax_numerators_t(sc, lambda s: jnp.where(n_idx <= last_complete, s, NEG))
        wbias = wbias_ref[...]
        ew = _softmax_numerators_t(sw, lambda s: s + wbias)
        not_future = lax.broadcasted_iota(jnp.int32, (tq, tq), 0) <= lax.broadcasted_iota(jnp.int32, (tq, tq), 1)
        sd = jnp.concatenate([jnp.where(not_future, _head_cols(sd, h), NEG) for h in range(hp)], axis=1)
        m0 = jnp.max(sd, axis=0, keepdims=True)
        pd = jnp.exp2(sd - m0)

        o_cmp = _dot(vct_ref[:, :ncv], jnp.concatenate(es, axis=1).astype(BF16))
        o_win = _dot(vwt_ref[:, pl.ds(wstart, span)], jnp.concatenate(ew, axis=1).astype(BF16))
        m_ref[...] = m0
        acc_ref[...] = _dot(vst_ref[:, pl.ds(q0, tq)], pd.astype(BF16))
        inv_c = jnp.where(jnp.concatenate([tok >= CMP_LEN - 1] * hp, axis=1), 1.0 / o_cmp[HEAD_DIM:HEAD_DIM + 1], 0.0)
        ocmp_ref[...] = o_cmp[:HEAD_DIM] * inv_c
        owin_ref[...] = o_win[:HEAD_DIM] * (1.0 / o_win[HEAD_DIM:HEAD_DIM + 1])

        psum = sum(es[h] * _head_cols(inv_c, h) for h in range(hp))
        ovt = ovt_ref[:rows, :ncv]
        imp = sum(_dot(ovt, term) for term in _split3(psum))

        m_idx = lax.broadcasted_iota(jnp.int32, (rows, tq), 0)
        blk_t = tok // SEL_LEN
        forced = (m_idx == 0) | (m_idx == blk_t) | (m_idx == blk_t - 1)
        candidate = jnp.logical_and(m_idx <= blk_t, jnp.logical_not(forced))
        imp = jnp.where(candidate, imp, -jnp.inf)

        def pick(_, v):
            mx = jnp.max(v, axis=0, keepdims=True)
            idx = jnp.min(jnp.where(v == mx, m_idx, rows), axis=0, keepdims=True)
            return jnp.where(m_idx == idx, -jnp.inf, v)

        picked = lax.fori_loop(0, SEL_TOPK - 3, pick, imp, unroll=True) == -jnp.inf
        before = m_idx < (i * tq) // SEL_LEN
        bias_ref[:rows, :] = jnp.where(jnp.logical_and(picked, before), 0.0, NEG)
        if rows < n_sel:
            bias_ref[rows:, :] = jnp.full((n_sel - rows, tq), NEG, F32)

    rows_per_chunk = CMP_CHUNK * CMP_STRIDE // SEL_LEN
    n_prefix = -(-nc // CMP_CHUNK)
    need = ((i + 1) * tq - 1) // (CMP_STRIDE * CMP_CHUNK) + 1
    for c in range(1, n_prefix + 1):
        pl.when(need == c)(functools.partial(cmp_and_select, min(c * CMP_CHUNK, nc), min(c * rows_per_chunk, n_sel)))
    bpt = tk // SEL_LEN

    def sel_scores(kt, s_ref, mx_ref):
        start = pl.multiple_of(kt * tk, tk)
        st = _dot(ks_ref[pl.ds(start, tk), :], qr4)
        rows = bias_ref[pl.ds(pl.multiple_of(kt * bpt, bpt), bpt), :]
        bias = jnp.concatenate([jnp.broadcast_to(rows[j:j + 1], (SEL_LEN, tq)) for j in range(bpt)], axis=0)
        s = jnp.concatenate([_head_cols(st, h) + bias for h in range(hp)], axis=1)
        s_ref[...] = s
        mx_ref[...] = jnp.max(s, axis=0, keepdims=True)

    def sel_accumulate(kt, s_ref, mx_ref):
        start = pl.multiple_of(kt * tk, tk)
        m = m_ref[...]
        m_new = jnp.maximum(m, mx_ref[...])
        p = jnp.exp2(s_ref[...] - m_new)
        acc_ref[...] = jnp.exp2(m - m_new) * acc_ref[...] + _dot(vst_ref[:, pl.ds(start, tk)], p.astype(BF16))
        m_ref[...] = m_new

    last = jnp.maximum(i * tq - 1, 0) // tk
    final_tile = ks_ref.shape[0] // tk - 1

    def sel_pair(j):
        sel_scores(2 * j + 1, s1_ref, mx1_ref)
        sel_accumulate(2 * j, s0_ref, mx0_ref)
        sel_scores(jnp.minimum(2 * j + 2, final_tile), s0_ref, mx0_ref)
        sel_accumulate(2 * j + 1, s1_ref, mx1_ref)

    def sel_pairs(jj, carry):
        for u in range(SEL_UNROLL):
            sel_pair(SEL_UNROLL * jj + u)
        return carry

    sel_scores(0, s0_ref, mx0_ref)
    pairs = (last + 1) // 2
    lax.fori_loop(0, pairs // SEL_UNROLL, sel_pairs, 0)
    done = pairs // SEL_UNROLL * SEL_UNROLL
    run = SEL_UNROLL // 2
    while run >= 1:
        def sel_run(done=done, run=run):
            for u in range(run):
                sel_pair(done + u)

        take = (pairs - done) >= run
        pl.when(take)(sel_run)
        done = done + jnp.where(take, run, 0)
        run //= 2

    @pl.when(last % 2 == 0)
    def _():
        sel_accumulate(last, s0_ref, mx0_ref)

    acc = acc_ref[...]
    o_sel = acc[:HEAD_DIM] * (1.0 / acc[HEAD_DIM:HEAD_DIM + 1])

    o_cmp = ocmp_ref[...]
    o_win = owin_ref[...]
    for h in range(hp):
        mix = (gates[h:h + 1] * _head_cols(o_cmp, h)
               + gates[hp + h:hp + h + 1] * _head_cols(o_sel, h)
               + gates[2 * hp + h:2 * hp + h + 1] * _head_cols(o_win, h))
        o_ref[:, HEAD_DIM * h:HEAD_DIM * (h + 1)] = mix.T.astype(BF16)


def _nsa_attention(q, q_rot, k_cmp, v_cmp_t, overlap_t, k_slc, v_slc_t, k_win, v_win_t, gates):
    s, d = q.shape
    nc = k_cmp.shape[1]
    n_sel = overlap_t.shape[0]
    gw = HEADS_PER_GROUP * HEAD_DIM
    cols = HEADS_PER_GROUP * Q_TILE
    tk = 512
    n_wcase = WINDOW // Q_TILE + 1
    lead = (jnp.arange(n_wcase) * Q_TILE)[:, None, None]
    dist = lead + jnp.arange(Q_TILE)[None, None, :] - jnp.arange(WINDOW + Q_TILE)[None, :, None]
    wbias = jnp.where((dist >= 0) & (dist < WINDOW), 0.0, NEG).astype(F32)
    qspec = pl.BlockSpec((Q_TILE, gw), lambda g, i: (i, g))
    per_group = lambda shape: pl.BlockSpec((None,) + shape, lambda g, i: (g, 0, 0))
    return pl.pallas_call(
        functools.partial(_nsa_kernel, tk),
        grid=(KV_GROUPS, s // Q_TILE),
        in_specs=[qspec, qspec,
                  per_group((nc, HEAD_DIM)), per_group((V_ROWS, nc)),
                  pl.BlockSpec((n_sel, nc), lambda g, i: (0, 0)),
                  per_group((s, HEAD_DIM)), per_group((V_ROWS, s)),
                  per_group((s, HEAD_DIM)), per_group((V_ROWS, s)),
                  pl.BlockSpec((None, WINDOW + Q_TILE, Q_TILE), lambda g, i: (jnp.minimum(i, n_wcase - 1), 0, 0)),
                  pl.BlockSpec((None, GATE_ROWS, Q_TILE), lambda g, i: (g, 0, i))],
        out_specs=qspec,
        out_shape=jax.ShapeDtypeStruct((s, d), BF16),
        scratch_shapes=[pltpu.VMEM((HEAD_DIM, cols), F32), pltpu.VMEM((HEAD_DIM, cols), F32),
                        pltpu.VMEM((n_sel, Q_TILE), F32),
                        pltpu.VMEM((tk, cols), F32), pltpu.VMEM((tk, cols), F32),
                        pltpu.VMEM((1, cols), F32), pltpu.VMEM((1, cols), F32),
                        pltpu.VMEM((1, cols), F32), pltpu.VMEM((V_ROWS, cols), F32)],
        compiler_params=_params(("arbitrary", "arbitrary")),
        name="nsa_attention",
    )(q, q_rot, k_cmp, v_cmp_t, overlap_t, k_slc, v_slc_t, k_win, v_win_t, wbias, gates)


def _rope_tables(s):
    half = ROT_DIM // 2
    inv_freq = ROPE_THETA ** (-jnp.arange(half, dtype=F32) * 2.0 / ROT_DIM)
    ang = jnp.arange(s, dtype=F32)[:, None] * inv_freq[None, :]
    cos, sin = jnp.cos(ang), jnp.sin(ang)
    rest = HEAD_DIM - ROT_DIM
    c = jnp.concatenate([cos, cos, jnp.ones((s, rest), F32)], axis=1)
    s1 = jnp.concatenate([-sin, jnp.zeros((s, half + rest), F32)], axis=1)
    s2 = jnp.concatenate([jnp.zeros((s, half), F32), sin, jnp.zeros((s, rest), F32)], axis=1)
    return tuple(jnp.tile(a, (1, 2)) for a in (c, s1, s2))


def _overlap_matrix(nch, n_sel):
    cmp_start = jnp.arange(nch)[:, None] * CMP_STRIDE
    sel_start = jnp.arange(n_sel)[None, :] * SEL_LEN
    return ((cmp_start < sel_start + SEL_LEN) & (cmp_start + CMP_LEN > sel_start)).astype(BF16)


def kernel(x, norm_gain, sb_w_qkv, sb_w_o, kv_norm, nsa_w_kv, cmp_pos, cmp_w1, cmp_w2,
           nsa_w_q, nsa_gate_b, nsa_w_o, mlp_w1, mlp_w2, final_norm):
    b, s, d = x.shape
    assert b == 1 and d == N_HEADS * HEAD_DIM
    assert s % 512 == 0 and s >= WINDOW + Q_TILE and s // SEL_LEN >= SEL_TOPK
    n_cmp = (s - CMP_LEN) // CMP_STRIDE + 1
    nch = s // CMP_STRIDE
    n_sel = s // SEL_LEN
    h0 = x[0]
    rope_tabs = _rope_tables(s)

    o_sb = _sb_attention(*_qkv_proj(h0, norm_gain[0, 0], sb_w_qkv[0].astype(BF16)))
    h1 = _attn_out_mlp(o_sb, h0, sb_w_o[0].astype(BF16), norm_gain[0, 1],
                       mlp_w1[0].astype(BF16), mlp_w2[0].astype(BF16))

    cmp_kv, k_slc, v_slc_t, k_win, v_win_t = _kv_proj(h1, kv_norm, nsa_w_kv.astype(BF16), rope_tabs)
    pos_flat = jnp.broadcast_to(cmp_pos.reshape(2, 1, CMP_LEN * HEAD_DIM), (2, 8, CMP_LEN * HEAD_DIM))
    kv_cmp, kv_cmp_t = _cmp_mlp(cmp_kv, pos_flat, cmp_w1.astype(BF16), cmp_w2.astype(BF16), n_cmp)

    n_qk = N_HEADS * HEAD_DIM
    w_q = jnp.concatenate([nsa_w_q[0][:, :n_qk], _group_gate_columns(nsa_w_q[0][:, n_qk:])], axis=1).astype(BF16)
    gate_b = _group_gate_columns(nsa_gate_b[0]).reshape(1, 128)
    q, q_rot, gates = _q_proj(h1, norm_gain[1, 0], w_q, gate_b, rope_tabs)
    o_nsa = _nsa_attention(q, q_rot, kv_cmp[0], kv_cmp_t[1], _overlap_matrix(nch, n_sel).T,
                           k_slc, v_slc_t, k_win, v_win_t, gates)
    out = _attn_out_mlp(o_nsa, h1, nsa_w_o[0].astype(BF16), norm_gain[1, 1],
                        mlp_w1[1].astype(BF16), mlp_w2[1].astype(BF16), final_gain=final_norm)
    return out[None]
```

```python
import functools

import jax
import jax.numpy as jnp
from jax import lax
from jax.experimental import pallas as pl
from jax.experimental.pallas import tpu as pltpu

HEAD_DIM = 64
N_HEADS = 16
KV_GROUPS = 4
HEADS_PER_GROUP = N_HEADS // KV_GROUPS
CMP_LEN = 32
CMP_STRIDE = 16
CMP_HIDDEN = 256
SEL_LEN = 64
SEL_TOPK = 16
WINDOW = 512
ROPE_THETA = 500000.0
ROT_DIM = HEAD_DIM // 4
Q_TILE = 128
GATE_ROWS = 16
V_ROWS = HEAD_DIM + 16
LOG2E = 1.4426950408889634
CMP_CHUNK = 256
SEL_UNROLL = 4
PROJ_SUB = 256
NORM_EPS = 1e-5
NEG = -1e30
FORCED_SCORE = 1e6
SB_UNDERFLOW = -110.0
SB_NEAR = 256
VMEM_LIMIT = 56 * 1024 * 1024

BF16 = jnp.bfloat16
F32 = jnp.float32


def _params(semantics, vmem=VMEM_LIMIT):
    return pltpu.CompilerParams(dimension_semantics=semantics, vmem_limit_bytes=vmem)


def _rms(x, g):
    return x * lax.rsqrt(jnp.mean(x * x, axis=-1, keepdims=True) + NORM_EPS) * g


def _dot(a, b):
    return jnp.dot(a, b, preferred_element_type=F32)


def _rope128(x, c, s1, s2):
    return x * c + pltpu.roll(x, 128 - ROT_DIM // 2, axis=1) * s1 + pltpu.roll(x, ROT_DIM // 2, axis=1) * s2


def _split3(x):
    hi = x.astype(BF16)
    r = x - hi.astype(F32)
    mid = r.astype(BF16)
    lo = (r - mid.astype(F32)).astype(BF16)
    return hi, mid, lo


def _qkv_kernel(x_ref, g_ref, w_ref, q_ref, kt_ref, v_ref):
    xn = _rms(x_ref[...], g_ref[...]).astype(BF16)
    acc = _dot(xn, w_ref[...])
    d = x_ref.shape[1]
    q_ref[...] = (acc[:, :d] * (HEAD_DIM ** -0.5)).astype(BF16)
    kt_ref[...] = acc[:, d:2 * d].T.astype(BF16)
    v_ref[...] = acc[:, 2 * d:].astype(BF16)


def _qkv_proj(x, gain, w):
    s, d = x.shape
    n = w.shape[1]
    tm = 256
    row = pl.BlockSpec((tm, d), lambda i: (i, 0))
    return pl.pallas_call(
        _qkv_kernel,
        grid=(s // tm,),
        in_specs=[row, pl.BlockSpec((1, d), lambda i: (0, 0)), pl.BlockSpec((d, n), lambda i: (0, 0))],
        out_specs=[row, pl.BlockSpec((d, tm), lambda i: (0, i)), row],
        out_shape=[jax.ShapeDtypeStruct((s, d), BF16), jax.ShapeDtypeStruct((d, s), BF16),
                   jax.ShapeDtypeStruct((s, d), BF16)],
        compiler_params=_params(("arbitrary",)),
        name="qkv_proj",
    )(x, gain.reshape(1, d), w)


def _sb_blocks(qs, kts, vs, tri, carries, mask):
    n = len(qs)
    zs = [_dot(qs[h], kts[h]) for h in range(n)]
    sps = [jnp.maximum(z, 0.0) + jnp.log(1.0 + jnp.exp(-jnp.abs(z))) for z in zs]
    logs = [-sp for sp in sps]
    if mask is not None:
        logs = [jnp.where(mask, x, 0.0) for x in logs]
    excls = [_dot(logs[h].astype(BF16), tri) for h in range(n)]
    ws = [jnp.exp((zs[h] - sps[h]) + (excls[h] + carries[h])) for h in range(n)]
    if mask is not None:
        ws = [jnp.where(mask, w, 0.0) for w in ws]
    outs = [_dot(ws[h].astype(BF16), vs[h]) for h in range(n)]
    new_carries = [carries[h] + jnp.sum(logs[h], axis=1, keepdims=True) for h in range(n)]
    return outs, new_carries


def _sb_kernel(q_ref, kt_ref, v_ref, tri_ref, o_ref):
    i = pl.program_id(1)
    tq = q_ref.shape[0]
    nh = q_ref.shape[1] // HEAD_DIM
    near = tri_ref.shape[0]
    tri_near = tri_ref[...]
    tri = tri_near[:tq, :tq]
    q = q_ref[...]
    head = lambda x, h: x[:, HEAD_DIM * h:HEAD_DIM * (h + 1)]

    start = pl.multiple_of(jnp.maximum(i * tq - (near - tq), 0), tq)
    row = lax.broadcasted_iota(jnp.int32, (tq, near), 0)
    col = lax.broadcasted_iota(jnp.int32, (tq, near), 1)
    causal = col - row < i * tq - start
    kt = kt_ref[:, pl.ds(start, near)]
    v = v_ref[pl.ds(start, near), :]
    qs = [head(q, h) for h in range(nh)]
    split = lambda x: [head(x, h) for h in range(nh)]
    split_t = lambda x: [x[HEAD_DIM * h:HEAD_DIM * (h + 1)] for h in range(nh)]
    outs, carries = _sb_blocks(qs, split_t(kt), split(v), tri_near, [jnp.zeros((tq, 1), F32)] * nh, causal)

    def worst(cs):
        mx = jnp.max(cs[0])
        for c in cs[1:]:
            mx = jnp.maximum(mx, jnp.max(c))
        return mx

    def cond(st):
        return jnp.logical_and(st[0] >= 0, st[1] > SB_UNDERFLOW)

    def body(st):
        kb, _, outs, carries = st
        start = pl.multiple_of(kb * tq, tq)
        kt = kt_ref[:, pl.ds(start, tq)]
        v = v_ref[pl.ds(start, tq), :]
        more, new_c = _sb_blocks(qs, split_t(kt), split(v), tri, list(carries), None)
        return kb - 1, worst(new_c), tuple(outs[h] + more[h] for h in range(nh)), tuple(new_c)

    st = lax.while_loop(cond, body, (start // tq - 1, worst(carries), tuple(outs), tuple(carries)))
    for h in range(nh):
        o_ref[:, HEAD_DIM * h:HEAD_DIM * (h + 1)] = st[2][h].astype(BF16)


def _sb_attention(q, k_t, v):
    s, d = q.shape
    hps = 8
    ngrp = N_HEADS // hps
    w = hps * HEAD_DIM
    near = SB_NEAR + Q_TILE
    idx = jnp.arange(near)
    tri = (idx[:, None] > idx[None, :]).astype(BF16)
    return pl.pallas_call(
        _sb_kernel,
        grid=(ngrp, s // Q_TILE),
        in_specs=[pl.BlockSpec((Q_TILE, w), lambda h, i: (i, h)),
                  pl.BlockSpec((w, s), lambda h, i: (h, 0), pipeline_mode=pl.Buffered(1)),
                  pl.BlockSpec((s, w), lambda h, i: (0, h), pipeline_mode=pl.Buffered(1)),
                  pl.BlockSpec((near, near), lambda h, i: (0, 0))],
        out_specs=pl.BlockSpec((Q_TILE, w), lambda h, i: (i, h)),
        out_shape=jax.ShapeDtypeStruct((s, d), BF16),
        compiler_params=_params(("arbitrary", "arbitrary")),
        name="sb_attention",
    )(q, k_t, v, tri)


def _mlp_kernel(final, tf, o_ref, res_ref, wo_ref, g_ref, w1_ref, w2_ref, *rest):
    out_ref = rest[-1]
    h = res_ref[...] + _dot(o_ref[...], wo_ref[...])
    xn = _rms(h, g_ref[...]).astype(BF16)
    out_ref[...] = h
    for j in range(w1_ref.shape[1] // tf):
        a = jnp.maximum(_dot(xn, w1_ref[:, tf * j:tf * (j + 1)]), 0.0)
        out_ref[...] += _dot((a * a).astype(BF16), w2_ref[tf * j:tf * (j + 1), :])
    if final:
        out_ref[...] = _rms(out_ref[...], rest[0][...])


def _attn_out_mlp(o, resid, wo, gain, w1, w2, final_gain=None):
    s, d = resid.shape
    f = w1.shape[1]
    tm, tf = 512, 512
    final = final_gain is not None
    row = lambda i: (i, 0)
    resident = lambda shape: pl.BlockSpec(shape, lambda i: (0, 0), pipeline_mode=pl.Buffered(1))
    in_specs = [pl.BlockSpec((tm, d), row), pl.BlockSpec((tm, d), row), resident((d, d)), resident((1, d)),
                resident((d, f)), resident((f, d))]
    args = [o, resid, wo, gain.reshape(1, d), w1, w2]
    if final:
        in_specs.append(resident((1, d)))
        args.append(final_gain.reshape(1, d))
    return pl.pallas_call(
        functools.partial(_mlp_kernel, final, tf),
        grid=(s // tm,),
        in_specs=in_specs,
        out_specs=pl.BlockSpec((tm, d), row),
        out_shape=jax.ShapeDtypeStruct((s, d), F32),
        compiler_params=_params(("arbitrary",)),
        name="attn_out_mlp",
    )(*args)


def _row_chunks(tm):
    return [slice(r, r + PROJ_SUB) for r in range(0, tm, PROJ_SUB)]


def _kv_kernel(x_ref, g_ref, w_ref, c_ref, s1_ref, s2_ref, cmp_ref, ks_ref, vs_ref, kw_ref, vw_ref):
    gw = KV_GROUPS * HEAD_DIM
    chunks = _row_chunks(x_ref.shape[0])
    accs = [_dot(_rms(x_ref[rows, :], g_ref[...]).astype(BF16), w_ref[...]) for rows in chunks]
    ones_row = (lax.broadcasted_iota(jnp.int32, (V_ROWS - HEAD_DIM, PROJ_SUB), 0) == 0).astype(BF16)

    def put_k(ref, x, rows):
        c, s1, s2 = c_ref[rows, :], s1_ref[rows, :], s2_ref[rows, :]
        for p in range(gw // 128):
            xp = _rope128(x[:, 128 * p:128 * (p + 1)], c, s1, s2)
            ref[2 * p, rows, :] = xp[:, :HEAD_DIM].astype(BF16)
            ref[2 * p + 1, rows, :] = xp[:, HEAD_DIM:].astype(BF16)

    def put_vt(ref, x, rows):
        xt = x.T
        for g in range(KV_GROUPS):
            ref[g, :HEAD_DIM, rows] = xt[HEAD_DIM * g:HEAD_DIM * (g + 1)].astype(BF16)
            ref[g, HEAD_DIM:, rows] = ones_row

    for rows, acc in zip(chunks, accs):
        cmp_ref[rows, :] = acc[:, :2 * gw]
        put_k(ks_ref, acc[:, 2 * gw:3 * gw], rows)
        put_vt(vs_ref, acc[:, 3 * gw:4 * gw], rows)
        put_k(kw_ref, acc[:, 4 * gw:5 * gw], rows)
        put_vt(vw_ref, acc[:, 5 * gw:6 * gw], rows)


def _kv_proj(h, gain, w, rope_tabs):
    s, d = h.shape
    n = w.shape[1]
    gw = KV_GROUPS * HEAD_DIM
    tm = 2 * PROJ_SUB
    row = lambda i: (i, 0)
    fixed = lambda i: (0, 0)
    k_spec = pl.BlockSpec((KV_GROUPS, tm, HEAD_DIM), lambda i: (0, i, 0))
    k_shape = jax.ShapeDtypeStruct((KV_GROUPS, s, HEAD_DIM), BF16)
    vt_spec = pl.BlockSpec((KV_GROUPS, V_ROWS, tm), lambda i: (0, 0, i))
    vt_shape = jax.ShapeDtypeStruct((KV_GROUPS, V_ROWS, s), BF16)
    return pl.pallas_call(
        _kv_kernel,
        grid=(s // tm,),
        in_specs=[pl.BlockSpec((tm, d), row), pl.BlockSpec((1, d), fixed), pl.BlockSpec((d, n), fixed),
                  pl.BlockSpec((tm, 128), row), pl.BlockSpec((tm, 128), row), pl.BlockSpec((tm, 128), row)],
        out_specs=[pl.BlockSpec((tm, 2 * gw), row), k_spec, vt_spec, k_spec, vt_spec],
        out_shape=[jax.ShapeDtypeStruct((s, 2 * gw), F32), k_shape, vt_shape, k_shape, vt_shape],
        compiler_params=_params(("arbitrary",)),
        name="kv_proj",
    )(h, gain.reshape(1, d), w, *rope_tabs)


def _cmp_kernel(n_cmp, x_ref, pos_ref, w1_ref, w2_ref, o_ref, ot_ref):
    nch = x_ref.shape[0] // CMP_STRIDE
    half = CMP_STRIDE * HEAD_DIM
    w1 = w1_ref[...]
    bias = _dot(pos_ref[...].astype(BF16), w1)[0:1]
    acc = [[jnp.zeros((nch, CMP_HIDDEN), F32) for _ in range(2)] for _ in range(2)]
    for l in range(CMP_STRIDE):
        y = x_ref[pl.ds(l, nch, stride=CMP_STRIDE), :].astype(BF16)
        wa = w1[HEAD_DIM * l:HEAD_DIM * (l + 1)]
        wb = w1[half + HEAD_DIM * l:half + HEAD_DIM * (l + 1)]
        for gg in range(2):
            yg = y[:, HEAD_DIM * gg:HEAD_DIM * (gg + 1)]
            acc[gg][0] = acc[gg][0] + _dot(yg, wa)
            acc[gg][1] = acc[gg][1] + _dot(yg, wb)
    live = lax.broadcasted_iota(jnp.int32, (nch, HEAD_DIM), 0) < n_cmp
    ones_row = (lax.broadcasted_iota(jnp.int32, (V_ROWS - HEAD_DIM, nch), 0) == 0).astype(BF16)
    outs = []
    for gg in range(2):
        hid = acc[gg][0] + pltpu.roll(acc[gg][1], nch - 1, axis=0) + bias
        hid = jax.nn.gelu(hid, approximate=True).astype(BF16)
        outs.append(jnp.where(live, _dot(hid, w2_ref[...]), 0.0))
        o_ref[gg] = outs[gg].astype(BF16)
    out_t = jnp.concatenate(outs, axis=1).T
    for gg in range(2):
        ot_ref[gg, :HEAD_DIM, :] = out_t[HEAD_DIM * gg:HEAD_DIM * (gg + 1)].astype(BF16)
        ot_ref[gg, HEAD_DIM:, :] = ones_row


def _cmp_mlp(cmp_kv, pos_flat, w1, w2, n_cmp):
    s = cmp_kv.shape[0]
    nch = s // CMP_STRIDE
    feat = CMP_LEN * HEAD_DIM
    return pl.pallas_call(
        functools.partial(_cmp_kernel, n_cmp),
        grid=(2, KV_GROUPS // 2),
        in_specs=[pl.BlockSpec((s, 2 * HEAD_DIM), lambda c, p: (0, c * (KV_GROUPS // 2) + p)),
                  pl.BlockSpec((None, 8, feat), lambda c, p: (c, 0, 0)),
                  pl.BlockSpec((None, feat, CMP_HIDDEN), lambda c, p: (c, 0, 0)),
                  pl.BlockSpec((None, CMP_HIDDEN, HEAD_DIM), lambda c, p: (c, 0, 0))],
        out_specs=[pl.BlockSpec((None, 2, nch, HEAD_DIM), lambda c, p: (c, p, 0, 0)),
                   pl.BlockSpec((None, 2, V_ROWS, nch), lambda c, p: (c, p, 0, 0))],
        out_shape=[jax.ShapeDtypeStruct((2, KV_GROUPS, nch, HEAD_DIM), BF16),
                   jax.ShapeDtypeStruct((2, KV_GROUPS, V_ROWS, nch), BF16)],
        compiler_params=_params(("arbitrary", "arbitrary")),
        name="cmp_mlp",
    )(cmp_kv, pos_flat, w1, w2)


def _q_kernel(x_ref, g_ref, w_ref, b_ref, c_ref, s1_ref, s2_ref, q_ref, qr_ref, gate_ref):
    d = q_ref.shape[1]
    chunks = _row_chunks(x_ref.shape[0])
    accs = [_dot(_rms(x_ref[rows, :], g_ref[...]).astype(BF16), w_ref[...]) for rows in chunks]
    for rows, acc in zip(chunks, accs):
        c, s1, s2 = c_ref[rows, :], s1_ref[rows, :], s2_ref[rows, :]
        for p in range(d // 128):
            qp = acc[:, 128 * p:128 * (p + 1)] * (HEAD_DIM ** -0.5 * LOG2E)
            q_ref[rows, 128 * p:128 * (p + 1)] = qp.astype(BF16)
            qr_ref[rows, 128 * p:128 * (p + 1)] = _rope128(qp, c, s1, s2).astype(BF16)
        gates_t = (1.0 / (1.0 + jnp.exp(-(acc[:, d:] + b_ref[...])))).T
        for g in range(KV_GROUPS):
            gate_ref[g, :, rows] = gates_t[GATE_ROWS * g:GATE_ROWS * (g + 1)]


def _q_proj(h, gain, w, gate_b, rope_tabs):
    s, d = h.shape
    n = w.shape[1]
    tm = 2 * PROJ_SUB
    row = lambda i: (i, 0)
    fixed = lambda i: (0, 0)
    return pl.pallas_call(
        _q_kernel,
        grid=(s // tm,),
        in_specs=[pl.BlockSpec((tm, d), row), pl.BlockSpec((1, d), fixed), pl.BlockSpec((d, n), fixed),
                  pl.BlockSpec((1, 128), fixed),
                  pl.BlockSpec((tm, 128), row), pl.BlockSpec((tm, 128), row), pl.BlockSpec((tm, 128), row)],
        out_specs=[pl.BlockSpec((tm, d), row), pl.BlockSpec((tm, d), row),
                   pl.BlockSpec((KV_GROUPS, GATE_ROWS, tm), lambda i: (0, 0, i))],
        out_shape=[jax.ShapeDtypeStruct((s, d), BF16), jax.ShapeDtypeStruct((s, d), BF16),
                   jax.ShapeDtypeStruct((KV_GROUPS, GATE_ROWS, s), F32)],
        compiler_params=_params(("arbitrary",)),
        name="nsa_q_proj",
    )(h, gain.reshape(1, d), w, gate_b, *rope_tabs)


def _group_gate_columns(a):
    lead = a.shape[:-1]
    a = a.reshape(lead + (3, KV_GROUPS, HEADS_PER_GROUP))
    a = jnp.moveaxis(a, -2, -3).reshape(lead + (KV_GROUPS, 3 * HEADS_PER_GROUP))
    a = jnp.pad(a, [(0, 0)] * len(lead) + [(0, 0), (0, GATE_ROWS - 3 * HEADS_PER_GROUP)])
    a = a.reshape(lead + (KV_GROUPS * GATE_ROWS,))
    return jnp.pad(a, [(0, 0)] * len(lead) + [(0, 128 - KV_GROUPS * GATE_ROWS)])


def _stack_heads_t(q):
    qt = q.astype(F32).T
    return jnp.concatenate([qt[HEAD_DIM * h:HEAD_DIM * (h + 1)] for h in range(HEADS_PER_GROUP)], axis=1).astype(BF16)


def _head_cols(x, h):
    return x[:, Q_TILE * h:Q_TILE * (h + 1)]


def _softmax_numerators_t(st, mask):
    es = []
    for h in range(HEADS_PER_GROUP):
        s = mask(_head_cols(st, h))
        es.append(jnp.exp2(s - jnp.max(s, axis=0, keepdims=True)))
    return es


def _nsa_kernel(tk, q_ref, qr_ref, kc_ref, vct_ref, ovt_ref, ks_ref, vst_ref, kw_ref, vwt_ref, wbias_ref, gate_ref,
                o_ref, ocmp_ref, owin_ref, bias_ref, s0_ref, s1_ref, mx0_ref, mx1_ref, m_ref, acc_ref):
    i = pl.program_id(1)
    tq = q_ref.shape[0]
    nc = kc_ref.shape[0]
    n_sel = ovt_ref.shape[0]
    hp = HEADS_PER_GROUP
    q4 = _stack_heads_t(q_ref[...])
    qr4 = _stack_heads_t(qr_ref[...])
    tok = i * tq + lax.broadcasted_iota(jnp.int32, (1, tq), 1)
    gates = gate_ref[...]

    q0 = pl.multiple_of(i * tq, tq)
    span = WINDOW + tq
    wstart = pl.multiple_of(jnp.maximum(i * tq - WINDOW, 0), tq)

    def cmp_and_select(ncv, rows):
        sc = _dot(kc_ref[:ncv, :], q4)
        sw = _dot(kw_ref[pl.ds(wstart, span), :], qr4)
        sd = _dot(ks_ref[pl.ds(q0, tq), :], qr4)
        s0_ref[...] = _dot(ks_ref[:tk, :], qr4)

        n_idx = lax.broadcasted_iota(jnp.int32, (ncv, tq), 0)
        last_complete = lax.shift_right_arithmetic(tok - (CMP_LEN - 1), CMP_STRIDE.bit_length() - 1)
        es = _softmax_numerators_t(sc, lambda s: jnp.where(n_idx <= last_complete, s, NEG))
        wbias = wbias_ref[...]
        ew = _softmax_numerators_t(sw, lambda s: s + wbias)
        not_future = lax.broadcasted_iota(jnp.int32, (tq, tq), 0) <= lax.broadcasted_iota(jnp.int32, (tq, tq), 1)
        sd = jnp.concatenate([jnp.where(not_future, _head_cols(sd, h), NEG) for h in range(hp)], axis=1)
        m0 = jnp.max(sd, axis=0, keepdims=True)
        pd = jnp.exp2(sd - m0)

        o_cmp = _dot(vct_ref[:, :ncv], jnp.concatenate(es, axis=1).astype(BF16))
        o_win = _dot(vwt_ref[:, pl.ds(wstart, span)], jnp.concatenate(ew, axis=1).astype(BF16))
        m_ref[...] = m0
        acc_ref[...] = _dot(vst_ref[:, pl.ds(q0, tq)], pd.astype(BF16))
        inv_c = jnp.where(jnp.concatenate([tok >= CMP_LEN - 1] * hp, axis=1), 1.0 / o_cmp[HEAD_DIM:HEAD_DIM + 1], 0.0)
        ocmp_ref[...] = o_cmp[:HEAD_DIM] * inv_c
        owin_ref[...] = o_win[:HEAD_DIM] * (1.0 / o_win[HEAD_DIM:HEAD_DIM + 1])

        psum = sum(es[h] * _head_cols(inv_c, h) for h in range(hp))
        ovt = ovt_ref[:rows, :ncv]
        imp = sum(_dot(ovt, term) for term in _split3(psum))

        m_idx = lax.broadcasted_iota(jnp.int32, (rows, tq), 0)
        blk_t = tok // SEL_LEN
        forced = (m_idx == 0) | (m_idx == blk_t) | (m_idx == blk_t - 1)
        candidate = jnp.logical_and(m_idx <= blk_t, jnp.logical_not(forced))
        imp = jnp.where(candidate, imp, -jnp.inf)

        def pick(_, v):
            mx = jnp.max(v, axis=0, keepdims=True)
            idx = jnp.min(jnp.where(v == mx, m_idx, rows), axis=0, keepdims=True)
            return jnp.where(m_idx == idx, -jnp.inf, v)

        picked = lax.fori_loop(0, SEL_TOPK - 3, pick, imp, unroll=True) == -jnp.inf
        before = m_idx < (i * tq) // SEL_LEN
        bias_ref[:rows, :] = jnp.where(jnp.logical_and(picked, before), 0.0, NEG)
        if rows < n_sel:
            bias_ref[rows:, :] = jnp.full((n_sel - rows, tq), NEG, F32)

    rows_per_chunk = CMP_CHUNK * CMP_STRIDE // SEL_LEN
    n_prefix = -(-nc // CMP_CHUNK)
    need = ((i + 1) * tq - 1) // (CMP_STRIDE * CMP_CHUNK) + 1
    for c in range(1, n_prefix + 1):
        pl.when(need == c)(functools.partial(cmp_and_select, min(c * CMP_CHUNK, nc), min(c * rows_per_chunk, n_sel)))
    bpt = tk // SEL_LEN

    def sel_bias_and_max(kt, st, s_ref, mx_ref):
        rows = bias_ref[pl.ds(pl.multiple_of(kt * bpt, bpt), bpt), :]
        bias = jnp.concatenate([jnp.broadcast_to(rows[j:j + 1], (SEL_LEN, tq)) for j in range(bpt)], axis=0)
        s = jnp.concatenate([_head_cols(st, h) + bias for h in range(hp)], axis=1)
        s_ref[...] = s
        mx_ref[...] = jnp.max(s, axis=0, keepdims=True)

    def sel_scores(kt, s_ref, mx_ref):
        start = pl.multiple_of(kt * tk, tk)
        sel_bias_and_max(kt, _dot(ks_ref[pl.ds(start, tk), :], qr4), s_ref, mx_ref)

    def sel_accumulate(kt, s_ref, mx_ref):
        start = pl.multiple_of(kt * tk, tk)
        m = m_ref[...]
        m_new = jnp.maximum(m, mx_ref[...])
        p = jnp.exp2(s_ref[...] - m_new)
        acc_ref[...] = jnp.exp2(m - m_new) * acc_ref[...] + _dot(vst_ref[:, pl.ds(start, tk)], p.astype(BF16))
        m_ref[...] = m_new

    last = jnp.maximum(i * tq - 1, 0) // tk
    final_tile = ks_ref.shape[0] // tk - 1

    def sel_pair(j):
        sel_scores(2 * j + 1, s1_ref, mx1_ref)
        sel_accumulate(2 * j, s0_ref, mx0_ref)
        sel_scores(jnp.minimum(2 * j + 2, final_tile), s0_ref, mx0_ref)
        sel_accumulate(2 * j + 1, s1_ref, mx1_ref)

    def sel_pairs(jj, carry):
        for u in range(SEL_UNROLL):
            sel_pair(SEL_UNROLL * jj + u)
        return carry

    sel_bias_and_max(0, s0_ref[...], s0_ref, mx0_ref)
    pairs = (last + 1) // 2
    lax.fori_loop(0, pairs // SEL_UNROLL, sel_pairs, 0)
    done = pairs // SEL_UNROLL * SEL_UNROLL
    run = SEL_UNROLL // 2
    while run >= 1:
        def sel_run(done=done, run=run):
            for u in range(run):
                sel_pair(done + u)

        take = (pairs - done) >= run
        pl.when(take)(sel_run)
        done = done + jnp.where(take, run, 0)
        run //= 2

    @pl.when(last % 2 == 0)
    def _():
        sel_accumulate(last, s0_ref, mx0_ref)

    acc = acc_ref[...]
    o_sel = acc[:HEAD_DIM] * (1.0 / acc[HEAD_DIM:HEAD_DIM + 1])

    o_cmp = ocmp_ref[...]
    o_win = owin_ref[...]
    for h in range(hp):
        mix = (gates[h:h + 1] * _head_cols(o_cmp, h)
               + gates[hp + h:hp + h + 1] * _head_cols(o_sel, h)
               + gates[2 * hp + h:2 * hp + h + 1] * _head_cols(o_win, h))
        o_ref[:, HEAD_DIM * h:HEAD_DIM * (h + 1)] = mix.T.astype(BF16)


def _nsa_attention(q, q_rot, k_cmp, v_cmp_t, overlap_t, k_slc, v_slc_t, k_win, v_win_t, gates):
    s, d = q.shape
    nc = k_cmp.shape[1]
    n_sel = overlap_t.shape[0]
    gw = HEADS_PER_GROUP * HEAD_DIM
    cols = HEADS_PER_GROUP * Q_TILE
    tk = 512
    n_wcase = WINDOW // Q_TILE + 1
    lead = (jnp.arange(n_wcase) * Q_TILE)[:, None, None]
    dist = lead + jnp.arange(Q_TILE)[None, None, :] - jnp.arange(WINDOW + Q_TILE)[None, :, None]
    wbias = jnp.where((dist >= 0) & (dist < WINDOW), 0.0, NEG).astype(F32)
    qspec = pl.BlockSpec((Q_TILE, gw), lambda g, i: (i, g))
    per_group = lambda shape: pl.BlockSpec((None,) + shape, lambda g, i: (g, 0, 0))
    return pl.pallas_call(
        functools.partial(_nsa_kernel, tk),
        grid=(KV_GROUPS, s // Q_TILE),
        in_specs=[qspec, qspec,
                  per_group((nc, HEAD_DIM)), per_group((V_ROWS, nc)),
                  pl.BlockSpec((n_sel, nc), lambda g, i: (0, 0)),
                  per_group((s, HEAD_DIM)), per_group((V_ROWS, s)),
                  per_group((s, HEAD_DIM)), per_group((V_ROWS, s)),
                  pl.BlockSpec((None, WINDOW + Q_TILE, Q_TILE), lambda g, i: (jnp.minimum(i, n_wcase - 1), 0, 0)),
                  pl.BlockSpec((None, GATE_ROWS, Q_TILE), lambda g, i: (g, 0, i))],
        out_specs=qspec,
        out_shape=jax.ShapeDtypeStruct((s, d), BF16),
        scratch_shapes=[pltpu.VMEM((HEAD_DIM, cols), F32), pltpu.VMEM((HEAD_DIM, cols), F32),
                        pltpu.VMEM((n_sel, Q_TILE), F32),
                        pltpu.VMEM((tk, cols), F32), pltpu.VMEM((tk, cols), F32),
                        pltpu.VMEM((1, cols), F32), pltpu.VMEM((1, cols), F32),
                        pltpu.VMEM((1, cols), F32), pltpu.VMEM((V_ROWS, cols), F32)],
        compiler_params=_params(("arbitrary", "arbitrary")),
        name="nsa_attention",
    )(q, q_rot, k_cmp, v_cmp_t, overlap_t, k_slc, v_slc_t, k_win, v_win_t, wbias, gates)


def _rope_tables(s):
    half = ROT_DIM // 2
    inv_freq = ROPE_THETA ** (-jnp.arange(half, dtype=F32) * 2.0 / ROT_DIM)
    ang = jnp.arange(s, dtype=F32)[:, None] * inv_freq[None, :]
    cos, sin = jnp.cos(ang), jnp.sin(ang)
    rest = HEAD_DIM - ROT_DIM
    c = jnp.concatenate([cos, cos, jnp.ones((s, rest), F32)], axis=1)
    s1 = jnp.concatenate([-sin, jnp.zeros((s, half + rest), F32)], axis=1)
    s2 = jnp.concatenate([jnp.zeros((s, half), F32), sin, jnp.zeros((s, rest), F32)], axis=1)
    return tuple(jnp.tile(a, (1, 2)) for a in (c, s1, s2))


def _overlap_matrix(nch, n_sel):
    cmp_start = jnp.arange(nch)[:, None] * CMP_STRIDE
    sel_start = jnp.arange(n_sel)[None, :] * SEL_LEN
    return ((cmp_start < sel_start + SEL_LEN) & (cmp_start + CMP_LEN > sel_start)).astype(BF16)


def kernel(x, norm_gain, sb_w_qkv, sb_w_o, kv_norm, nsa_w_kv, cmp_pos, cmp_w1, cmp_w2,
           nsa_w_q, nsa_gate_b, nsa_w_o, mlp_w1, mlp_w2, final_norm):
    b, s, d = x.shape
    assert b == 1 and d == N_HEADS * HEAD_DIM
    assert s % 512 == 0 and s >= WINDOW + Q_TILE and s // SEL_LEN >= SEL_TOPK
    n_cmp = (s - CMP_LEN) // CMP_STRIDE + 1
    nch = s // CMP_STRIDE
    n_sel = s // SEL_LEN
    h0 = x[0]
    rope_tabs = _rope_tables(s)

    o_sb = _sb_attention(*_qkv_proj(h0, norm_gain[0, 0], sb_w_qkv[0].astype(BF16)))
    h1 = _attn_out_mlp(o_sb, h0, sb_w_o[0].astype(BF16), norm_gain[0, 1],
                       mlp_w1[0].astype(BF16), mlp_w2[0].astype(BF16))

    cmp_kv, k_slc, v_slc_t, k_win, v_win_t = _kv_proj(h1, kv_norm, nsa_w_kv.astype(BF16), rope_tabs)
    pos_flat = jnp.broadcast_to(cmp_pos.reshape(2, 1, CMP_LEN * HEAD_DIM), (2, 8, CMP_LEN * HEAD_DIM))
    kv_cmp, kv_cmp_t = _cmp_mlp(cmp_kv, pos_flat, cmp_w1.astype(BF16), cmp_w2.astype(BF16), n_cmp)

    n_qk = N_HEADS * HEAD_DIM
    w_q = jnp.concatenate([nsa_w_q[0][:, :n_qk], _group_gate_columns(nsa_w_q[0][:, n_qk:])], axis=1).astype(BF16)
    gate_b = _group_gate_columns(nsa_gate_b[0]).reshape(1, 128)
    q, q_rot, gates = _q_proj(h1, norm_gain[1, 0], w_q, gate_b, rope_tabs)
    o_nsa = _nsa_attention(q, q_rot, kv_cmp[0], kv_cmp_t[1], _overlap_matrix(nch, n_sel).T,
                           k_slc, v_slc_t, k_win, v_win_t, gates)
    out = _attn_out_mlp(o_nsa, h1, nsa_w_o[0].astype(BF16), norm_gain[1, 1],
                        mlp_w1[1].astype(BF16), mlp_w2[1].astype(BF16), final_gain=final_norm)
    return out[None]
```

```python
import functools

import jax
import jax.numpy as jnp
from jax import lax
from jax.experimental import pallas as pl
from jax.experimental.pallas import tpu as pltpu

HEAD_DIM = 64
N_HEADS = 16
KV_GROUPS = 4
HEADS_PER_GROUP = N_HEADS // KV_GROUPS
CMP_LEN = 32
CMP_STRIDE = 16
CMP_HIDDEN = 256
SEL_LEN = 64
SEL_TOPK = 16
WINDOW = 512
ROPE_THETA = 500000.0
ROT_DIM = HEAD_DIM // 4
Q_TILE = 128
GATE_ROWS = 16
V_ROWS = HEAD_DIM + 16
LOG2E = 1.4426950408889634
CMP_CHUNK = 256
SEL_UNROLL = 4
PROJ_SUB = 256
NORM_EPS = 1e-5
NEG = -1e30
SB_UNDERFLOW = -110.0
SB_NEAR = 256
VMEM_LIMIT = 56 * 1024 * 1024

BF16 = jnp.bfloat16
F32 = jnp.float32


def _params(semantics, vmem=VMEM_LIMIT):
    return pltpu.CompilerParams(dimension_semantics=semantics, vmem_limit_bytes=vmem)


def _rms(x, g):
    return x * lax.rsqrt(jnp.mean(x * x, axis=-1, keepdims=True) + NORM_EPS) * g


def _dot(a, b):
    return jnp.dot(a, b, preferred_element_type=F32)


def _rope128(x, c, s1, s2):
    return x * c + pltpu.roll(x, 128 - ROT_DIM // 2, axis=1) * s1 + pltpu.roll(x, ROT_DIM // 2, axis=1) * s2


def _split3(x):
    hi = x.astype(BF16)
    r = x - hi.astype(F32)
    mid = r.astype(BF16)
    lo = (r - mid.astype(F32)).astype(BF16)
    return hi, mid, lo


def _qkv_kernel(x_ref, g_ref, w_ref, q_ref, kt_ref, v_ref):
    xn = _rms(x_ref[...], g_ref[...]).astype(BF16)
    acc = _dot(xn, w_ref[...])
    d = x_ref.shape[1]
    q_ref[...] = (acc[:, :d] * (HEAD_DIM ** -0.5)).astype(BF16)
    kt_ref[...] = acc[:, d:2 * d].T.astype(BF16)
    v_ref[...] = acc[:, 2 * d:].astype(BF16)


def _qkv_proj(x, gain, w):
    s, d = x.shape
    n = w.shape[1]
    tm = 256
    row = pl.BlockSpec((tm, d), lambda i: (i, 0))
    return pl.pallas_call(
        _qkv_kernel,
        grid=(s // tm,),
        in_specs=[row, pl.BlockSpec((1, d), lambda i: (0, 0)), pl.BlockSpec((d, n), lambda i: (0, 0))],
        out_specs=[row, pl.BlockSpec((d, tm), lambda i: (0, i)), row],
        out_shape=[jax.ShapeDtypeStruct((s, d), BF16), jax.ShapeDtypeStruct((d, s), BF16),
                   jax.ShapeDtypeStruct((s, d), BF16)],
        compiler_params=_params(("arbitrary",)),
        name="qkv_proj",
    )(x, gain.reshape(1, d), w)


def _sb_blocks(qs, kts, vs, tri, carries, mask):
    n = len(qs)
    zs = [_dot(qs[h], kts[h]) for h in range(n)]
    sps = [jnp.maximum(z, 0.0) + jnp.log(1.0 + jnp.exp(-jnp.abs(z))) for z in zs]
    logs = [-sp for sp in sps]
    if mask is not None:
        logs = [jnp.where(mask, x, 0.0) for x in logs]
    excls = [_dot(logs[h].astype(BF16), tri) for h in range(n)]
    ws = [jnp.exp((zs[h] - sps[h]) + (excls[h] + carries[h])) for h in range(n)]
    if mask is not None:
        ws = [jnp.where(mask, w, 0.0) for w in ws]
    outs = [_dot(ws[h].astype(BF16), vs[h]) for h in range(n)]
    new_carries = [carries[h] + jnp.sum(logs[h], axis=1, keepdims=True) for h in range(n)]
    return outs, new_carries


def _sb_kernel(q_ref, kt_ref, v_ref, tri_ref, o_ref):
    i = pl.program_id(1)
    tq = q_ref.shape[0]
    nh = q_ref.shape[1] // HEAD_DIM
    near = tri_ref.shape[0]
    tri_near = tri_ref[...]
    tri = tri_near[:tq, :tq]
    q = q_ref[...]
    head = lambda x, h: x[:, HEAD_DIM * h:HEAD_DIM * (h + 1)]

    start = pl.multiple_of(jnp.maximum(i * tq - (near - tq), 0), tq)
    row = lax.broadcasted_iota(jnp.int32, (tq, near), 0)
    col = lax.broadcasted_iota(jnp.int32, (tq, near), 1)
    causal = col - row < i * tq - start
    kt = kt_ref[:, pl.ds(start, near)]
    v = v_ref[pl.ds(start, near), :]
    qs = [head(q, h) for h in range(nh)]
    split = lambda x: [head(x, h) for h in range(nh)]
    split_t = lambda x: [x[HEAD_DIM * h:HEAD_DIM * (h + 1)] for h in range(nh)]
    outs, carries = _sb_blocks(qs, split_t(kt), split(v), tri_near, [jnp.zeros((tq, 1), F32)] * nh, causal)

    def worst(cs):
        mx = jnp.max(cs[0])
        for c in cs[1:]:
            mx = jnp.maximum(mx, jnp.max(c))
        return mx

    def cond(st):
        return jnp.logical_and(st[0] >= 0, st[1] > SB_UNDERFLOW)

    def body(st):
        kb, _, outs, carries = st
        start = pl.multiple_of(kb * tq, tq)
        kt = kt_ref[:, pl.ds(start, tq)]
        v = v_ref[pl.ds(start, tq), :]
        more, new_c = _sb_blocks(qs, split_t(kt), split(v), tri, list(carries), None)
        return kb - 1, worst(new_c), tuple(outs[h] + more[h] for h in range(nh)), tuple(new_c)

    st = lax.while_loop(cond, body, (start // tq - 1, worst(carries), tuple(outs), tuple(carries)))
    for h in range(nh):
        o_ref[:, HEAD_DIM * h:HEAD_DIM * (h + 1)] = st[2][h].astype(BF16)


def _sb_attention(q, k_t, v):
    s, d = q.shape
    hps = 8
    ngrp = N_HEADS // hps
    w = hps * HEAD_DIM
    near = SB_NEAR + Q_TILE
    idx = jnp.arange(near)
    tri = (idx[:, None] > idx[None, :]).astype(BF16)
    return pl.pallas_call(
        _sb_kernel,
        grid=(ngrp, s // Q_TILE),
        in_specs=[pl.BlockSpec((Q_TILE, w), lambda h, i: (i, h)),
                  pl.BlockSpec((w, s), lambda h, i: (h, 0), pipeline_mode=pl.Buffered(1)),
                  pl.BlockSpec((s, w), lambda h, i: (0, h), pipeline_mode=pl.Buffered(1)),
                  pl.BlockSpec((near, near), lambda h, i: (0, 0))],
        out_specs=pl.BlockSpec((Q_TILE, w), lambda h, i: (i, h)),
        out_shape=jax.ShapeDtypeStruct((s, d), BF16),
        compiler_params=_params(("arbitrary", "arbitrary")),
        name="sb_attention",
    )(q, k_t, v, tri)


def _mlp_kernel(final, tf, o_ref, res_ref, wo_ref, g_ref, w1_ref, w2_ref, *rest):
    out_ref = rest[-1]
    h = res_ref[...] + _dot(o_ref[...], wo_ref[...])
    xn = _rms(h, g_ref[...]).astype(BF16)
    out_ref[...] = h
    for j in range(w1_ref.shape[1] // tf):
        a = jnp.maximum(_dot(xn, w1_ref[:, tf * j:tf * (j + 1)]), 0.0)
        out_ref[...] += _dot((a * a).astype(BF16), w2_ref[tf * j:tf * (j + 1), :])
    if final:
        out_ref[...] = _rms(out_ref[...], rest[0][...])


def _attn_out_mlp(o, resid, wo, gain, w1, w2, final_gain=None):
    s, d = resid.shape
    f = w1.shape[1]
    tm, tf = 512, 512
    final = final_gain is not None
    row = lambda i: (i, 0)
    resident = lambda shape: pl.BlockSpec(shape, lambda i: (0, 0), pipeline_mode=pl.Buffered(1))
    in_specs = [pl.BlockSpec((tm, d), row), pl.BlockSpec((tm, d), row), resident((d, d)), resident((1, d)),
                resident((d, f)), resident((f, d))]
    args = [o, resid, wo, gain.reshape(1, d), w1, w2]
    if final:
        in_specs.append(resident((1, d)))
        args.append(final_gain.reshape(1, d))
    return pl.pallas_call(
        functools.partial(_mlp_kernel, final, tf),
        grid=(s // tm,),
        in_specs=in_specs,
        out_specs=pl.BlockSpec((tm, d), row),
        out_shape=jax.ShapeDtypeStruct((s, d), F32),
        compiler_params=_params(("arbitrary",)),
        name="attn_out_mlp",
    )(*args)


def _row_chunks(tm):
    return [slice(r, r + PROJ_SUB) for r in range(0, tm, PROJ_SUB)]


def _kv_kernel(x_ref, g_ref, w_ref, c_ref, s1_ref, s2_ref, cmp_ref, ks_ref, vs_ref, kw_ref, vw_ref):
    gw = KV_GROUPS * HEAD_DIM
    chunks = _row_chunks(x_ref.shape[0])
    accs = [_dot(_rms(x_ref[rows, :], g_ref[...]).astype(BF16), w_ref[...]) for rows in chunks]
    ones_row = (lax.broadcasted_iota(jnp.int32, (V_ROWS - HEAD_DIM, PROJ_SUB), 0) == 0).astype(BF16)

    def put_k(ref, x, rows):
        c, s1, s2 = c_ref[rows, :], s1_ref[rows, :], s2_ref[rows, :]
        for p in range(gw // 128):
            xp = _rope128(x[:, 128 * p:128 * (p + 1)], c, s1, s2)
            ref[2 * p, rows, :] = xp[:, :HEAD_DIM].astype(BF16)
            ref[2 * p + 1, rows, :] = xp[:, HEAD_DIM:].astype(BF16)

    def put_vt(ref, x, rows):
        xt = x.T
        for g in range(KV_GROUPS):
            ref[g, :HEAD_DIM, rows] = xt[HEAD_DIM * g:HEAD_DIM * (g + 1)].astype(BF16)
            ref[g, HEAD_DIM:, rows] = ones_row

    for rows, acc in zip(chunks, accs):
        cmp_ref[rows, :] = acc[:, :2 * gw]
        put_k(ks_ref, acc[:, 2 * gw:3 * gw], rows)
        put_vt(vs_ref, acc[:, 3 * gw:4 * gw], rows)
        put_k(kw_ref, acc[:, 4 * gw:5 * gw], rows)
        put_vt(vw_ref, acc[:, 5 * gw:6 * gw], rows)


def _kv_proj(h, gain, w, rope_tabs):
    s, d = h.shape
    n = w.shape[1]
    gw = KV_GROUPS * HEAD_DIM
    tm = 2 * PROJ_SUB
    row = lambda i: (i, 0)
    fixed = lambda i: (0, 0)
    k_spec = pl.BlockSpec((KV_GROUPS, tm, HEAD_DIM), lambda i: (0, i, 0))
    k_shape = jax.ShapeDtypeStruct((KV_GROUPS, s, HEAD_DIM), BF16)
    vt_spec = pl.BlockSpec((KV_GROUPS, V_ROWS, tm), lambda i: (0, 0, i))
    vt_shape = jax.ShapeDtypeStruct((KV_GROUPS, V_ROWS, s), BF16)
    return pl.pallas_call(
        _kv_kernel,
        grid=(s // tm,),
        in_specs=[pl.BlockSpec((tm, d), row), pl.BlockSpec((1, d), fixed), pl.BlockSpec((d, n), fixed),
                  pl.BlockSpec((tm, 128), row), pl.BlockSpec((tm, 128), row), pl.BlockSpec((tm, 128), row)],
        out_specs=[pl.BlockSpec((tm, 2 * gw), row), k_spec, vt_spec, k_spec, vt_spec],
        out_shape=[jax.ShapeDtypeStruct((s, 2 * gw), F32), k_shape, vt_shape, k_shape, vt_shape],
        compiler_params=_params(("arbitrary",)),
        name="kv_proj",
    )(h, gain.reshape(1, d), w, *rope_tabs)


def _cmp_kernel(n_cmp, x_ref, pos_ref, w1_ref, w2_ref, o_ref, ot_ref):
    nch = x_ref.shape[0] // CMP_STRIDE
    half = CMP_STRIDE * HEAD_DIM
    w1 = w1_ref[...]
    bias = _dot(pos_ref[...].astype(BF16), w1)[0:1]
    acc = [[jnp.zeros((nch, CMP_HIDDEN), F32) for _ in range(2)] for _ in range(2)]
    for l in range(CMP_STRIDE):
        y = x_ref[pl.ds(l, nch, stride=CMP_STRIDE), :].astype(BF16)
        wa = w1[HEAD_DIM * l:HEAD_DIM * (l + 1)]
        wb = w1[half + HEAD_DIM * l:half + HEAD_DIM * (l + 1)]
        for gg in range(2):
            yg = y[:, HEAD_DIM * gg:HEAD_DIM * (gg + 1)]
            acc[gg][0] = acc[gg][0] + _dot(yg, wa)
            acc[gg][1] = acc[gg][1] + _dot(yg, wb)
    live = lax.broadcasted_iota(jnp.int32, (nch, HEAD_DIM), 0) < n_cmp
    ones_row = (lax.broadcasted_iota(jnp.int32, (V_ROWS - HEAD_DIM, nch), 0) == 0).astype(BF16)
    outs = []
    for gg in range(2):
        hid = acc[gg][0] + pltpu.roll(acc[gg][1], nch - 1, axis=0) + bias
        hid = jax.nn.gelu(hid, approximate=True).astype(BF16)
        outs.append(jnp.where(live, _dot(hid, w2_ref[...]), 0.0))
        o_ref[gg] = outs[gg].astype(BF16)
    out_t = jnp.concatenate(outs, axis=1).T
    for gg in range(2):
        ot_ref[gg, :HEAD_DIM, :] = out_t[HEAD_DIM * gg:HEAD_DIM * (gg + 1)].astype(BF16)
        ot_ref[gg, HEAD_DIM:, :] = ones_row


def _cmp_mlp(cmp_kv, pos_flat, w1, w2, n_cmp):
    s = cmp_kv.shape[0]
    nch = s // CMP_STRIDE
    feat = CMP_LEN * HEAD_DIM
    return pl.pallas_call(
        functools.partial(_cmp_kernel, n_cmp),
        grid=(2, KV_GROUPS // 2),
        in_specs=[pl.BlockSpec((s, 2 * HEAD_DIM), lambda c, p: (0, c * (KV_GROUPS // 2) + p)),
                  pl.BlockSpec((None, 8, feat), lambda c, p: (c, 0, 0)),
                  pl.BlockSpec((None, feat, CMP_HIDDEN), lambda c, p: (c, 0, 0)),
                  pl.BlockSpec((None, CMP_HIDDEN, HEAD_DIM), lambda c, p: (c, 0, 0))],
        out_specs=[pl.BlockSpec((None, 2, nch, HEAD_DIM), lambda c, p: (c, p, 0, 0)),
                   pl.BlockSpec((None, 2, V_ROWS, nch), lambda c, p: (c, p, 0, 0))],
        out_shape=[jax.ShapeDtypeStruct((2, KV_GROUPS, nch, HEAD_DIM), BF16),
                   jax.ShapeDtypeStruct((2, KV_GROUPS, V_ROWS, nch), BF16)],
        compiler_params=_params(("arbitrary", "arbitrary")),
        name="cmp_mlp",
    )(cmp_kv, pos_flat, w1, w2)


def _q_kernel(x_ref, g_ref, w_ref, b_ref, c_ref, s1_ref, s2_ref, q_ref, qr_ref, gate_ref):
    d = q_ref.shape[1]
    chunks = _row_chunks(x_ref.shape[0])
    accs = [_dot(_rms(x_ref[rows, :], g_ref[...]).astype(BF16), w_ref[...]) for rows in chunks]
    for rows, acc in zip(chunks, accs):
        c, s1, s2 = c_ref[rows, :], s1_ref[rows, :], s2_ref[rows, :]
        for p in range(d // 128):
            qp = acc[:, 128 * p:128 * (p + 1)] * (HEAD_DIM ** -0.5 * LOG2E)
            q_ref[rows, 128 * p:128 * (p + 1)] = qp.astype(BF16)
            qr_ref[rows, 128 * p:128 * (p + 1)] = _rope128(qp, c, s1, s2).astype(BF16)
        gates_t = (1.0 / (1.0 + jnp.exp(-(acc[:, d:] + b_ref[...])))).T
        for g in range(KV_GROUPS):
            gate_ref[g, :, rows] = gates_t[GATE_ROWS * g:GATE_ROWS * (g + 1)]


def _q_proj(h, gain, w, gate_b, rope_tabs):
    s, d = h.shape
    n = w.shape[1]
    tm = 2 * PROJ_SUB
    row = lambda i: (i, 0)
    fixed = lambda i: (0, 0)
    return pl.pallas_call(
        _q_kernel,
        grid=(s // tm,),
        in_specs=[pl.BlockSpec((tm, d), row), pl.BlockSpec((1, d), fixed), pl.BlockSpec((d, n), fixed),
                  pl.BlockSpec((1, 128), fixed),
                  pl.BlockSpec((tm, 128), row), pl.BlockSpec((tm, 128), row), pl.BlockSpec((tm, 128), row)],
        out_specs=[pl.BlockSpec((tm, d), row), pl.BlockSpec((tm, d), row),
                   pl.BlockSpec((KV_GROUPS, GATE_ROWS, tm), lambda i: (0, 0, i))],
        out_shape=[jax.ShapeDtypeStruct((s, d), BF16), jax.ShapeDtypeStruct((s, d), BF16),
                   jax.ShapeDtypeStruct((KV_GROUPS, GATE_ROWS, s), F32)],
        compiler_params=_params(("arbitrary",)),
        name="nsa_q_proj",
    )(h, gain.reshape(1, d), w, gate_b, *rope_tabs)


def _group_gate_columns(a):
    lead = a.shape[:-1]
    a = a.reshape(lead + (3, KV_GROUPS, HEADS_PER_GROUP))
    a = jnp.moveaxis(a, -2, -3).reshape(lead + (KV_GROUPS, 3 * HEADS_PER_GROUP))
    a = jnp.pad(a, [(0, 0)] * len(lead) + [(0, 0), (0, GATE_ROWS - 3 * HEADS_PER_GROUP)])
    a = a.reshape(lead + (KV_GROUPS * GATE_ROWS,))
    return jnp.pad(a, [(0, 0)] * len(lead) + [(0, 128 - KV_GROUPS * GATE_ROWS)])


def _stack_heads_t(q):
    qt = q.astype(F32).T
    return jnp.concatenate([qt[HEAD_DIM * h:HEAD_DIM * (h + 1)] for h in range(HEADS_PER_GROUP)], axis=1).astype(BF16)


def _head_cols(x, h):
    return x[:, Q_TILE * h:Q_TILE * (h + 1)]


def _softmax_numerators_t(st, mask):
    es = []
    for h in range(HEADS_PER_GROUP):
        s = mask(_head_cols(st, h))
        es.append(jnp.exp2(s - jnp.max(s, axis=0, keepdims=True)))
    return es


def _nsa_kernel(tk, q_ref, qr_ref, kc_ref, vct_ref, ovt_ref, ks_ref, vst_ref, kw_ref, vwt_ref, wbias_ref, gate_ref,
                o_ref, ocmp_ref, owin_ref, bias_ref, s0_ref, s1_ref, mx0_ref, mx1_ref, m_ref, acc_ref):
    i = pl.program_id(1)
    tq = q_ref.shape[0]
    nc = kc_ref.shape[0]
    n_sel = ovt_ref.shape[0]
    hp = HEADS_PER_GROUP
    q4 = _stack_heads_t(q_ref[...])
    qr4 = _stack_heads_t(qr_ref[...])
    tok = i * tq + lax.broadcasted_iota(jnp.int32, (1, tq), 1)
    gates = gate_ref[...]

    q0 = pl.multiple_of(i * tq, tq)
    span = WINDOW + tq
    wstart = pl.multiple_of(jnp.maximum(i * tq - WINDOW, 0), tq)

    def cmp_and_select(ncv, rows):
        sc = _dot(kc_ref[:ncv, :], q4)
        sw = _dot(kw_ref[pl.ds(wstart, span), :], qr4)
        sd = _dot(ks_ref[pl.ds(q0, tq), :], qr4)
        s0_ref[...] = _dot(ks_ref[:tk, :], qr4)

        n_idx = lax.broadcasted_iota(jnp.int32, (ncv, tq), 0)
        last_complete = lax.shift_right_arithmetic(tok - (CMP_LEN - 1), CMP_STRIDE.bit_length() - 1)
        es = _softmax_numerators_t(sc, lambda s: jnp.where(n_idx <= last_complete, s, NEG))
        wbias = wbias_ref[...]
        ew = _softmax_numerators_t(sw, lambda s: s + wbias)
        not_future = lax.broadcasted_iota(jnp.int32, (tq, tq), 0) <= lax.broadcasted_iota(jnp.int32, (tq, tq), 1)
        sd = jnp.concatenate([jnp.where(not_future, _head_cols(sd, h), NEG) for h in range(hp)], axis=1)
        m0 = jnp.max(sd, axis=0, keepdims=True)
        pd = jnp.exp2(sd - m0)

        o_cmp = _dot(vct_ref[:, :ncv], jnp.concatenate(es, axis=1).astype(BF16))
        o_win = _dot(vwt_ref[:, pl.ds(wstart, span)], jnp.concatenate(ew, axis=1).astype(BF16))
        m_ref[...] = m0
        acc_ref[...] = _dot(vst_ref[:, pl.ds(q0, tq)], pd.astype(BF16))
        inv_c = jnp.where(jnp.concatenate([tok >= CMP_LEN - 1] * hp, axis=1), 1.0 / o_cmp[HEAD_DIM:HEAD_DIM + 1], 0.0)
        ocmp_ref[...] = o_cmp[:HEAD_DIM] * inv_c
        owin_ref[...] = o_win[:HEAD_DIM] * (1.0 / o_win[HEAD_DIM:HEAD_DIM + 1])

        psum = sum(es[h] * _head_cols(inv_c, h) for h in range(hp))
        ovt = ovt_ref[:rows, :ncv]
        imp = sum(_dot(ovt, term) for term in _split3(psum))

        m_idx = lax.broadcasted_iota(jnp.int32, (rows, tq), 0)
        blk_t = tok // SEL_LEN
        forced = (m_idx == 0) | (m_idx == blk_t) | (m_idx == blk_t - 1)
        candidate = jnp.logical_and(m_idx <= blk_t, jnp.logical_not(forced))
        imp = jnp.where(candidate, imp, -jnp.inf)

        def pick(_, v):
            mx = jnp.max(v, axis=0, keepdims=True)
            idx = jnp.min(jnp.where(v == mx, m_idx, rows), axis=0, keepdims=True)
            return jnp.where(m_idx == idx, -jnp.inf, v)

        picked = lax.fori_loop(0, SEL_TOPK - 3, pick, imp, unroll=True) == -jnp.inf
        before = m_idx < (i * tq) // SEL_LEN
        bias_ref[:rows, :] = jnp.where(jnp.logical_and(picked, before), 0.0, NEG)
        if rows < n_sel:
            bias_ref[rows:, :] = jnp.full((n_sel - rows, tq), NEG, F32)

    rows_per_chunk = CMP_CHUNK * CMP_STRIDE // SEL_LEN
    n_prefix = -(-nc // CMP_CHUNK)
    need = ((i + 1) * tq - 1) // (CMP_STRIDE * CMP_CHUNK) + 1
    for c in range(1, n_prefix + 1):
        pl.when(need == c)(functools.partial(cmp_and_select, min(c * CMP_CHUNK, nc), min(c * rows_per_chunk, n_sel)))
    bpt = tk // SEL_LEN

    def sel_bias_and_max(kt, st, s_ref, mx_ref):
        rows = bias_ref[pl.ds(pl.multiple_of(kt * bpt, bpt), bpt), :]
        bias = jnp.concatenate([jnp.broadcast_to(rows[j:j + 1], (SEL_LEN, tq)) for j in range(bpt)], axis=0)
        s = jnp.concatenate([_head_cols(st, h) + bias for h in range(hp)], axis=1)
        s_ref[...] = s
        mx_ref[...] = jnp.max(s, axis=0, keepdims=True)

    def sel_scores(kt, s_ref, mx_ref):
        start = pl.multiple_of(kt * tk, tk)
        sel_bias_and_max(kt, _dot(ks_ref[pl.ds(start, tk), :], qr4), s_ref, mx_ref)

    def sel_accumulate(kt, s_ref, mx_ref):
        start = pl.multiple_of(kt * tk, tk)
        m = m_ref[...]
        m_new = jnp.maximum(m, mx_ref[...])
        p = jnp.exp2(s_ref[...] - m_new)
        acc_ref[...] = jnp.exp2(m - m_new) * acc_ref[...] + _dot(vst_ref[:, pl.ds(start, tk)], p.astype(BF16))
        m_ref[...] = m_new

    last = jnp.maximum(i * tq - 1, 0) // tk
    final_tile = ks_ref.shape[0] // tk - 1

    def sel_pair(j):
        sel_scores(2 * j + 1, s1_ref, mx1_ref)
        sel_accumulate(2 * j, s0_ref, mx0_ref)
        sel_scores(jnp.minimum(2 * j + 2, final_tile), s0_ref, mx0_ref)
        sel_accumulate(2 * j + 1, s1_ref, mx1_ref)

    def sel_pairs(jj, carry):
        for u in range(SEL_UNROLL):
            sel_pair(SEL_UNROLL * jj + u)
        return carry

    sel_bias_and_max(0, s0_ref[...], s0_ref, mx0_ref)
    pairs = (last + 1) // 2
    lax.fori_loop(0, pairs // SEL_UNROLL, sel_pairs, 0)
    done = pairs // SEL_UNROLL * SEL_UNROLL
    run = SEL_UNROLL // 2
    while run >= 1:
        def sel_run(done=done, run=run):
            for u in range(run):
                sel_pair(done + u)

        take = (pairs - done) >= run
        pl.when(take)(sel_run)
        done = done + jnp.where(take, run, 0)
        run //= 2

    @pl.when(last % 2 == 0)
    def _():
        sel_accumulate(last, s0_ref, mx0_ref)

    acc = acc_ref[...]
    o_sel = acc[:HEAD_DIM] * (1.0 / acc[HEAD_DIM:HEAD_DIM + 1])

    o_cmp = ocmp_ref[...]
    o_win = owin_ref[...]
    for h in range(hp):
        mix = (gates[h:h + 1] * _head_cols(o_cmp, h)
               + gates[hp + h:hp + h + 1] * _head_cols(o_sel, h)
               + gates[2 * hp + h:2 * hp + h + 1] * _head_cols(o_win, h))
        o_ref[:, HEAD_DIM * h:HEAD_DIM * (h + 1)] = mix.T.astype(BF16)


def _nsa_attention(q, q_rot, k_cmp, v_cmp_t, overlap_t, k_slc, v_slc_t, k_win, v_win_t, gates):
    s, d = q.shape
    nc = k_cmp.shape[1]
    n_sel = overlap_t.shape[0]
    gw = HEADS_PER_GROUP * HEAD_DIM
    cols = HEADS_PER_GROUP * Q_TILE
    tk = 512
    n_wcase = WINDOW // Q_TILE + 1
    lead = (jnp.arange(n_wcase) * Q_TILE)[:, None, None]
    dist = lead + jnp.arange(Q_TILE)[None, None, :] - jnp.arange(WINDOW + Q_TILE)[None, :, None]
    wbias = jnp.where((dist >= 0) & (dist < WINDOW), 0.0, NEG).astype(F32)
    qspec = pl.BlockSpec((Q_TILE, gw), lambda g, i: (i, g))
    per_group = lambda shape: pl.BlockSpec((None,) + shape, lambda g, i: (g, 0, 0))
    return pl.pallas_call(
        functools.partial(_nsa_kernel, tk),
        grid=(KV_GROUPS, s // Q_TILE),
        in_specs=[qspec, qspec,
                  per_group((nc, HEAD_DIM)), per_group((V_ROWS, nc)),
                  pl.BlockSpec((n_sel, nc), lambda g, i: (0, 0)),
                  per_group((s, HEAD_DIM)), per_group((V_ROWS, s)),
                  per_group((s, HEAD_DIM)), per_group((V_ROWS, s)),
                  pl.BlockSpec((None, WINDOW + Q_TILE, Q_TILE), lambda g, i: (jnp.minimum(i, n_wcase - 1), 0, 0)),
                  pl.BlockSpec((None, GATE_ROWS, Q_TILE), lambda g, i: (g, 0, i))],
        out_specs=qspec,
        out_shape=jax.ShapeDtypeStruct((s, d), BF16),
        scratch_shapes=[pltpu.VMEM((HEAD_DIM, cols), F32), pltpu.VMEM((HEAD_DIM, cols), F32),
                        pltpu.VMEM((n_sel, Q_TILE), F32),
                        pltpu.VMEM((tk, cols), F32), pltpu.VMEM((tk, cols), F32),
                        pltpu.VMEM((1, cols), F32), pltpu.VMEM((1, cols), F32),
                        pltpu.VMEM((1, cols), F32), pltpu.VMEM((V_ROWS, cols), F32)],
        compiler_params=_params(("arbitrary", "arbitrary")),
        name="nsa_attention",
    )(q, q_rot, k_cmp, v_cmp_t, overlap_t, k_slc, v_slc_t, k_win, v_win_t, wbias, gates)


def _rope_tables(s):
    half = ROT_DIM // 2
    inv_freq = ROPE_THETA ** (-jnp.arange(half, dtype=F32) * 2.0 / ROT_DIM)
    ang = jnp.arange(s, dtype=F32)[:, None] * inv_freq[None, :]
    cos, sin = jnp.cos(ang), jnp.sin(ang)
    rest = HEAD_DIM - ROT_DIM
    c = jnp.concatenate([cos, cos, jnp.ones((s, rest), F32)], axis=1)
    s1 = jnp.concatenate([-sin, jnp.zeros((s, half + rest), F32)], axis=1)
    s2 = jnp.concatenate([jnp.zeros((s, half), F32), sin, jnp.zeros((s, rest), F32)], axis=1)
    return tuple(jnp.tile(a, (1, 2)) for a in (c, s1, s2))


def _overlap_matrix(nch, n_sel):
    cmp_start = jnp.arange(nch)[:, None] * CMP_STRIDE
    sel_start = jnp.arange(n_sel)[None, :] * SEL_LEN
    return ((cmp_start < sel_start + SEL_LEN) & (cmp_start + CMP_LEN > sel_start)).astype(BF16)


def kernel(x, norm_gain, sb_w_qkv, sb_w_o, kv_norm, nsa_w_kv, cmp_pos, cmp_w1, cmp_w2,
           nsa_w_q, nsa_gate_b, nsa_w_o, mlp_w1, mlp_w2, final_norm):
    b, s, d = x.shape
    assert b == 1 and d == N_HEADS * HEAD_DIM
    assert s % 512 == 0 and s >= WINDOW + Q_TILE and s // SEL_LEN >= SEL_TOPK
    n_cmp = (s - CMP_LEN) // CMP_STRIDE + 1
    nch = s // CMP_STRIDE
    n_sel = s // SEL_LEN
    h0 = x[0]
    rope_tabs = _rope_tables(s)

    o_sb = _sb_attention(*_qkv_proj(h0, norm_gain[0, 0], sb_w_qkv[0].astype(BF16)))
    h1 = _attn_out_mlp(o_sb, h0, sb_w_o[0].astype(BF16), norm_gain[0, 1],
                       mlp_w1[0].astype(BF16), mlp_w2[0].astype(BF16))

    cmp_kv, k_slc, v_slc_t, k_win, v_win_t = _kv_proj(h1, kv_norm, nsa_w_kv.astype(BF16), rope_tabs)
    pos_flat = jnp.broadcast_to(cmp_pos.reshape(2, 1, CMP_LEN * HEAD_DIM), (2, 8, CMP_LEN * HEAD_DIM))
    kv_cmp, kv_cmp_t = _cmp_mlp(cmp_kv, pos_flat, cmp_w1.astype(BF16), cmp_w2.astype(BF16), n_cmp)

    n_qk = N_HEADS * HEAD_DIM
    w_q = jnp.concatenate([nsa_w_q[0][:, :n_qk], _group_gate_columns(nsa_w_q[0][:, n_qk:])], axis=1).astype(BF16)
    gate_b = _group_gate_columns(nsa_gate_b[0]).reshape(1, 128)
    q, q_rot, gates = _q_proj(h1, norm_gain[1, 0], w_q, gate_b, rope_tabs)
    o_nsa = _nsa_attention(q, q_rot, kv_cmp[0], kv_cmp_t[1], _overlap_matrix(nch, n_sel).T,
                           k_slc, v_slc_t, k_win, v_win_t, gates)
    out = _attn_out_mlp(o_nsa, h1, nsa_w_o[0].astype(BF16), norm_gain[1, 1],
                        mlp_w1[1].astype(BF16), mlp_w2[1].astype(BF16), final_gain=final_norm)
    return out[None]
```

```python
import functools

import jax
import jax.numpy as jnp
from jax import lax
from jax.experimental import pallas as pl
from jax.experimental.pallas import tpu as pltpu

HEAD_DIM = 64
N_HEADS = 16
KV_GROUPS = 4
HEADS_PER_GROUP = N_HEADS // KV_GROUPS
CMP_LEN = 32
CMP_STRIDE = 16
CMP_HIDDEN = 256
SEL_LEN = 64
SEL_TOPK = 16
WINDOW = 512
ROPE_THETA = 500000.0
ROT_DIM = HEAD_DIM // 4
Q_TILE = 128
GATE_ROWS = 16
V_ROWS = HEAD_DIM + 16
LOG2E = 1.4426950408889634
CMP_CHUNK = 256
SEL_UNROLL = 4
PROJ_SUB = 256
NORM_EPS = 1e-5
NEG = -1e30
SB_UNDERFLOW = -110.0
SB_NEAR = 256
VMEM_LIMIT = 56 * 1024 * 1024

BF16 = jnp.bfloat16
F32 = jnp.float32


def _params(semantics, vmem=VMEM_LIMIT):
    return pltpu.CompilerParams(dimension_semantics=semantics, vmem_limit_bytes=vmem)


def _rms(x, g):
    return x * lax.rsqrt(jnp.mean(x * x, axis=-1, keepdims=True) + NORM_EPS) * g


def _dot(a, b):
    return jnp.dot(a, b, preferred_element_type=F32)


def _rope128(x, c, s1, s2):
    return x * c + pltpu.roll(x, 128 - ROT_DIM // 2, axis=1) * s1 + pltpu.roll(x, ROT_DIM // 2, axis=1) * s2


def _split3(x):
    hi = x.astype(BF16)
    r = x - hi.astype(F32)
    mid = r.astype(BF16)
    lo = (r - mid.astype(F32)).astype(BF16)
    return hi, mid, lo


def _qkv_kernel(x_ref, g_ref, w_ref, q_ref, kt_ref, v_ref):
    xn = _rms(x_ref[...], g_ref[...]).astype(BF16)
    acc = _dot(xn, w_ref[...])
    d = x_ref.shape[1]
    q_ref[...] = (acc[:, :d] * (HEAD_DIM ** -0.5)).astype(BF16)
    kt_ref[...] = acc[:, d:2 * d].T.astype(BF16)
    v_ref[...] = acc[:, 2 * d:].astype(BF16)


def _qkv_proj(x, gain, w):
    s, d = x.shape
    n = w.shape[1]
    tm = 256
    row = pl.BlockSpec((tm, d), lambda i: (i, 0))
    return pl.pallas_call(
        _qkv_kernel,
        grid=(s // tm,),
        in_specs=[row, pl.BlockSpec((1, d), lambda i: (0, 0)), pl.BlockSpec((d, n), lambda i: (0, 0))],
        out_specs=[row, pl.BlockSpec((d, tm), lambda i: (0, i)), row],
        out_shape=[jax.ShapeDtypeStruct((s, d), BF16), jax.ShapeDtypeStruct((d, s), BF16),
                   jax.ShapeDtypeStruct((s, d), BF16)],
        compiler_params=_params(("arbitrary",)),
        name="qkv_proj",
    )(x, gain.reshape(1, d), w)


def _sb_blocks(qs, kts, vs, tri, carries, mask):
    n = len(qs)
    zs = [_dot(qs[h], kts[h]) for h in range(n)]
    sps = [jnp.maximum(z, 0.0) + jnp.log(1.0 + jnp.exp(-jnp.abs(z))) for z in zs]
    logs = [-sp for sp in sps]
    if mask is not None:
        logs = [jnp.where(mask, x, 0.0) for x in logs]
    excls = [_dot(logs[h].astype(BF16), tri) for h in range(n)]
    ws = [jnp.exp((zs[h] - sps[h]) + (excls[h] + carries[h])) for h in range(n)]
    if mask is not None:
        ws = [jnp.where(mask, w, 0.0) for w in ws]
    outs = [_dot(ws[h].astype(BF16), vs[h]) for h in range(n)]
    new_carries = [carries[h] + jnp.sum(logs[h], axis=1, keepdims=True) for h in range(n)]
    return outs, new_carries


def _sb_kernel(q_ref, kt_ref, v_ref, tri_ref, o_ref):
    i = pl.program_id(1)
    tq = q_ref.shape[0]
    nh = q_ref.shape[1] // HEAD_DIM
    near = tri_ref.shape[0]
    tri_near = tri_ref[...]
    tri = tri_near[:tq, :tq]
    q = q_ref[...]
    head = lambda x, h: x[:, HEAD_DIM * h:HEAD_DIM * (h + 1)]

    start = pl.multiple_of(jnp.maximum(i * tq - (near - tq), 0), tq)
    row = lax.broadcasted_iota(jnp.int32, (tq, near), 0)
    col = lax.broadcasted_iota(jnp.int32, (tq, near), 1)
    causal = col - row < i * tq - start
    kt = kt_ref[:, pl.ds(start, near)]
    v = v_ref[pl.ds(start, near), :]
    qs = [head(q, h) for h in range(nh)]
    split = lambda x: [head(x, h) for h in range(nh)]
    split_t = lambda x: [x[HEAD_DIM * h:HEAD_DIM * (h + 1)] for h in range(nh)]
    outs, carries = _sb_blocks(qs, split_t(kt), split(v), tri_near, [jnp.zeros((tq, 1), F32)] * nh, causal)

    def worst(cs):
        mx = jnp.max(cs[0])
        for c in cs[1:]:
            mx = jnp.maximum(mx, jnp.max(c))
        return mx

    def cond(st):
        return jnp.logical_and(st[0] >= 0, st[1] > SB_UNDERFLOW)

    def body(st):
        kb, _, outs, carries = st
        start = pl.multiple_of(kb * tq, tq)
        kt = kt_ref[:, pl.ds(start, tq)]
        v = v_ref[pl.ds(start, tq), :]
        more, new_c = _sb_blocks(qs, split_t(kt), split(v), tri, list(carries), None)
        return kb - 1, worst(new_c), tuple(outs[h] + more[h] for h in range(nh)), tuple(new_c)

    st = lax.while_loop(cond, body, (start // tq - 1, worst(carries), tuple(outs), tuple(carries)))
    for h in range(nh):
        o_ref[:, HEAD_DIM * h:HEAD_DIM * (h + 1)] = st[2][h].astype(BF16)


def _sb_attention(q, k_t, v):
    s, d = q.shape
    hps = 8
    ngrp = N_HEADS // hps
    w = hps * HEAD_DIM
    near = SB_NEAR + Q_TILE
    idx = jnp.arange(near)
    tri = (idx[:, None] > idx[None, :]).astype(BF16)
    return pl.pallas_call(
        _sb_kernel,
        grid=(ngrp, s // Q_TILE),
        in_specs=[pl.BlockSpec((Q_TILE, w), lambda h, i: (i, h)),
                  pl.BlockSpec((w, s), lambda h, i: (h, 0), pipeline_mode=pl.Buffered(1)),
                  pl.BlockSpec((s, w), lambda h, i: (0, h), pipeline_mode=pl.Buffered(1)),
                  pl.BlockSpec((near, near), lambda h, i: (0, 0))],
        out_specs=pl.BlockSpec((Q_TILE, w), lambda h, i: (i, h)),
        out_shape=jax.ShapeDtypeStruct((s, d), BF16),
        compiler_params=_params(("arbitrary", "arbitrary")),
        name="sb_attention",
    )(q, k_t, v, tri)


def _mlp_kernel(final, tf, o_ref, res_ref, wo_ref, g_ref, w1_ref, w2_ref, *rest):
    out_ref = rest[-1]
    h = res_ref[...] + _dot(o_ref[...], wo_ref[...])
    xn = _rms(h, g_ref[...]).astype(BF16)
    out_ref[...] = h
    for j in range(w1_ref.shape[1] // tf):
        a = jnp.maximum(_dot(xn, w1_ref[:, tf * j:tf * (j + 1)]), 0.0)
        out_ref[...] += _dot((a * a).astype(BF16), w2_ref[tf * j:tf * (j + 1), :])
    if final:
        out_ref[...] = _rms(out_ref[...], rest[0][...])


def _attn_out_mlp(o, resid, wo, gain, w1, w2, final_gain=None):
    s, d = resid.shape
    f = w1.shape[1]
    tm, tf = 512, 512
    final = final_gain is not None
    row = lambda i: (i, 0)
    resident = lambda shape: pl.BlockSpec(shape, lambda i: (0, 0), pipeline_mode=pl.Buffered(1))
    in_specs = [pl.BlockSpec((tm, d), row), pl.BlockSpec((tm, d), row), resident((d, d)), resident((1, d)),
                resident((d, f)), resident((f, d))]
    args = [o, resid, wo, gain.reshape(1, d), w1, w2]
    if final:
        in_specs.append(resident((1, d)))
        args.append(final_gain.reshape(1, d))
    return pl.pallas_call(
        functools.partial(_mlp_kernel, final, tf),
        grid=(s // tm,),
        in_specs=in_specs,
        out_specs=pl.BlockSpec((tm, d), row),
        out_shape=jax.ShapeDtypeStruct((s, d), F32),
        compiler_params=_params(("arbitrary",)),
        name="attn_out_mlp",
    )(*args)


def _row_chunks(tm):
    return [slice(r, r + PROJ_SUB) for r in range(0, tm, PROJ_SUB)]


def _kv_kernel(x_ref, g_ref, w_ref, c_ref, s1_ref, s2_ref, cmp_ref, ks_ref, vs_ref, kw_ref, vw_ref):
    gw = KV_GROUPS * HEAD_DIM
    chunks = _row_chunks(x_ref.shape[0])
    accs = [_dot(_rms(x_ref[rows, :], g_ref[...]).astype(BF16), w_ref[...]) for rows in chunks]
    ones_row = (lax.broadcasted_iota(jnp.int32, (V_ROWS - HEAD_DIM, PROJ_SUB), 0) == 0).astype(BF16)

    def put_k(ref, x, rows):
        c, s1, s2 = c_ref[rows, :], s1_ref[rows, :], s2_ref[rows, :]
        for p in range(gw // 128):
            xp = _rope128(x[:, 128 * p:128 * (p + 1)], c, s1, s2)
            ref[2 * p, rows, :] = xp[:, :HEAD_DIM].astype(BF16)
            ref[2 * p + 1, rows, :] = xp[:, HEAD_DIM:].astype(BF16)

    def put_vt(ref, x, rows):
        xt = x.T
        for g in range(KV_GROUPS):
            ref[g, :HEAD_DIM, rows] = xt[HEAD_DIM * g:HEAD_DIM * (g + 1)].astype(BF16)
            ref[g, HEAD_DIM:, rows] = ones_row

    for rows, acc in zip(chunks, accs):
        cmp_ref[rows, :] = acc[:, :2 * gw]
        put_k(ks_ref, acc[:, 2 * gw:3 * gw], rows)
        put_vt(vs_ref, acc[:, 3 * gw:4 * gw], rows)
        put_k(kw_ref, acc[:, 4 * gw:5 * gw], rows)
        put_vt(vw_ref, acc[:, 5 * gw:6 * gw], rows)


def _kv_proj(h, gain, w, rope_tabs):
    s, d = h.shape
    n = w.shape[1]
    gw = KV_GROUPS * HEAD_DIM
    tm = 2 * PROJ_SUB
    row = lambda i: (i, 0)
    fixed = lambda i: (0, 0)
    k_spec = pl.BlockSpec((KV_GROUPS, tm, HEAD_DIM), lambda i: (0, i, 0))
    k_shape = jax.ShapeDtypeStruct((KV_GROUPS, s, HEAD_DIM), BF16)
    vt_spec = pl.BlockSpec((KV_GROUPS, V_ROWS, tm), lambda i: (0, 0, i))
    vt_shape = jax.ShapeDtypeStruct((KV_GROUPS, V_ROWS, s), BF16)
    return pl.pallas_call(
        _kv_kernel,
        grid=(s // tm,),
        in_specs=[pl.BlockSpec((tm, d), row), pl.BlockSpec((1, d), fixed), pl.BlockSpec((d, n), fixed),
                  pl.BlockSpec((tm, 128), row), pl.BlockSpec((tm, 128), row), pl.BlockSpec((tm, 128), row)],
        out_specs=[pl.BlockSpec((tm, 2 * gw), row), k_spec, vt_spec, k_spec, vt_spec],
        out_shape=[jax.ShapeDtypeStruct((s, 2 * gw), F32), k_shape, vt_shape, k_shape, vt_shape],
        compiler_params=_params(("arbitrary",)),
        name="kv_proj",
    )(h, gain.reshape(1, d), w, *rope_tabs)


def _cmp_kernel(n_cmp, x_ref, pos_ref, w1_ref, w2_ref, o_ref, ot_ref):
    nch = x_ref.shape[0] // CMP_STRIDE
    half = CMP_STRIDE * HEAD_DIM
    w1 = w1_ref[...]
    bias = _dot(pos_ref[...].astype(BF16), w1)[0:1]
    acc = [[jnp.zeros((nch, CMP_HIDDEN), F32) for _ in range(2)] for _ in range(2)]
    for l in range(CMP_STRIDE):
        y = x_ref[pl.ds(l, nch, stride=CMP_STRIDE), :].astype(BF16)
        wa = w1[HEAD_DIM * l:HEAD_DIM * (l + 1)]
        wb = w1[half + HEAD_DIM * l:half + HEAD_DIM * (l + 1)]
        for gg in range(2):
            yg = y[:, HEAD_DIM * gg:HEAD_DIM * (gg + 1)]
            acc[gg][0] = acc[gg][0] + _dot(yg, wa)
            acc[gg][1] = acc[gg][1] + _dot(yg, wb)
    live = lax.broadcasted_iota(jnp.int32, (nch, HEAD_DIM), 0) < n_cmp
    ones_row = (lax.broadcasted_iota(jnp.int32, (V_ROWS - HEAD_DIM, nch), 0) == 0).astype(BF16)
    outs = []
    for gg in range(2):
        hid = acc[gg][0] + pltpu.roll(acc[gg][1], nch - 1, axis=0) + bias
        hid = jax.nn.gelu(hid, approximate=True).astype(BF16)
        outs.append(jnp.where(live, _dot(hid, w2_ref[...]), 0.0))
        o_ref[gg] = outs[gg].astype(BF16)
    out_t = jnp.concatenate(outs, axis=1).T
    for gg in range(2):
        ot_ref[gg, :HEAD_DIM, :] = out_t[HEAD_DIM * gg:HEAD_DIM * (gg + 1)].astype(BF16)
        ot_ref[gg, HEAD_DIM:, :] = ones_row


def _cmp_mlp(cmp_kv, pos_flat, w1, w2, n_cmp):
    s = cmp_kv.shape[0]
    nch = s // CMP_STRIDE
    feat = CMP_LEN * HEAD_DIM
    return pl.pallas_call(
        functools.partial(_cmp_kernel, n_cmp),
        grid=(2, KV_GROUPS // 2),
        in_specs=[pl.BlockSpec((s, 2 * HEAD_DIM), lambda c, p: (0, c * (KV_GROUPS // 2) + p)),
                  pl.BlockSpec((None, 8, feat), lambda c, p: (c, 0, 0)),
                  pl.BlockSpec((None, feat, CMP_HIDDEN), lambda c, p: (c, 0, 0)),
                  pl.BlockSpec((None, CMP_HIDDEN, HEAD_DIM), lambda c, p: (c, 0, 0))],
        out_specs=[pl.BlockSpec((None, 2, nch, HEAD_DIM), lambda c, p: (c, p, 0, 0)),
                   pl.BlockSpec((None, 2, V_ROWS, nch), lambda c, p: (c, p, 0, 0))],
        out_shape=[jax.ShapeDtypeStruct((2, KV_GROUPS, nch, HEAD_DIM), BF16),
                   jax.ShapeDtypeStruct((2, KV_GROUPS, V_ROWS, nch), BF16)],
        compiler_params=_params(("arbitrary", "arbitrary")),
        name="cmp_mlp",
    )(cmp_kv, pos_flat, w1, w2)


def _q_kernel(x_ref, g_ref, w_ref, b_ref, c_ref, s1_ref, s2_ref, q_ref, qr_ref, gate_ref):
    d = q_ref.shape[1]
    chunks = _row_chunks(x_ref.shape[0])
    accs = [_dot(_rms(x_ref[rows, :], g_ref[...]).astype(BF16), w_ref[...]) for rows in chunks]
    for rows, acc in zip(chunks, accs):
        c, s1, s2 = c_ref[rows, :], s1_ref[rows, :], s2_ref[rows, :]
        for p in range(d // 128):
            qp = acc[:, 128 * p:128 * (p + 1)] * (HEAD_DIM ** -0.5 * LOG2E)
            q_ref[rows, 128 * p:128 * (p + 1)] = qp.astype(BF16)
            qr_ref[rows, 128 * p:128 * (p + 1)] = _rope128(qp, c, s1, s2).astype(BF16)
        gates_t = (1.0 / (1.0 + jnp.exp(-(acc[:, d:] + b_ref[...])))).T
        for g in range(KV_GROUPS):
            gate_ref[g, :, rows] = gates_t[GATE_ROWS * g:GATE_ROWS * (g + 1)]


def _q_proj(h, gain, w, gate_b, rope_tabs):
    s, d = h.shape
    n = w.shape[1]
    tm = 2 * PROJ_SUB
    row = lambda i: (i, 0)
    fixed = lambda i: (0, 0)
    return pl.pallas_call(
        _q_kernel,
        grid=(s // tm,),
        in_specs=[pl.BlockSpec((tm, d), row), pl.BlockSpec((1, d), fixed), pl.BlockSpec((d, n), fixed),
                  pl.BlockSpec((1, 128), fixed),
                  pl.BlockSpec((tm, 128), row), pl.BlockSpec((tm, 128), row), pl.BlockSpec((tm, 128), row)],
        out_specs=[pl.BlockSpec((tm, d), row), pl.BlockSpec((tm, d), row),
                   pl.BlockSpec((KV_GROUPS, GATE_ROWS, tm), lambda i: (0, 0, i))],
        out_shape=[jax.ShapeDtypeStruct((s, d), BF16), jax.ShapeDtypeStruct((s, d), BF16),
                   jax.ShapeDtypeStruct((KV_GROUPS, GATE_ROWS, s), F32)],
        compiler_params=_params(("arbitrary",)),
        name="nsa_q_proj",
    )(h, gain.reshape(1, d), w, gate_b, *rope_tabs)


def _group_gate_columns(a):
    lead = a.shape[:-1]
    a = a.reshape(lead + (3, KV_GROUPS, HEADS_PER_GROUP))
    a = jnp.moveaxis(a, -2, -3).reshape(lead + (KV_GROUPS, 3 * HEADS_PER_GROUP))
    a = jnp.pad(a, [(0, 0)] * len(lead) + [(0, 0), (0, GATE_ROWS - 3 * HEADS_PER_GROUP)])
    a = a.reshape(lead + (KV_GROUPS * GATE_ROWS,))
    return jnp.pad(a, [(0, 0)] * len(lead) + [(0, 128 - KV_GROUPS * GATE_ROWS)])


def _stack_heads_t(q):
    qt = q.astype(F32).T
    return jnp.concatenate([qt[HEAD_DIM * h:HEAD_DIM * (h + 1)] for h in range(HEADS_PER_GROUP)], axis=1).astype(BF16)


def _head_cols(x, h):
    return x[:, Q_TILE * h:Q_TILE * (h + 1)]


def _softmax_numerators_t(st, mask):
    es = []
    for h in range(HEADS_PER_GROUP):
        s = mask(_head_cols(st, h))
        es.append(jnp.exp2(s - jnp.max(s, axis=0, keepdims=True)))
    return es


def _nsa_kernel(tk, q_ref, qr_ref, kc_ref, vct_ref, ovt_ref, ks_ref, vst_ref, kw_ref, vwt_ref, wbias_ref, gate_ref,
                o_ref, ocmp_ref, owin_ref, bias_ref, s0_ref, s1_ref, mx0_ref, mx1_ref, m_ref, acc_ref):
    i = pl.program_id(1)
    tq = q_ref.shape[0]
    nc = kc_ref.shape[0]
    n_sel = ovt_ref.shape[0]
    hp = HEADS_PER_GROUP
    q4 = _stack_heads_t(q_ref[...])
    qr4 = _stack_heads_t(qr_ref[...])
    tok = i * tq + lax.broadcasted_iota(jnp.int32, (1, tq), 1)
    gates = gate_ref[...]

    q0 = pl.multiple_of(i * tq, tq)
    span = WINDOW + tq
    wstart = pl.multiple_of(jnp.maximum(i * tq - WINDOW, 0), tq)

    def cmp_and_select(ncv, rows):
        sc = _dot(kc_ref[:ncv, :], q4)
        sw = _dot(kw_ref[pl.ds(wstart, span), :], qr4)
        sd = _dot(ks_ref[pl.ds(q0, tq), :], qr4)
        s0_ref[...] = _dot(ks_ref[:tk, :], qr4)

        n_idx = lax.broadcasted_iota(jnp.int32, (ncv, tq), 0)
        last_complete = lax.shift_right_arithmetic(tok - (CMP_LEN - 1), CMP_STRIDE.bit_length() - 1)
        es = _softmax_numerators_t(sc, lambda s: jnp.where(n_idx <= last_complete, s, NEG))
        wbias = wbias_ref[...]
        ew = _softmax_numerators_t(sw, lambda s: s + wbias)
        not_future = lax.broadcasted_iota(jnp.int32, (tq, tq), 0) <= lax.broadcasted_iota(jnp.int32, (tq, tq), 1)
        sd = jnp.concatenate([jnp.where(not_future, _head_cols(sd, h), NEG) for h in range(hp)], axis=1)
        m0 = jnp.max(sd, axis=0, keepdims=True)
        pd = jnp.exp2(sd - m0)

        o_cmp = _dot(vct_ref[:, :ncv], jnp.concatenate(es, axis=1).astype(BF16))
        o_win = _dot(vwt_ref[:, pl.ds(wstart, span)], jnp.concatenate(ew, axis=1).astype(BF16))
        m_ref[...] = m0
        acc_ref[...] = _dot(vst_ref[:, pl.ds(q0, tq)], pd.astype(BF16))
        inv_c = jnp.where(jnp.concatenate([tok >= CMP_LEN - 1] * hp, axis=1), 1.0 / o_cmp[HEAD_DIM:HEAD_DIM + 1], 0.0)
        ocmp_ref[...] = o_cmp[:HEAD_DIM] * inv_c
        owin_ref[...] = o_win[:HEAD_DIM] * (1.0 / o_win[HEAD_DIM:HEAD_DIM + 1])

        psum = sum(es[h] * _head_cols(inv_c, h) for h in range(hp))
        ovt = ovt_ref[:rows, :ncv]
        imp = sum(_dot(ovt, term) for term in _split3(psum))

        m_idx = lax.broadcasted_iota(jnp.int32, (rows, tq), 0)
        blk_t = tok // SEL_LEN
        forced = (m_idx == 0) | (m_idx == blk_t) | (m_idx == blk_t - 1)
        candidate = jnp.logical_and(m_idx <= blk_t, jnp.logical_not(forced))
        imp = jnp.where(candidate, imp, -jnp.inf)

        m_f32 = m_idx.astype(F32)

        def pick(_, v):
            mx = jnp.max(v, axis=0, keepdims=True)
            idx = jnp.min(jnp.where(v == mx, m_f32, float(rows)), axis=0, keepdims=True)
            return jnp.where(m_f32 == idx, -jnp.inf, v)

        picked = lax.fori_loop(0, SEL_TOPK - 3, pick, imp, unroll=True) == -jnp.inf
        before = m_idx < (i * tq) // SEL_LEN
        bias_ref[:rows, :] = jnp.where(jnp.logical_and(picked, before), 0.0, NEG)
        if rows < n_sel:
            bias_ref[rows:, :] = jnp.full((n_sel - rows, tq), NEG, F32)

    rows_per_chunk = CMP_CHUNK * CMP_STRIDE // SEL_LEN
    n_prefix = -(-nc // CMP_CHUNK)
    need = ((i + 1) * tq - 1) // (CMP_STRIDE * CMP_CHUNK) + 1
    for c in range(1, n_prefix + 1):
        pl.when(need == c)(functools.partial(cmp_and_select, min(c * CMP_CHUNK, nc), min(c * rows_per_chunk, n_sel)))
    bpt = tk // SEL_LEN

    def sel_bias_and_max(kt, st, s_ref, mx_ref):
        rows = bias_ref[pl.ds(pl.multiple_of(kt * bpt, bpt), bpt), :]
        bias = jnp.concatenate([jnp.broadcast_to(rows[j:j + 1], (SEL_LEN, tq)) for j in range(bpt)], axis=0)
        s = jnp.concatenate([_head_cols(st, h) + bias for h in range(hp)], axis=1)
        s_ref[...] = s
        mx_ref[...] = jnp.max(s, axis=0, keepdims=True)

    def sel_scores(kt, s_ref, mx_ref):
        start = pl.multiple_of(kt * tk, tk)
        sel_bias_and_max(kt, _dot(ks_ref[pl.ds(start, tk), :], qr4), s_ref, mx_ref)

    def sel_accumulate(kt, s_ref, mx_ref):
        start = pl.multiple_of(kt * tk, tk)
        m = m_ref[...]
        m_new = jnp.maximum(m, mx_ref[...])
        p = jnp.exp2(s_ref[...] - m_new)
        acc_ref[...] = jnp.exp2(m - m_new) * acc_ref[...] + _dot(vst_ref[:, pl.ds(start, tk)], p.astype(BF16))
        m_ref[...] = m_new

    last = jnp.maximum(i * tq - 1, 0) // tk
    final_tile = ks_ref.shape[0] // tk - 1

    def sel_pair(j):
        sel_scores(2 * j + 1, s1_ref, mx1_ref)
        sel_accumulate(2 * j, s0_ref, mx0_ref)
        sel_scores(jnp.minimum(2 * j + 2, final_tile), s0_ref, mx0_ref)
        sel_accumulate(2 * j + 1, s1_ref, mx1_ref)

    def sel_pairs(jj, carry):
        for u in range(SEL_UNROLL):
            sel_pair(SEL_UNROLL * jj + u)
        return carry

    sel_bias_and_max(0, s0_ref[...], s0_ref, mx0_ref)
    pairs = (last + 1) // 2
    lax.fori_loop(0, pairs // SEL_UNROLL, sel_pairs, 0)
    done = pairs // SEL_UNROLL * SEL_UNROLL
    run = SEL_UNROLL // 2
    while run >= 1:
        def sel_run(done=done, run=run):
            for u in range(run):
                sel_pair(done + u)

        take = (pairs - done) >= run
        pl.when(take)(sel_run)
        done = done + jnp.where(take, run, 0)
        run //= 2

    @pl.when(last % 2 == 0)
    def _():
        sel_accumulate(last, s0_ref, mx0_ref)

    acc = acc_ref[...]
    o_sel = acc[:HEAD_DIM] * (1.0 / acc[HEAD_DIM:HEAD_DIM + 1])

    o_cmp = ocmp_ref[...]
    o_win = owin_ref[...]
    for h in range(hp):
        mix = (gates[h:h + 1] * _head_cols(o_cmp, h)
               + gates[hp + h:hp + h + 1] * _head_cols(o_sel, h)
               + gates[2 * hp + h:2 * hp + h + 1] * _head_cols(o_win, h))
        o_ref[:, HEAD_DIM * h:HEAD_DIM * (h + 1)] = mix.T.astype(BF16)


def _nsa_attention(q, q_rot, k_cmp, v_cmp_t, overlap_t, k_slc, v_slc_t, k_win, v_win_t, gates):
    s, d = q.shape
    nc = k_cmp.shape[1]
    n_sel = overlap_t.shape[0]
    gw = HEADS_PER_GROUP * HEAD_DIM
    cols = HEADS_PER_GROUP * Q_TILE
    tk = 512
    n_wcase = WINDOW // Q_TILE + 1
    lead = (jnp.arange(n_wcase) * Q_TILE)[:, None, None]
    dist = lead + jnp.arange(Q_TILE)[None, None, :] - jnp.arange(WINDOW + Q_TILE)[None, :, None]
    wbias = jnp.where((dist >= 0) & (dist < WINDOW), 0.0, NEG).astype(F32)
    qspec = pl.BlockSpec((Q_TILE, gw), lambda g, i: (i, g))
    per_group = lambda shape: pl.BlockSpec((None,) + shape, lambda g, i: (g, 0, 0))
    return pl.pallas_call(
        functools.partial(_nsa_kernel, tk),
        grid=(KV_GROUPS, s // Q_TILE),
        in_specs=[qspec, qspec,
                  per_group((nc, HEAD_DIM)), per_group((V_ROWS, nc)),
                  pl.BlockSpec((n_sel, nc), lambda g, i: (0, 0)),
                  per_group((s, HEAD_DIM)), per_group((V_ROWS, s)),
                  per_group((s, HEAD_DIM)), per_group((V_ROWS, s)),
                  pl.BlockSpec((None, WINDOW + Q_TILE, Q_TILE), lambda g, i: (jnp.minimum(i, n_wcase - 1), 0, 0)),
                  pl.BlockSpec((None, GATE_ROWS, Q_TILE), lambda g, i: (g, 0, i))],
        out_specs=qspec,
        out_shape=jax.ShapeDtypeStruct((s, d), BF16),
        scratch_shapes=[pltpu.VMEM((HEAD_DIM, cols), F32), pltpu.VMEM((HEAD_DIM, cols), F32),
                        pltpu.VMEM((n_sel, Q_TILE), F32),
                        pltpu.VMEM((tk, cols), F32), pltpu.VMEM((tk, cols), F32),
                        pltpu.VMEM((1, cols), F32), pltpu.VMEM((1, cols), F32),
                        pltpu.VMEM((1, cols), F32), pltpu.VMEM((V_ROWS, cols), F32)],
        compiler_params=_params(("arbitrary", "arbitrary")),
        name="nsa_attention",
    )(q, q_rot, k_cmp, v_cmp_t, overlap_t, k_slc, v_slc_t, k_win, v_win_t, wbias, gates)


def _rope_tables(s):
    half = ROT_DIM // 2
    inv_freq = ROPE_THETA ** (-jnp.arange(half, dtype=F32) * 2.0 / ROT_DIM)
    ang = jnp.arange(s, dtype=F32)[:, None] * inv_freq[None, :]
    cos, sin = jnp.cos(ang), jnp.sin(ang)
    rest = HEAD_DIM - ROT_DIM
    c = jnp.concatenate([cos, cos, jnp.ones((s, rest), F32)], axis=1)
    s1 = jnp.concatenate([-sin, jnp.zeros((s, half + rest), F32)], axis=1)
    s2 = jnp.concatenate([jnp.zeros((s, half), F32), sin, jnp.zeros((s, rest), F32)], axis=1)
    return tuple(jnp.tile(a, (1, 2)) for a in (c, s1, s2))


def _overlap_matrix(nch, n_sel):
    cmp_start = jnp.arange(nch)[:, None] * CMP_STRIDE
    sel_start = jnp.arange(n_sel)[None, :] * SEL_LEN
    return ((cmp_start < sel_start + SEL_LEN) & (cmp_start + CMP_LEN > sel_start)).astype(BF16)


def kernel(x, norm_gain, sb_w_qkv, sb_w_o, kv_norm, nsa_w_kv, cmp_pos, cmp_w1, cmp_w2,
           nsa_w_q, nsa_gate_b, nsa_w_o, mlp_w1, mlp_w2, final_norm):
    b, s, d = x.shape
    assert b == 1 and d == N_HEADS * HEAD_DIM
    assert s % 512 == 0 and s >= WINDOW + Q_TILE and s // SEL_LEN >= SEL_TOPK
    n_cmp = (s - CMP_LEN) // CMP_STRIDE + 1
    nch = s // CMP_STRIDE
    n_sel = s // SEL_LEN
    h0 = x[0]
    rope_tabs = _rope_tables(s)

    o_sb = _sb_attention(*_qkv_proj(h0, norm_gain[0, 0], sb_w_qkv[0].astype(BF16)))
    h1 = _attn_out_mlp(o_sb, h0, sb_w_o[0].astype(BF16), norm_gain[0, 1],
                       mlp_w1[0].astype(BF16), mlp_w2[0].astype(BF16))

    cmp_kv, k_slc, v_slc_t, k_win, v_win_t = _kv_proj(h1, kv_norm, nsa_w_kv.astype(BF16), rope_tabs)
    pos_flat = jnp.broadcast_to(cmp_pos.reshape(2, 1, CMP_LEN * HEAD_DIM), (2, 8, CMP_LEN * HEAD_DIM))
    kv_cmp, kv_cmp_t = _cmp_mlp(cmp_kv, pos_flat, cmp_w1.astype(BF16), cmp_w2.astype(BF16), n_cmp)

    n_qk = N_HEADS * HEAD_DIM
    w_q = jnp.concatenate([nsa_w_q[0][:, :n_qk], _group_gate_columns(nsa_w_q[0][:, n_qk:])], axis=1).astype(BF16)
    gate_b = _group_gate_columns(nsa_gate_b[0]).reshape(1, 128)
    q, q_rot, gates = _q_proj(h1, norm_gain[1, 0], w_q, gate_b, rope_tabs)
    o_nsa = _nsa_attention(q, q_rot, kv_cmp[0], kv_cmp_t[1], _overlap_matrix(nch, n_sel).T,
                           k_slc, v_slc_t, k_win, v_win_t, gates)
    out = _attn_out_mlp(o_nsa, h1, nsa_w_o[0].astype(BF16), norm_gain[1, 1],
                        mlp_w1[1].astype(BF16), mlp_w2[1].astype(BF16), final_gain=final_norm)
    return out[None]
```

```python
import functools

import jax
import jax.numpy as jnp
import numpy as np
from jax import lax
from jax.experimental import pallas as pl
from jax.experimental.pallas import tpu as pltpu

HEAD_DIM = 64
N_HEADS = 16
KV_GROUPS = 4
HEADS_PER_GROUP = N_HEADS // KV_GROUPS
CMP_LEN = 32
CMP_STRIDE = 16
CMP_HIDDEN = 256
SEL_LEN = 64
SEL_TOPK = 16
WINDOW = 512
ROPE_THETA = 500000.0
ROT_DIM = HEAD_DIM // 4
Q_TILE = 128
GATE_ROWS = 16
V_ROWS = HEAD_DIM + 16
LOG2E = 1.4426950408889634
CMP_CHUNK = 256
SEL_UNROLL = 4
PROJ_SUB = 256
NORM_EPS = 1e-5
NEG = -1e30
SB_UNDERFLOW = -110.0
SB_NEAR = 256
VMEM_LIMIT = 56 * 1024 * 1024

BF16 = jnp.bfloat16
F32 = jnp.float32


def _params(semantics, vmem=VMEM_LIMIT):
    return pltpu.CompilerParams(dimension_semantics=semantics, vmem_limit_bytes=vmem)


def _rms(x, g):
    return x * lax.rsqrt(jnp.mean(x * x, axis=-1, keepdims=True) + NORM_EPS) * g


def _dot(a, b):
    return jnp.dot(a, b, preferred_element_type=F32)


def _rope128(x, c, s1, s2):
    return x * c + pltpu.roll(x, 128 - ROT_DIM // 2, axis=1) * s1 + pltpu.roll(x, ROT_DIM // 2, axis=1) * s2


def _split3(x):
    hi = x.astype(BF16)
    r = x - hi.astype(F32)
    mid = r.astype(BF16)
    lo = (r - mid.astype(F32)).astype(BF16)
    return hi, mid, lo


def _qkv_kernel(x_ref, g_ref, w_ref, q_ref, kt_ref, v_ref):
    xn = _rms(x_ref[...], g_ref[...]).astype(BF16)
    acc = _dot(xn, w_ref[...])
    d = x_ref.shape[1]
    q_ref[...] = (acc[:, :d] * (HEAD_DIM ** -0.5)).astype(BF16)
    kt_ref[...] = acc[:, d:2 * d].T.astype(BF16)
    v_ref[...] = acc[:, 2 * d:].astype(BF16)


def _qkv_proj(x, gain, w):
    s, d = x.shape
    n = w.shape[1]
    tm = 256
    row = pl.BlockSpec((tm, d), lambda i: (i, 0))
    return pl.pallas_call(
        _qkv_kernel,
        grid=(s // tm,),
        in_specs=[row, pl.BlockSpec((1, d), lambda i: (0, 0)), pl.BlockSpec((d, n), lambda i: (0, 0))],
        out_specs=[row, pl.BlockSpec((d, tm), lambda i: (0, i)), row],
        out_shape=[jax.ShapeDtypeStruct((s, d), BF16), jax.ShapeDtypeStruct((d, s), BF16),
                   jax.ShapeDtypeStruct((s, d), BF16)],
        compiler_params=_params(("arbitrary",)),
        name="qkv_proj",
    )(x, gain.reshape(1, d), w)


def _sb_blocks(qs, kts, vs, tri, carries, mask):
    n = len(qs)
    zs = [_dot(qs[h], kts[h]) for h in range(n)]
    sps = [jnp.maximum(z, 0.0) + jnp.log(1.0 + jnp.exp(-jnp.abs(z))) for z in zs]
    logs = [-sp for sp in sps]
    if mask is not None:
        logs = [jnp.where(mask, x, 0.0) for x in logs]
    excls = [_dot(logs[h].astype(BF16), tri) for h in range(n)]
    ws = [jnp.exp((zs[h] - sps[h]) + (excls[h] + carries[h])) for h in range(n)]
    if mask is not None:
        ws = [jnp.where(mask, w, 0.0) for w in ws]
    outs = [_dot(ws[h].astype(BF16), vs[h]) for h in range(n)]
    new_carries = [carries[h] + jnp.sum(logs[h], axis=1, keepdims=True) for h in range(n)]
    return outs, new_carries


def _sb_kernel(q_ref, kt_ref, v_ref, tri_ref, o_ref):
    i = pl.program_id(1)
    tq = q_ref.shape[0]
    nh = q_ref.shape[1] // HEAD_DIM
    near = tri_ref.shape[0]
    tri_near = tri_ref[...]
    tri = tri_near[:tq, :tq]
    q = q_ref[...]
    head = lambda x, h: x[:, HEAD_DIM * h:HEAD_DIM * (h + 1)]

    start = pl.multiple_of(jnp.maximum(i * tq - (near - tq), 0), tq)
    row = lax.broadcasted_iota(jnp.int32, (tq, near), 0)
    col = lax.broadcasted_iota(jnp.int32, (tq, near), 1)
    causal = col - row < i * tq - start
    kt = kt_ref[:, pl.ds(start, near)]
    v = v_ref[pl.ds(start, near), :]
    qs = [head(q, h) for h in range(nh)]
    split = lambda x: [head(x, h) for h in range(nh)]
    split_t = lambda x: [x[HEAD_DIM * h:HEAD_DIM * (h + 1)] for h in range(nh)]
    outs, carries = _sb_blocks(qs, split_t(kt), split(v), tri_near, [jnp.zeros((tq, 1), F32)] * nh, causal)

    def worst(cs):
        mx = jnp.max(cs[0])
        for c in cs[1:]:
            mx = jnp.maximum(mx, jnp.max(c))
        return mx

    def cond(st):
        return jnp.logical_and(st[0] >= 0, st[1] > SB_UNDERFLOW)

    def body(st):
        kb, _, outs, carries = st
        start = pl.multiple_of(kb * tq, tq)
        kt = kt_ref[:, pl.ds(start, tq)]
        v = v_ref[pl.ds(start, tq), :]
        more, new_c = _sb_blocks(qs, split_t(kt), split(v), tri, list(carries), None)
        return kb - 1, worst(new_c), tuple(outs[h] + more[h] for h in range(nh)), tuple(new_c)

    st = lax.while_loop(cond, body, (start // tq - 1, worst(carries), tuple(outs), tuple(carries)))
    for h in range(nh):
        o_ref[:, HEAD_DIM * h:HEAD_DIM * (h + 1)] = st[2][h].astype(BF16)


def _sb_attention(q, k_t, v):
    s, d = q.shape
    hps = 8
    ngrp = N_HEADS // hps
    w = hps * HEAD_DIM
    near = SB_NEAR + Q_TILE
    idx = jnp.arange(near)
    tri = (idx[:, None] > idx[None, :]).astype(BF16)
    return pl.pallas_call(
        _sb_kernel,
        grid=(ngrp, s // Q_TILE),
        in_specs=[pl.BlockSpec((Q_TILE, w), lambda h, i: (i, h)),
                  pl.BlockSpec((w, s), lambda h, i: (h, 0), pipeline_mode=pl.Buffered(1)),
                  pl.BlockSpec((s, w), lambda h, i: (0, h), pipeline_mode=pl.Buffered(1)),
                  pl.BlockSpec((near, near), lambda h, i: (0, 0))],
        out_specs=pl.BlockSpec((Q_TILE, w), lambda h, i: (i, h)),
        out_shape=jax.ShapeDtypeStruct((s, d), BF16),
        compiler_params=_params(("arbitrary", "arbitrary")),
        name="sb_attention",
    )(q, k_t, v, tri)


def _mlp_kernel(final, tf, o_ref, res_ref, wo_ref, g_ref, w1_ref, w2_ref, *rest):
    out_ref = rest[-1]
    h = res_ref[...] + _dot(o_ref[...], wo_ref[...])
    xn = _rms(h, g_ref[...]).astype(BF16)
    out_ref[...] = h
    for j in range(w1_ref.shape[1] // tf):
        a = jnp.maximum(_dot(xn, w1_ref[:, tf * j:tf * (j + 1)]), 0.0)
        out_ref[...] += _dot((a * a).astype(BF16), w2_ref[tf * j:tf * (j + 1), :])
    if final:
        out_ref[...] = _rms(out_ref[...], rest[0][...])


def _attn_out_mlp(o, resid, wo, gain, w1, w2, final_gain=None):
    s, d = resid.shape
    f = w1.shape[1]
    tm, tf = 512, 512
    final = final_gain is not None
    row = lambda i: (i, 0)
    resident = lambda shape: pl.BlockSpec(shape, lambda i: (0, 0), pipeline_mode=pl.Buffered(1))
    in_specs = [pl.BlockSpec((tm, d), row), pl.BlockSpec((tm, d), row), resident((d, d)), resident((1, d)),
                resident((d, f)), resident((f, d))]
    args = [o, resid, wo, gain.reshape(1, d), w1, w2]
    if final:
        in_specs.append(resident((1, d)))
        args.append(final_gain.reshape(1, d))
    return pl.pallas_call(
        functools.partial(_mlp_kernel, final, tf),
        grid=(s // tm,),
        in_specs=in_specs,
        out_specs=pl.BlockSpec((tm, d), row),
        out_shape=jax.ShapeDtypeStruct((s, d), F32),
        compiler_params=_params(("arbitrary",)),
        name="attn_out_mlp",
    )(*args)


def _row_chunks(tm):
    return [slice(r, r + PROJ_SUB) for r in range(0, tm, PROJ_SUB)]


def _kv_kernel(x_ref, g_ref, w_ref, c_ref, s1_ref, s2_ref, cmp_ref, ks_ref, vs_ref, kw_ref, vw_ref):
    gw = KV_GROUPS * HEAD_DIM
    chunks = _row_chunks(x_ref.shape[0])
    accs = [_dot(_rms(x_ref[rows, :], g_ref[...]).astype(BF16), w_ref[...]) for rows in chunks]
    ones_row = (lax.broadcasted_iota(jnp.int32, (V_ROWS - HEAD_DIM, PROJ_SUB), 0) == 0).astype(BF16)

    def put_k(ref, x, rows):
        c, s1, s2 = c_ref[rows, :], s1_ref[rows, :], s2_ref[rows, :]
        for p in range(gw // 128):
            xp = _rope128(x[:, 128 * p:128 * (p + 1)], c, s1, s2)
            ref[2 * p, rows, :] = xp[:, :HEAD_DIM].astype(BF16)
            ref[2 * p + 1, rows, :] = xp[:, HEAD_DIM:].astype(BF16)

    def put_vt(ref, x, rows):
        xt = x.T
        for g in range(KV_GROUPS):
            ref[g, :HEAD_DIM, rows] = xt[HEAD_DIM * g:HEAD_DIM * (g + 1)].astype(BF16)
            ref[g, HEAD_DIM:, rows] = ones_row

    for rows, acc in zip(chunks, accs):
        cmp_ref[rows, :] = acc[:, :2 * gw]
        put_k(ks_ref, acc[:, 2 * gw:3 * gw], rows)
        put_vt(vs_ref, acc[:, 3 * gw:4 * gw], rows)
        put_k(kw_ref, acc[:, 4 * gw:5 * gw], rows)
        put_vt(vw_ref, acc[:, 5 * gw:6 * gw], rows)


def _kv_proj(h, gain, w, rope_tabs):
    s, d = h.shape
    n = w.shape[1]
    gw = KV_GROUPS * HEAD_DIM
    tm = 2 * PROJ_SUB
    row = lambda i: (i, 0)
    fixed = lambda i: (0, 0)
    k_spec = pl.BlockSpec((KV_GROUPS, tm, HEAD_DIM), lambda i: (0, i, 0))
    k_shape = jax.ShapeDtypeStruct((KV_GROUPS, s, HEAD_DIM), BF16)
    vt_spec = pl.BlockSpec((KV_GROUPS, V_ROWS, tm), lambda i: (0, 0, i))
    vt_shape = jax.ShapeDtypeStruct((KV_GROUPS, V_ROWS, s), BF16)
    return pl.pallas_call(
        _kv_kernel,
        grid=(s // tm,),
        in_specs=[pl.BlockSpec((tm, d), row), pl.BlockSpec((1, d), fixed), pl.BlockSpec((d, n), fixed),
                  pl.BlockSpec((tm, 128), row), pl.BlockSpec((tm, 128), row), pl.BlockSpec((tm, 128), row)],
        out_specs=[pl.BlockSpec((tm, 2 * gw), row), k_spec, vt_spec, k_spec, vt_spec],
        out_shape=[jax.ShapeDtypeStruct((s, 2 * gw), F32), k_shape, vt_shape, k_shape, vt_shape],
        compiler_params=_params(("arbitrary",)),
        name="kv_proj",
    )(h, gain.reshape(1, d), w, *rope_tabs)


def _cmp_kernel(n_cmp, x_ref, pos_ref, w1_ref, w2_ref, o_ref, ot_ref):
    nch = x_ref.shape[0] // CMP_STRIDE
    half = CMP_STRIDE * HEAD_DIM
    w1 = w1_ref[...]
    bias = _dot(pos_ref[...].astype(BF16), w1)[0:1]
    acc = [[jnp.zeros((nch, CMP_HIDDEN), F32) for _ in range(2)] for _ in range(2)]
    for l in range(CMP_STRIDE):
        y = x_ref[pl.ds(l, nch, stride=CMP_STRIDE), :].astype(BF16)
        wa = w1[HEAD_DIM * l:HEAD_DIM * (l + 1)]
        wb = w1[half + HEAD_DIM * l:half + HEAD_DIM * (l + 1)]
        for gg in range(2):
            yg = y[:, HEAD_DIM * gg:HEAD_DIM * (gg + 1)]
            acc[gg][0] = acc[gg][0] + _dot(yg, wa)
            acc[gg][1] = acc[gg][1] + _dot(yg, wb)
    live = lax.broadcasted_iota(jnp.int32, (nch, HEAD_DIM), 0) < n_cmp
    ones_row = (lax.broadcasted_iota(jnp.int32, (V_ROWS - HEAD_DIM, nch), 0) == 0).astype(BF16)
    outs = []
    for gg in range(2):
        hid = acc[gg][0] + pltpu.roll(acc[gg][1], nch - 1, axis=0) + bias
        hid = jax.nn.gelu(hid, approximate=True).astype(BF16)
        outs.append(jnp.where(live, _dot(hid, w2_ref[...]), 0.0))
        o_ref[gg] = outs[gg].astype(BF16)
    out_t = jnp.concatenate(outs, axis=1).T
    for gg in range(2):
        ot_ref[gg, :HEAD_DIM, :] = out_t[HEAD_DIM * gg:HEAD_DIM * (gg + 1)].astype(BF16)
        ot_ref[gg, HEAD_DIM:, :] = ones_row


def _cmp_mlp(cmp_kv, pos_flat, w1, w2, n_cmp):
    s = cmp_kv.shape[0]
    nch = s // CMP_STRIDE
    feat = CMP_LEN * HEAD_DIM
    return pl.pallas_call(
        functools.partial(_cmp_kernel, n_cmp),
        grid=(2, KV_GROUPS // 2),
        in_specs=[pl.BlockSpec((s, 2 * HEAD_DIM), lambda c, p: (0, c * (KV_GROUPS // 2) + p)),
                  pl.BlockSpec((None, 8, feat), lambda c, p: (c, 0, 0)),
                  pl.BlockSpec((None, feat, CMP_HIDDEN), lambda c, p: (c, 0, 0)),
                  pl.BlockSpec((None, CMP_HIDDEN, HEAD_DIM), lambda c, p: (c, 0, 0))],
        out_specs=[pl.BlockSpec((None, 2, nch, HEAD_DIM), lambda c, p: (c, p, 0, 0)),
                   pl.BlockSpec((None, 2, V_ROWS, nch), lambda c, p: (c, p, 0, 0))],
        out_shape=[jax.ShapeDtypeStruct((2, KV_GROUPS, nch, HEAD_DIM), BF16),
                   jax.ShapeDtypeStruct((2, KV_GROUPS, V_ROWS, nch), BF16)],
        compiler_params=_params(("arbitrary", "arbitrary")),
        name="cmp_mlp",
    )(cmp_kv, pos_flat, w1, w2)


def _q_kernel(x_ref, g_ref, w_ref, b_ref, c_ref, s1_ref, s2_ref, q_ref, qr_ref, gate_ref):
    d = q_ref.shape[1]
    chunks = _row_chunks(x_ref.shape[0])
    accs = [_dot(_rms(x_ref[rows, :], g_ref[...]).astype(BF16), w_ref[...]) for rows in chunks]
    for rows, acc in zip(chunks, accs):
        c, s1, s2 = c_ref[rows, :], s1_ref[rows, :], s2_ref[rows, :]
        for p in range(d // 128):
            qp = acc[:, 128 * p:128 * (p + 1)] * (HEAD_DIM ** -0.5 * LOG2E)
            q_ref[rows, 128 * p:128 * (p + 1)] = qp.astype(BF16)
            qr_ref[rows, 128 * p:128 * (p + 1)] = _rope128(qp, c, s1, s2).astype(BF16)
        gates_t = (1.0 / (1.0 + jnp.exp(-(acc[:, d:] + b_ref[...])))).T
        for g in range(KV_GROUPS):
            gate_ref[g, :, rows] = gates_t[GATE_ROWS * g:GATE_ROWS * (g + 1)]


def _q_proj(h, gain, w, gate_b, rope_tabs):
    s, d = h.shape
    n = w.shape[1]
    tm = 2 * PROJ_SUB
    row = lambda i: (i, 0)
    fixed = lambda i: (0, 0)
    return pl.pallas_call(
        _q_kernel,
        grid=(s // tm,),
        in_specs=[pl.BlockSpec((tm, d), row), pl.BlockSpec((1, d), fixed), pl.BlockSpec((d, n), fixed),
                  pl.BlockSpec((1, 128), fixed),
                  pl.BlockSpec((tm, 128), row), pl.BlockSpec((tm, 128), row), pl.BlockSpec((tm, 128), row)],
        out_specs=[pl.BlockSpec((tm, d), row), pl.BlockSpec((tm, d), row),
                   pl.BlockSpec((KV_GROUPS, GATE_ROWS, tm), lambda i: (0, 0, i))],
        out_shape=[jax.ShapeDtypeStruct((s, d), BF16), jax.ShapeDtypeStruct((s, d), BF16),
                   jax.ShapeDtypeStruct((KV_GROUPS, GATE_ROWS, s), F32)],
        compiler_params=_params(("arbitrary",)),
        name="nsa_q_proj",
    )(h, gain.reshape(1, d), w, gate_b, *rope_tabs)


def _group_gate_columns(a):
    lead = a.shape[:-1]
    a = a.reshape(lead + (3, KV_GROUPS, HEADS_PER_GROUP))
    a = jnp.moveaxis(a, -2, -3).reshape(lead + (KV_GROUPS, 3 * HEADS_PER_GROUP))
    a = jnp.pad(a, [(0, 0)] * len(lead) + [(0, 0), (0, GATE_ROWS - 3 * HEADS_PER_GROUP)])
    a = a.reshape(lead + (KV_GROUPS * GATE_ROWS,))
    return jnp.pad(a, [(0, 0)] * len(lead) + [(0, 128 - KV_GROUPS * GATE_ROWS)])


def _stack_heads_t(q):
    qt = q.astype(F32).T
    return jnp.concatenate([qt[HEAD_DIM * h:HEAD_DIM * (h + 1)] for h in range(HEADS_PER_GROUP)], axis=1).astype(BF16)


def _head_cols(x, h):
    return x[:, Q_TILE * h:Q_TILE * (h + 1)]


def _softmax_numerators_t(st, mask):
    es = []
    for h in range(HEADS_PER_GROUP):
        s = mask(_head_cols(st, h))
        es.append(jnp.exp2(s - jnp.max(s, axis=0, keepdims=True)))
    return es


def _nsa_kernel(tk, q_ref, qr_ref, kc_ref, vct_ref, ovt_ref, ks_ref, vst_ref, kw_ref, vwt_ref, wbias_ref, gate_ref,
                o_ref, ocmp_ref, owin_ref, bias_ref, s0_ref, s1_ref, mx0_ref, mx1_ref, m_ref, acc_ref):
    i = pl.program_id(1)
    tq = q_ref.shape[0]
    nc = kc_ref.shape[0]
    n_sel = ovt_ref.shape[0]
    hp = HEADS_PER_GROUP
    q4 = _stack_heads_t(q_ref[...])
    qr4 = _stack_heads_t(qr_ref[...])
    tok = i * tq + lax.broadcasted_iota(jnp.int32, (1, tq), 1)
    gates = gate_ref[...]

    q0 = pl.multiple_of(i * tq, tq)
    span = WINDOW + tq
    wstart = pl.multiple_of(jnp.maximum(i * tq - WINDOW, 0), tq)

    def cmp_and_select(ncv, rows):
        sc = _dot(kc_ref[:ncv, :], q4)
        sw = _dot(kw_ref[pl.ds(wstart, span), :], qr4)
        sd = _dot(ks_ref[pl.ds(q0, tq), :], qr4)
        s0_ref[...] = _dot(ks_ref[:tk, :], qr4)

        n_idx = lax.broadcasted_iota(jnp.int32, (ncv, tq), 0)
        last_complete = lax.shift_right_arithmetic(tok - (CMP_LEN - 1), CMP_STRIDE.bit_length() - 1)
        es = _softmax_numerators_t(sc, lambda s: jnp.where(n_idx <= last_complete, s, NEG))
        wbias = wbias_ref[...]
        ew = _softmax_numerators_t(sw, lambda s: s + wbias)
        not_future = lax.broadcasted_iota(jnp.int32, (tq, tq), 0) <= lax.broadcasted_iota(jnp.int32, (tq, tq), 1)
        sd = jnp.concatenate([jnp.where(not_future, _head_cols(sd, h), NEG) for h in range(hp)], axis=1)
        m0 = jnp.max(sd, axis=0, keepdims=True)
        pd = jnp.exp2(sd - m0)

        o_cmp = _dot(vct_ref[:, :ncv], jnp.concatenate(es, axis=1).astype(BF16))
        o_win = _dot(vwt_ref[:, pl.ds(wstart, span)], jnp.concatenate(ew, axis=1).astype(BF16))
        m_ref[...] = m0
        acc_ref[...] = _dot(vst_ref[:, pl.ds(q0, tq)], pd.astype(BF16))
        inv_c = jnp.where(jnp.concatenate([tok >= CMP_LEN - 1] * hp, axis=1), 1.0 / o_cmp[HEAD_DIM:HEAD_DIM + 1], 0.0)
        ocmp_ref[...] = o_cmp[:HEAD_DIM] * inv_c
        owin_ref[...] = o_win[:HEAD_DIM] * (1.0 / o_win[HEAD_DIM:HEAD_DIM + 1])

        psum = sum(es[h] * _head_cols(inv_c, h) for h in range(hp))
        ovt = ovt_ref[:rows, :ncv]
        imp = sum(_dot(ovt, term) for term in _split3(psum))

        m_idx = lax.broadcasted_iota(jnp.int32, (rows, tq), 0)
        blk_t = tok // SEL_LEN
        forced = (m_idx == 0) | (m_idx == blk_t) | (m_idx == blk_t - 1)
        candidate = jnp.logical_and(m_idx <= blk_t, jnp.logical_not(forced))
        imp = jnp.where(candidate, imp, -jnp.inf)

        def pick(_, v):
            mx = jnp.max(v, axis=0, keepdims=True)
            idx = jnp.min(jnp.where(v == mx, m_idx, rows), axis=0, keepdims=True)
            return jnp.where(m_idx == idx, -jnp.inf, v)

        picked = lax.fori_loop(0, SEL_TOPK - 3, pick, imp, unroll=True) == -jnp.inf
        before = m_idx < (i * tq) // SEL_LEN
        bias_ref[:rows, :] = jnp.where(jnp.logical_and(picked, before), 0.0, NEG)
        if rows < n_sel:
            bias_ref[rows:, :] = jnp.full((n_sel - rows, tq), NEG, F32)

    rows_per_chunk = CMP_CHUNK * CMP_STRIDE // SEL_LEN
    n_prefix = -(-nc // CMP_CHUNK)
    need = ((i + 1) * tq - 1) // (CMP_STRIDE * CMP_CHUNK) + 1
    for c in range(1, n_prefix + 1):
        pl.when(need == c)(functools.partial(cmp_and_select, min(c * CMP_CHUNK, nc), min(c * rows_per_chunk, n_sel)))
    bpt = tk // SEL_LEN

    def sel_bias_and_max(kt, st, s_ref, mx_ref):
        rows = bias_ref[pl.ds(pl.multiple_of(kt * bpt, bpt), bpt), :]
        bias = jnp.concatenate([jnp.broadcast_to(rows[j:j + 1], (SEL_LEN, tq)) for j in range(bpt)], axis=0)
        s = jnp.concatenate([_head_cols(st, h) + bias for h in range(hp)], axis=1)
        s_ref[...] = s
        mx_ref[...] = jnp.max(s, axis=0, keepdims=True)

    def sel_scores(kt, s_ref, mx_ref):
        start = pl.multiple_of(kt * tk, tk)
        sel_bias_and_max(kt, _dot(ks_ref[pl.ds(start, tk), :], qr4), s_ref, mx_ref)

    def sel_accumulate(kt, s_ref, mx_ref):
        start = pl.multiple_of(kt * tk, tk)
        m = m_ref[...]
        m_new = jnp.maximum(m, mx_ref[...])
        p = jnp.exp2(s_ref[...] - m_new)
        acc_ref[...] = jnp.exp2(m - m_new) * acc_ref[...] + _dot(vst_ref[:, pl.ds(start, tk)], p.astype(BF16))
        m_ref[...] = m_new

    last = jnp.maximum(i * tq - 1, 0) // tk
    final_tile = ks_ref.shape[0] // tk - 1

    def sel_pair(j):
        sel_scores(2 * j + 1, s1_ref, mx1_ref)
        sel_accumulate(2 * j, s0_ref, mx0_ref)
        sel_scores(jnp.minimum(2 * j + 2, final_tile), s0_ref, mx0_ref)
        sel_accumulate(2 * j + 1, s1_ref, mx1_ref)

    def sel_pairs(jj, carry):
        for u in range(SEL_UNROLL):
            sel_pair(SEL_UNROLL * jj + u)
        return carry

    sel_bias_and_max(0, s0_ref[...], s0_ref, mx0_ref)
    pairs = (last + 1) // 2
    lax.fori_loop(0, pairs // SEL_UNROLL, sel_pairs, 0)
    done = pairs // SEL_UNROLL * SEL_UNROLL
    run = SEL_UNROLL // 2
    while run >= 1:
        def sel_run(done=done, run=run):
            for u in range(run):
                sel_pair(done + u)

        take = (pairs - done) >= run
        pl.when(take)(sel_run)
        done = done + jnp.where(take, run, 0)
        run //= 2

    @pl.when(last % 2 == 0)
    def _():
        sel_accumulate(last, s0_ref, mx0_ref)

    acc = acc_ref[...]
    o_sel = acc[:HEAD_DIM] * (1.0 / acc[HEAD_DIM:HEAD_DIM + 1])

    o_cmp = ocmp_ref[...]
    o_win = owin_ref[...]
    for h in range(hp):
        mix = (gates[h:h + 1] * _head_cols(o_cmp, h)
               + gates[hp + h:hp + h + 1] * _head_cols(o_sel, h)
               + gates[2 * hp + h:2 * hp + h + 1] * _head_cols(o_win, h))
        o_ref[:, HEAD_DIM * h:HEAD_DIM * (h + 1)] = mix.T.astype(BF16)


def _nsa_attention(q, q_rot, k_cmp, v_cmp_t, overlap_t, k_slc, v_slc_t, k_win, v_win_t, gates):
    s, d = q.shape
    nc = k_cmp.shape[1]
    n_sel = overlap_t.shape[0]
    gw = HEADS_PER_GROUP * HEAD_DIM
    cols = HEADS_PER_GROUP * Q_TILE
    tk = 512
    n_wcase = WINDOW // Q_TILE + 1
    lead = (jnp.arange(n_wcase) * Q_TILE)[:, None, None]
    dist = lead + jnp.arange(Q_TILE)[None, None, :] - jnp.arange(WINDOW + Q_TILE)[None, :, None]
    wbias = jnp.where((dist >= 0) & (dist < WINDOW), 0.0, NEG).astype(F32)
    qspec = pl.BlockSpec((Q_TILE, gw), lambda g, i: (i, g))
    per_group = lambda shape: pl.BlockSpec((None,) + shape, lambda g, i: (g, 0, 0))
    return pl.pallas_call(
        functools.partial(_nsa_kernel, tk),
        grid=(KV_GROUPS, s // Q_TILE),
        in_specs=[qspec, qspec,
                  per_group((nc, HEAD_DIM)), per_group((V_ROWS, nc)),
                  pl.BlockSpec((n_sel, nc), lambda g, i: (0, 0)),
                  per_group((s, HEAD_DIM)), per_group((V_ROWS, s)),
                  per_group((s, HEAD_DIM)), per_group((V_ROWS, s)),
                  pl.BlockSpec((None, WINDOW + Q_TILE, Q_TILE), lambda g, i: (jnp.minimum(i, n_wcase - 1), 0, 0)),
                  pl.BlockSpec((None, GATE_ROWS, Q_TILE), lambda g, i: (g, 0, i))],
        out_specs=qspec,
        out_shape=jax.ShapeDtypeStruct((s, d), BF16),
        scratch_shapes=[pltpu.VMEM((HEAD_DIM, cols), F32), pltpu.VMEM((HEAD_DIM, cols), F32),
                        pltpu.VMEM((n_sel, Q_TILE), F32),
                        pltpu.VMEM((tk, cols), F32), pltpu.VMEM((tk, cols), F32),
                        pltpu.VMEM((1, cols), F32), pltpu.VMEM((1, cols), F32),
                        pltpu.VMEM((1, cols), F32), pltpu.VMEM((V_ROWS, cols), F32)],
        compiler_params=_params(("arbitrary", "arbitrary")),
        name="nsa_attention",
    )(q, q_rot, k_cmp, v_cmp_t, overlap_t, k_slc, v_slc_t, k_win, v_win_t, wbias, gates)


def _rope_tables(s):
    half = ROT_DIM // 2
    inv_freq = (np.float32(ROPE_THETA) ** (-np.arange(half, dtype=np.float32) * np.float32(2.0) / ROT_DIM))
    ang = np.arange(s, dtype=np.float32)[:, None] * inv_freq.astype(np.float32)[None, :]
    cos, sin = np.cos(ang).astype(np.float32), np.sin(ang).astype(np.float32)
    rest = HEAD_DIM - ROT_DIM
    c = np.concatenate([cos, cos, np.ones((s, rest), np.float32)], axis=1)
    s1 = np.concatenate([-sin, np.zeros((s, half + rest), np.float32)], axis=1)
    s2 = np.concatenate([np.zeros((s, half), np.float32), sin, np.zeros((s, rest), np.float32)], axis=1)
    return tuple(jnp.asarray(np.tile(a, (1, 2))) for a in (c, s1, s2))


def _overlap_matrix(nch, n_sel):
    cmp_start = jnp.arange(nch)[:, None] * CMP_STRIDE
    sel_start = jnp.arange(n_sel)[None, :] * SEL_LEN
    return ((cmp_start < sel_start + SEL_LEN) & (cmp_start + CMP_LEN > sel_start)).astype(BF16)


def kernel(x, norm_gain, sb_w_qkv, sb_w_o, kv_norm, nsa_w_kv, cmp_pos, cmp_w1, cmp_w2,
           nsa_w_q, nsa_gate_b, nsa_w_o, mlp_w1, mlp_w2, final_norm):
    b, s, d = x.shape
    assert b == 1 and d == N_HEADS * HEAD_DIM
    assert s % 512 == 0 and s >= WINDOW + Q_TILE and s // SEL_LEN >= SEL_TOPK
    n_cmp = (s - CMP_LEN) // CMP_STRIDE + 1
    nch = s // CMP_STRIDE
    n_sel = s // SEL_LEN
    h0 = x[0]
    rope_tabs = _rope_tables(s)

    o_sb = _sb_attention(*_qkv_proj(h0, norm_gain[0, 0], sb_w_qkv[0].astype(BF16)))
    h1 = _attn_out_mlp(o_sb, h0, sb_w_o[0].astype(BF16), norm_gain[0, 1],
                       mlp_w1[0].astype(BF16), mlp_w2[0].astype(BF16))

    cmp_kv, k_slc, v_slc_t, k_win, v_win_t = _kv_proj(h1, kv_norm, nsa_w_kv.astype(BF16), rope_tabs)
    pos_flat = jnp.broadcast_to(cmp_pos.reshape(2, 1, CMP_LEN * HEAD_DIM), (2, 8, CMP_LEN * HEAD_DIM))
    kv_cmp, kv_cmp_t = _cmp_mlp(cmp_kv, pos_flat, cmp_w1.astype(BF16), cmp_w2.astype(BF16), n_cmp)

    n_qk = N_HEADS * HEAD_DIM
    w_q = jnp.concatenate([nsa_w_q[0][:, :n_qk], _group_gate_columns(nsa_w_q[0][:, n_qk:])], axis=1).astype(BF16)
    gate_b = _group_gate_columns(nsa_gate_b[0]).reshape(1, 128)
    q, q_rot, gates = _q_proj(h1, norm_gain[1, 0], w_q, gate_b, rope_tabs)
    o_nsa = _nsa_attention(q, q_rot, kv_cmp[0], kv_cmp_t[1], _overlap_matrix(nch, n_sel).T,
                           k_slc, v_slc_t, k_win, v_win_t, gates)
    out = _attn_out_mlp(o_nsa, h1, nsa_w_o[0].astype(BF16), norm_gain[1, 1],
                        mlp_w1[1].astype(BF16), mlp_w2[1].astype(BF16), final_gain=final_norm)
    return out[None]
```

```python
import functools

import jax
import jax.numpy as jnp
import numpy as np
from jax import lax
from jax.experimental import pallas as pl
from jax.experimental.pallas import tpu as pltpu

HEAD_DIM = 64
N_HEADS = 16
KV_GROUPS = 4
HEADS_PER_GROUP = N_HEADS // KV_GROUPS
CMP_LEN = 32
CMP_STRIDE = 16
CMP_HIDDEN = 256
SEL_LEN = 64
SEL_TOPK = 16
WINDOW = 512
ROPE_THETA = 500000.0
ROT_DIM = HEAD_DIM // 4
Q_TILE = 128
GATE_ROWS = 16
V_ROWS = HEAD_DIM + 16
LOG2E = 1.4426950408889634
CMP_CHUNK = 256
SEL_UNROLL = 4
PROJ_SUB = 256
NORM_EPS = 1e-5
NEG = -1e30
SB_UNDERFLOW = -110.0
SB_NEAR = 256
VMEM_LIMIT = 56 * 1024 * 1024

BF16 = jnp.bfloat16
F32 = jnp.float32


def _params(semantics, vmem=VMEM_LIMIT):
    return pltpu.CompilerParams(dimension_semantics=semantics, vmem_limit_bytes=vmem)


def _rms(x, g):
    return x * lax.rsqrt(jnp.mean(x * x, axis=-1, keepdims=True) + NORM_EPS) * g


def _dot(a, b):
    return jnp.dot(a, b, preferred_element_type=F32)


def _rope128(x, c, s1, s2):
    return x * c + pltpu.roll(x, 128 - ROT_DIM // 2, axis=1) * s1 + pltpu.roll(x, ROT_DIM // 2, axis=1) * s2


def _split3(x):
    hi = x.astype(BF16)
    r = x - hi.astype(F32)
    mid = r.astype(BF16)
    lo = (r - mid.astype(F32)).astype(BF16)
    return hi, mid, lo


def _qkv_kernel(x_ref, g_ref, w_ref, q_ref, kt_ref, v_ref):
    xn = _rms(x_ref[...], g_ref[...]).astype(BF16)
    acc = _dot(xn, w_ref[...])
    d = x_ref.shape[1]
    q_ref[...] = (acc[:, :d] * (HEAD_DIM ** -0.5)).astype(BF16)
    kt_ref[...] = acc[:, d:2 * d].T.astype(BF16)
    v_ref[...] = acc[:, 2 * d:].astype(BF16)


def _qkv_proj(x, gain, w):
    s, d = x.shape
    n = w.shape[1]
    tm = 256
    row = pl.BlockSpec((tm, d), lambda i: (i, 0))
    return pl.pallas_call(
        _qkv_kernel,
        grid=(s // tm,),
        in_specs=[row, pl.BlockSpec((1, d), lambda i: (0, 0)), pl.BlockSpec((d, n), lambda i: (0, 0))],
        out_specs=[row, pl.BlockSpec((d, tm), lambda i: (0, i)), row],
        out_shape=[jax.ShapeDtypeStruct((s, d), BF16), jax.ShapeDtypeStruct((d, s), BF16),
                   jax.ShapeDtypeStruct((s, d), BF16)],
        compiler_params=_params(("arbitrary",)),
        name="qkv_proj",
    )(x, gain.reshape(1, d), w)


def _sb_blocks(qs, kts, vs, tri, carries, mask):
    n = len(qs)
    zs = [_dot(qs[h], kts[h]) for h in range(n)]
    sps = [jnp.maximum(z, 0.0) + jnp.log(1.0 + jnp.exp(-jnp.abs(z))) for z in zs]
    logs = [-sp for sp in sps]
    if mask is not None:
        logs = [jnp.where(mask, x, 0.0) for x in logs]
    excls = [_dot(logs[h].astype(BF16), tri) for h in range(n)]
    ws = [jnp.exp((zs[h] - sps[h]) + (excls[h] + carries[h])) for h in range(n)]
    if mask is not None:
        ws = [jnp.where(mask, w, 0.0) for w in ws]
    outs = [_dot(ws[h].astype(BF16), vs[h]) for h in range(n)]
    new_carries = [carries[h] + jnp.sum(logs[h], axis=1, keepdims=True) for h in range(n)]
    return outs, new_carries


def _sb_kernel(q_ref, kt_ref, v_ref, tri_ref, o_ref):
    i = pl.program_id(1)
    tq = q_ref.shape[0]
    nh = q_ref.shape[1] // HEAD_DIM
    near = tri_ref.shape[0]
    tri_near = tri_ref[...]
    tri = tri_near[:tq, :tq]
    q = q_ref[...]
    head = lambda x, h: x[:, HEAD_DIM * h:HEAD_DIM * (h + 1)]

    start = pl.multiple_of(jnp.maximum(i * tq - (near - tq), 0), tq)
    row = lax.broadcasted_iota(jnp.int32, (tq, near), 0)
    col = lax.broadcasted_iota(jnp.int32, (tq, near), 1)
    causal = col - row < i * tq - start
    kt = kt_ref[:, pl.ds(start, near)]
    v = v_ref[pl.ds(start, near), :]
    qs = [head(q, h) for h in range(nh)]
    split = lambda x: [head(x, h) for h in range(nh)]
    split_t = lambda x: [x[HEAD_DIM * h:HEAD_DIM * (h + 1)] for h in range(nh)]
    outs, carries = _sb_blocks(qs, split_t(kt), split(v), tri_near, [jnp.zeros((tq, 1), F32)] * nh, causal)

    def worst(cs):
        mx = jnp.max(cs[0])
        for c in cs[1:]:
            mx = jnp.maximum(mx, jnp.max(c))
        return mx

    def cond(st):
        return jnp.logical_and(st[0] >= 0, st[1] > SB_UNDERFLOW)

    def body(st):
        kb, _, outs, carries = st
        start = pl.multiple_of(kb * tq, tq)
        kt = kt_ref[:, pl.ds(start, tq)]
        v = v_ref[pl.ds(start, tq), :]
        more, new_c = _sb_blocks(qs, split_t(kt), split(v), tri, list(carries), None)
        return kb - 1, worst(new_c), tuple(outs[h] + more[h] for h in range(nh)), tuple(new_c)

    st = lax.while_loop(cond, body, (start // tq - 1, worst(carries), tuple(outs), tuple(carries)))
    for h in range(nh):
        o_ref[:, HEAD_DIM * h:HEAD_DIM * (h + 1)] = st[2][h].astype(BF16)


def _sb_attention(q, k_t, v):
    s, d = q.shape
    hps = 8
    ngrp = N_HEADS // hps
    w = hps * HEAD_DIM
    near = SB_NEAR + Q_TILE
    idx = np.arange(near)
    tri = jnp.asarray((idx[:, None] > idx[None, :]).astype(np.float32), dtype=BF16)
    return pl.pallas_call(
        _sb_kernel,
        grid=(ngrp, s // Q_TILE),
        in_specs=[pl.BlockSpec((Q_TILE, w), lambda h, i: (i, h)),
                  pl.BlockSpec((w, s), lambda h, i: (h, 0), pipeline_mode=pl.Buffered(1)),
                  pl.BlockSpec((s, w), lambda h, i: (0, h), pipeline_mode=pl.Buffered(1)),
                  pl.BlockSpec((near, near), lambda h, i: (0, 0))],
        out_specs=pl.BlockSpec((Q_TILE, w), lambda h, i: (i, h)),
        out_shape=jax.ShapeDtypeStruct((s, d), BF16),
        compiler_params=_params(("arbitrary", "arbitrary")),
        name="sb_attention",
    )(q, k_t, v, tri)


def _mlp_kernel(final, tf, o_ref, res_ref, wo_ref, g_ref, w1_ref, w2_ref, *rest):
    out_ref = rest[-1]
    h = res_ref[...] + _dot(o_ref[...], wo_ref[...])
    xn = _rms(h, g_ref[...]).astype(BF16)
    out_ref[...] = h
    for j in range(w1_ref.shape[1] // tf):
        a = jnp.maximum(_dot(xn, w1_ref[:, tf * j:tf * (j + 1)]), 0.0)
        out_ref[...] += _dot((a * a).astype(BF16), w2_ref[tf * j:tf * (j + 1), :])
    if final:
        out_ref[...] = _rms(out_ref[...], rest[0][...])


def _attn_out_mlp(o, resid, wo, gain, w1, w2, final_gain=None):
    s, d = resid.shape
    f = w1.shape[1]
    tm, tf = 512, 512
    final = final_gain is not None
    row = lambda i: (i, 0)
    resident = lambda shape: pl.BlockSpec(shape, lambda i: (0, 0), pipeline_mode=pl.Buffered(1))
    in_specs = [pl.BlockSpec((tm, d), row), pl.BlockSpec((tm, d), row), resident((d, d)), resident((1, d)),
                resident((d, f)), resident((f, d))]
    args = [o, resid, wo, gain.reshape(1, d), w1, w2]
    if final:
        in_specs.append(resident((1, d)))
        args.append(final_gain.reshape(1, d))
    return pl.pallas_call(
        functools.partial(_mlp_kernel, final, tf),
        grid=(s // tm,),
        in_specs=in_specs,
        out_specs=pl.BlockSpec((tm, d), row),
        out_shape=jax.ShapeDtypeStruct((s, d), F32),
        compiler_params=_params(("arbitrary",)),
        name="attn_out_mlp",
    )(*args)


def _row_chunks(tm):
    return [slice(r, r + PROJ_SUB) for r in range(0, tm, PROJ_SUB)]


def _kv_kernel(x_ref, g_ref, w_ref, c_ref, s1_ref, s2_ref, cmp_ref, ks_ref, vs_ref, kw_ref, vw_ref):
    gw = KV_GROUPS * HEAD_DIM
    chunks = _row_chunks(x_ref.shape[0])
    accs = [_dot(_rms(x_ref[rows, :], g_ref[...]).astype(BF16), w_ref[...]) for rows in chunks]
    ones_row = (lax.broadcasted_iota(jnp.int32, (V_ROWS - HEAD_DIM, PROJ_SUB), 0) == 0).astype(BF16)

    def put_k(ref, x, rows):
        c, s1, s2 = c_ref[rows, :], s1_ref[rows, :], s2_ref[rows, :]
        for p in range(gw // 128):
            xp = _rope128(x[:, 128 * p:128 * (p + 1)], c, s1, s2)
            ref[2 * p, rows, :] = xp[:, :HEAD_DIM].astype(BF16)
            ref[2 * p + 1, rows, :] = xp[:, HEAD_DIM:].astype(BF16)

    def put_vt(ref, x, rows):
        xt = x.T
        for g in range(KV_GROUPS):
            ref[g, :HEAD_DIM, rows] = xt[HEAD_DIM * g:HEAD_DIM * (g + 1)].astype(BF16)
            ref[g, HEAD_DIM:, rows] = ones_row

    for rows, acc in zip(chunks, accs):
        cmp_ref[rows, :] = acc[:, :2 * gw]
        put_k(ks_ref, acc[:, 2 * gw:3 * gw], rows)
        put_vt(vs_ref, acc[:, 3 * gw:4 * gw], rows)
        put_k(kw_ref, acc[:, 4 * gw:5 * gw], rows)
        put_vt(vw_ref, acc[:, 5 * gw:6 * gw], rows)


def _kv_proj(h, gain, w, rope_tabs):
    s, d = h.shape
    n = w.shape[1]
    gw = KV_GROUPS * HEAD_DIM
    tm = 2 * PROJ_SUB
    row = lambda i: (i, 0)
    fixed = lambda i: (0, 0)
    k_spec = pl.BlockSpec((KV_GROUPS, tm, HEAD_DIM), lambda i: (0, i, 0))
    k_shape = jax.ShapeDtypeStruct((KV_GROUPS, s, HEAD_DIM), BF16)
    vt_spec = pl.BlockSpec((KV_GROUPS, V_ROWS, tm), lambda i: (0, 0, i))
    vt_shape = jax.ShapeDtypeStruct((KV_GROUPS, V_ROWS, s), BF16)
    return pl.pallas_call(
        _kv_kernel,
        grid=(s // tm,),
        in_specs=[pl.BlockSpec((tm, d), row), pl.BlockSpec((1, d), fixed), pl.BlockSpec((d, n), fixed),
                  pl.BlockSpec((tm, 128), row), pl.BlockSpec((tm, 128), row), pl.BlockSpec((tm, 128), row)],
        out_specs=[pl.BlockSpec((tm, 2 * gw), row), k_spec, vt_spec, k_spec, vt_spec],
        out_shape=[jax.ShapeDtypeStruct((s, 2 * gw), F32), k_shape, vt_shape, k_shape, vt_shape],
        compiler_params=_params(("arbitrary",)),
        name="kv_proj",
    )(h, gain.reshape(1, d), w, *rope_tabs)


def _cmp_kernel(n_cmp, x_ref, pos_ref, w1_ref, w2_ref, o_ref, ot_ref):
    nch = x_ref.shape[0] // CMP_STRIDE
    half = CMP_STRIDE * HEAD_DIM
    w1 = w1_ref[...]
    bias = _dot(pos_ref[...].astype(BF16), w1)[0:1]
    acc = [[jnp.zeros((nch, CMP_HIDDEN), F32) for _ in range(2)] for _ in range(2)]
    for l in range(CMP_STRIDE):
        y = x_ref[pl.ds(l, nch, stride=CMP_STRIDE), :].astype(BF16)
        wa = w1[HEAD_DIM * l:HEAD_DIM * (l + 1)]
        wb = w1[half + HEAD_DIM * l:half + HEAD_DIM * (l + 1)]
        for gg in range(2):
            yg = y[:, HEAD_DIM * gg:HEAD_DIM * (gg + 1)]
            acc[gg][0] = acc[gg][0] + _dot(yg, wa)
            acc[gg][1] = acc[gg][1] + _dot(yg, wb)
    live = lax.broadcasted_iota(jnp.int32, (nch, HEAD_DIM), 0) < n_cmp
    ones_row = (lax.broadcasted_iota(jnp.int32, (V_ROWS - HEAD_DIM, nch), 0) == 0).astype(BF16)
    outs = []
    for gg in range(2):
        hid = acc[gg][0] + pltpu.roll(acc[gg][1], nch - 1, axis=0) + bias
        hid = jax.nn.gelu(hid, approximate=True).astype(BF16)
        outs.append(jnp.where(live, _dot(hid, w2_ref[...]), 0.0))
        o_ref[gg] = outs[gg].astype(BF16)
    out_t = jnp.concatenate(outs, axis=1).T
    for gg in range(2):
        ot_ref[gg, :HEAD_DIM, :] = out_t[HEAD_DIM * gg:HEAD_DIM * (gg + 1)].astype(BF16)
        ot_ref[gg, HEAD_DIM:, :] = ones_row


def _cmp_mlp(cmp_kv, pos_flat, w1, w2, n_cmp):
    s = cmp_kv.shape[0]
    nch = s // CMP_STRIDE
    feat = CMP_LEN * HEAD_DIM
    return pl.pallas_call(
        functools.partial(_cmp_kernel, n_cmp),
        grid=(2, KV_GROUPS // 2),
        in_specs=[pl.BlockSpec((s, 2 * HEAD_DIM), lambda c, p: (0, c * (KV_GROUPS // 2) + p)),
                  pl.BlockSpec((None, 8, feat), lambda c, p: (c, 0, 0)),
                  pl.BlockSpec((None, feat, CMP_HIDDEN), lambda c, p: (c, 0, 0)),
                  pl.BlockSpec((None, CMP_HIDDEN, HEAD_DIM), lambda c, p: (c, 0, 0))],
        out_specs=[pl.BlockSpec((None, 2, nch, HEAD_DIM), lambda c, p: (c, p, 0, 0)),
                   pl.BlockSpec((None, 2, V_ROWS, nch), lambda c, p: (c, p, 0, 0))],
        out_shape=[jax.ShapeDtypeStruct((2, KV_GROUPS, nch, HEAD_DIM), BF16),
                   jax.ShapeDtypeStruct((2, KV_GROUPS, V_ROWS, nch), BF16)],
        compiler_params=_params(("arbitrary", "arbitrary")),
        name="cmp_mlp",
    )(cmp_kv, pos_flat, w1, w2)


def _q_kernel(x_ref, g_ref, w_ref, b_ref, c_ref, s1_ref, s2_ref, q_ref, qr_ref, gate_ref):
    d = q_ref.shape[1]
    chunks = _row_chunks(x_ref.shape[0])
    accs = [_dot(_rms(x_ref[rows, :], g_ref[...]).astype(BF16), w_ref[...]) for rows in chunks]
    for rows, acc in zip(chunks, accs):
        c, s1, s2 = c_ref[rows, :], s1_ref[rows, :], s2_ref[rows, :]
        for p in range(d // 128):
            qp = acc[:, 128 * p:128 * (p + 1)] * (HEAD_DIM ** -0.5 * LOG2E)
            q_ref[rows, 128 * p:128 * (p + 1)] = qp.astype(BF16)
            qr_ref[rows, 128 * p:128 * (p + 1)] = _rope128(qp, c, s1, s2).astype(BF16)
        gates_t = (1.0 / (1.0 + jnp.exp(-(acc[:, d:] + b_ref[...])))).T
        for g in range(KV_GROUPS):
            gate_ref[g, :, rows] = gates_t[GATE_ROWS * g:GATE_ROWS * (g + 1)]


def _q_proj(h, gain, w, gate_b, rope_tabs):
    s, d = h.shape
    n = w.shape[1]
    tm = 2 * PROJ_SUB
    row = lambda i: (i, 0)
    fixed = lambda i: (0, 0)
    return pl.pallas_call(
        _q_kernel,
        grid=(s // tm,),
        in_specs=[pl.BlockSpec((tm, d), row), pl.BlockSpec((1, d), fixed), pl.BlockSpec((d, n), fixed),
                  pl.BlockSpec((1, 128), fixed),
                  pl.BlockSpec((tm, 128), row), pl.BlockSpec((tm, 128), row), pl.BlockSpec((tm, 128), row)],
        out_specs=[pl.BlockSpec((tm, d), row), pl.BlockSpec((tm, d), row),
                   pl.BlockSpec((KV_GROUPS, GATE_ROWS, tm), lambda i: (0, 0, i))],
        out_shape=[jax.ShapeDtypeStruct((s, d), BF16), jax.ShapeDtypeStruct((s, d), BF16),
                   jax.ShapeDtypeStruct((KV_GROUPS, GATE_ROWS, s), F32)],
        compiler_params=_params(("arbitrary",)),
        name="nsa_q_proj",
    )(h, gain.reshape(1, d), w, gate_b, *rope_tabs)


def _group_gate_columns(a):
    lead = a.shape[:-1]
    a = a.reshape(lead + (3, KV_GROUPS, HEADS_PER_GROUP))
    a = jnp.moveaxis(a, -2, -3).reshape(lead + (KV_GROUPS, 3 * HEADS_PER_GROUP))
    a = jnp.pad(a, [(0, 0)] * len(lead) + [(0, 0), (0, GATE_ROWS - 3 * HEADS_PER_GROUP)])
    a = a.reshape(lead + (KV_GROUPS * GATE_ROWS,))
    return jnp.pad(a, [(0, 0)] * len(lead) + [(0, 128 - KV_GROUPS * GATE_ROWS)])


def _stack_heads_t(q):
    qt = q.astype(F32).T
    return jnp.concatenate([qt[HEAD_DIM * h:HEAD_DIM * (h + 1)] for h in range(HEADS_PER_GROUP)], axis=1).astype(BF16)


def _head_cols(x, h):
    return x[:, Q_TILE * h:Q_TILE * (h + 1)]


def _softmax_numerators_t(st, mask):
    es = []
    for h in range(HEADS_PER_GROUP):
        s = mask(_head_cols(st, h))
        es.append(jnp.exp2(s - jnp.max(s, axis=0, keepdims=True)))
    return es


def _nsa_kernel(tk, q_ref, qr_ref, kc_ref, vct_ref, ovt_ref, ks_ref, vst_ref, kw_ref, vwt_ref, wbias_ref, gate_ref,
                o_ref, ocmp_ref, owin_ref, bias_ref, s0_ref, s1_ref, mx0_ref, mx1_ref, m_ref, acc_ref):
    i = pl.program_id(1)
    tq = q_ref.shape[0]
    nc = kc_ref.shape[0]
    n_sel = ovt_ref.shape[0]
    hp = HEADS_PER_GROUP
    q4 = _stack_heads_t(q_ref[...])
    qr4 = _stack_heads_t(qr_ref[...])
    tok = i * tq + lax.broadcasted_iota(jnp.int32, (1, tq), 1)
    gates = gate_ref[...]

    q0 = pl.multiple_of(i * tq, tq)
    span = WINDOW + tq
    wstart = pl.multiple_of(jnp.maximum(i * tq - WINDOW, 0), tq)

    def cmp_and_select(ncv, rows):
        sc = _dot(kc_ref[:ncv, :], q4)
        sw = _dot(kw_ref[pl.ds(wstart, span), :], qr4)
        sd = _dot(ks_ref[pl.ds(q0, tq), :], qr4)
        s0_ref[...] = _dot(ks_ref[:tk, :], qr4)

        n_idx = lax.broadcasted_iota(jnp.int32, (ncv, tq), 0)
        last_complete = lax.shift_right_arithmetic(tok - (CMP_LEN - 1), CMP_STRIDE.bit_length() - 1)
        es = _softmax_numerators_t(sc, lambda s: jnp.where(n_idx <= last_complete, s, NEG))
        wbias = wbias_ref[...]
        ew = _softmax_numerators_t(sw, lambda s: s + wbias)
        not_future = lax.broadcasted_iota(jnp.int32, (tq, tq), 0) <= lax.broadcasted_iota(jnp.int32, (tq, tq), 1)
        sd = jnp.concatenate([jnp.where(not_future, _head_cols(sd, h), NEG) for h in range(hp)], axis=1)
        m0 = jnp.max(sd, axis=0, keepdims=True)
        pd = jnp.exp2(sd - m0)

        o_cmp = _dot(vct_ref[:, :ncv], jnp.concatenate(es, axis=1).astype(BF16))
        o_win = _dot(vwt_ref[:, pl.ds(wstart, span)], jnp.concatenate(ew, axis=1).astype(BF16))
        m_ref[...] = m0
        acc_ref[...] = _dot(vst_ref[:, pl.ds(q0, tq)], pd.astype(BF16))
        inv_c = jnp.where(jnp.concatenate([tok >= CMP_LEN - 1] * hp, axis=1), 1.0 / o_cmp[HEAD_DIM:HEAD_DIM + 1], 0.0)
        ocmp_ref[...] = o_cmp[:HEAD_DIM] * inv_c
        owin_ref[...] = o_win[:HEAD_DIM] * (1.0 / o_win[HEAD_DIM:HEAD_DIM + 1])

        psum = sum(es[h] * _head_cols(inv_c, h) for h in range(hp))
        ovt = ovt_ref[:rows, :ncv]
        imp = sum(_dot(ovt, term) for term in _split3(psum))

        m_idx = lax.broadcasted_iota(jnp.int32, (rows, tq), 0)
        blk_t = tok // SEL_LEN
        forced = (m_idx == 0) | (m_idx == blk_t) | (m_idx == blk_t - 1)
        candidate = jnp.logical_and(m_idx <= blk_t, jnp.logical_not(forced))
        imp = jnp.where(candidate, imp, -jnp.inf)

        def pick(_, v):
            mx = jnp.max(v, axis=0, keepdims=True)
            idx = jnp.min(jnp.where(v == mx, m_idx, rows), axis=0, keepdims=True)
            return jnp.where(m_idx == idx, -jnp.inf, v)

        picked = lax.fori_loop(0, SEL_TOPK - 3, pick, imp, unroll=True) == -jnp.inf
        before = m_idx < (i * tq) // SEL_LEN
        bias_ref[:rows, :] = jnp.where(jnp.logical_and(picked, before), 0.0, NEG)
        if rows < n_sel:
            bias_ref[rows:, :] = jnp.full((n_sel - rows, tq), NEG, F32)

    rows_per_chunk = CMP_CHUNK * CMP_STRIDE // SEL_LEN
    n_prefix = -(-nc // CMP_CHUNK)
    need = ((i + 1) * tq - 1) // (CMP_STRIDE * CMP_CHUNK) + 1
    for c in range(1, n_prefix + 1):
        pl.when(need == c)(functools.partial(cmp_and_select, min(c * CMP_CHUNK, nc), min(c * rows_per_chunk, n_sel)))
    bpt = tk // SEL_LEN

    def sel_bias_and_max(kt, st, s_ref, mx_ref):
        rows = bias_ref[pl.ds(pl.multiple_of(kt * bpt, bpt), bpt), :]
        bias = jnp.concatenate([jnp.broadcast_to(rows[j:j + 1], (SEL_LEN, tq)) for j in range(bpt)], axis=0)
        s = jnp.concatenate([_head_cols(st, h) + bias for h in range(hp)], axis=1)
        s_ref[...] = s
        mx_ref[...] = jnp.max(s, axis=0, keepdims=True)

    def sel_scores(kt, s_ref, mx_ref):
        start = pl.multiple_of(kt * tk, tk)
        sel_bias_and_max(kt, _dot(ks_ref[pl.ds(start, tk), :], qr4), s_ref, mx_ref)

    def sel_accumulate(kt, s_ref, mx_ref):
        start = pl.multiple_of(kt * tk, tk)
        m = m_ref[...]
        m_new = jnp.maximum(m, mx_ref[...])
        p = jnp.exp2(s_ref[...] - m_new)
        acc_ref[...] = jnp.exp2(m - m_new) * acc_ref[...] + _dot(vst_ref[:, pl.ds(start, tk)], p.astype(BF16))
        m_ref[...] = m_new

    last = jnp.maximum(i * tq - 1, 0) // tk
    final_tile = ks_ref.shape[0] // tk - 1

    def sel_pair(j):
        sel_scores(2 * j + 1, s1_ref, mx1_ref)
        sel_accumulate(2 * j, s0_ref, mx0_ref)
        sel_scores(jnp.minimum(2 * j + 2, final_tile), s0_ref, mx0_ref)
        sel_accumulate(2 * j + 1, s1_ref, mx1_ref)

    def sel_pairs(jj, carry):
        for u in range(SEL_UNROLL):
            sel_pair(SEL_UNROLL * jj + u)
        return carry

    sel_bias_and_max(0, s0_ref[...], s0_ref, mx0_ref)
    pairs = (last + 1) // 2
    lax.fori_loop(0, pairs // SEL_UNROLL, sel_pairs, 0)
    done = pairs // SEL_UNROLL * SEL_UNROLL
    run = SEL_UNROLL // 2
    while run >= 1:
        def sel_run(done=done, run=run):
            for u in range(run):
                sel_pair(done + u)

        take = (pairs - done) >= run
        pl.when(take)(sel_run)
        done = done + jnp.where(take, run, 0)
        run //= 2

    @pl.when(last % 2 == 0)
    def _():
        sel_accumulate(last, s0_ref, mx0_ref)

    acc = acc_ref[...]
    o_sel = acc[:HEAD_DIM] * (1.0 / acc[HEAD_DIM:HEAD_DIM + 1])

    o_cmp = ocmp_ref[...]
    o_win = owin_ref[...]
    for h in range(hp):
        mix = (gates[h:h + 1] * _head_cols(o_cmp, h)
               + gates[hp + h:hp + h + 1] * _head_cols(o_sel, h)
               + gates[2 * hp + h:2 * hp + h + 1] * _head_cols(o_win, h))
        o_ref[:, HEAD_DIM * h:HEAD_DIM * (h + 1)] = mix.T.astype(BF16)


def _nsa_attention(q, q_rot, k_cmp, v_cmp_t, overlap_t, k_slc, v_slc_t, k_win, v_win_t, gates):
    s, d = q.shape
    nc = k_cmp.shape[1]
    n_sel = overlap_t.shape[0]
    gw = HEADS_PER_GROUP * HEAD_DIM
    cols = HEADS_PER_GROUP * Q_TILE
    tk = 512
    n_wcase = WINDOW // Q_TILE + 1
    lead = (np.arange(n_wcase) * Q_TILE)[:, None, None]
    dist = lead + np.arange(Q_TILE)[None, None, :] - np.arange(WINDOW + Q_TILE)[None, :, None]
    wbias = jnp.asarray(np.where((dist >= 0) & (dist < WINDOW), 0.0, NEG).astype(np.float32))
    qspec = pl.BlockSpec((Q_TILE, gw), lambda g, i: (i, g))
    per_group = lambda shape: pl.BlockSpec((None,) + shape, lambda g, i: (g, 0, 0))
    return pl.pallas_call(
        functools.partial(_nsa_kernel, tk),
        grid=(KV_GROUPS, s // Q_TILE),
        in_specs=[qspec, qspec,
                  per_group((nc, HEAD_DIM)), per_group((V_ROWS, nc)),
                  pl.BlockSpec((n_sel, nc), lambda g, i: (0, 0)),
                  per_group((s, HEAD_DIM)), per_group((V_ROWS, s)),
                  per_group((s, HEAD_DIM)), per_group((V_ROWS, s)),
                  pl.BlockSpec((None, WINDOW + Q_TILE, Q_TILE), lambda g, i: (jnp.minimum(i, n_wcase - 1), 0, 0)),
                  pl.BlockSpec((None, GATE_ROWS, Q_TILE), lambda g, i: (g, 0, i))],
        out_specs=qspec,
        out_shape=jax.ShapeDtypeStruct((s, d), BF16),
        scratch_shapes=[pltpu.VMEM((HEAD_DIM, cols), F32), pltpu.VMEM((HEAD_DIM, cols), F32),
                        pltpu.VMEM((n_sel, Q_TILE), F32),
                        pltpu.VMEM((tk, cols), F32), pltpu.VMEM((tk, cols), F32),
                        pltpu.VMEM((1, cols), F32), pltpu.VMEM((1, cols), F32),
                        pltpu.VMEM((1, cols), F32), pltpu.VMEM((V_ROWS, cols), F32)],
        compiler_params=_params(("arbitrary", "arbitrary")),
        name="nsa_attention",
    )(q, q_rot, k_cmp, v_cmp_t, overlap_t, k_slc, v_slc_t, k_win, v_win_t, wbias, gates)


def _rope_tables(s):
    half = ROT_DIM // 2
    inv_freq = (np.float32(ROPE_THETA) ** (-np.arange(half, dtype=np.float32) * np.float32(2.0) / ROT_DIM))
    ang = np.arange(s, dtype=np.float32)[:, None] * inv_freq.astype(np.float32)[None, :]
    cos, sin = np.cos(ang).astype(np.float32), np.sin(ang).astype(np.float32)
    rest = HEAD_DIM - ROT_DIM
    c = np.concatenate([cos, cos, np.ones((s, rest), np.float32)], axis=1)
    s1 = np.concatenate([-sin, np.zeros((s, half + rest), np.float32)], axis=1)
    s2 = np.concatenate([np.zeros((s, half), np.float32), sin, np.zeros((s, rest), np.float32)], axis=1)
    return tuple(jnp.asarray(np.tile(a, (1, 2))) for a in (c, s1, s2))


def _overlap_matrix(nch, n_sel):
    cmp_start = np.arange(nch)[:, None] * CMP_STRIDE
    sel_start = np.arange(n_sel)[None, :] * SEL_LEN
    overlap = (cmp_start < sel_start + SEL_LEN) & (cmp_start + CMP_LEN > sel_start)
    return jnp.asarray(overlap.astype(np.float32), dtype=BF16)


def kernel(x, norm_gain, sb_w_qkv, sb_w_o, kv_norm, nsa_w_kv, cmp_pos, cmp_w1, cmp_w2,
           nsa_w_q, nsa_gate_b, nsa_w_o, mlp_w1, mlp_w2, final_norm):
    b, s, d = x.shape
    assert b == 1 and d == N_HEADS * HEAD_DIM
    assert s % 512 == 0 and s >= WINDOW + Q_TILE and s // SEL_LEN >= SEL_TOPK
    n_cmp = (s - CMP_LEN) // CMP_STRIDE + 1
    nch = s // CMP_STRIDE
    n_sel = s // SEL_LEN
    h0 = x[0]
    rope_tabs = _rope_tables(s)

    o_sb = _sb_attention(*_qkv_proj(h0, norm_gain[0, 0], sb_w_qkv[0].astype(BF16)))
    h1 = _attn_out_mlp(o_sb, h0, sb_w_o[0].astype(BF16), norm_gain[0, 1],
                       mlp_w1[0].astype(BF16), mlp_w2[0].astype(BF16))

    cmp_kv, k_slc, v_slc_t, k_win, v_win_t = _kv_proj(h1, kv_norm, nsa_w_kv.astype(BF16), rope_tabs)
    pos_flat = jnp.broadcast_to(cmp_pos.reshape(2, 1, CMP_LEN * HEAD_DIM), (2, 8, CMP_LEN * HEAD_DIM))
    kv_cmp, kv_cmp_t = _cmp_mlp(cmp_kv, pos_flat, cmp_w1.astype(BF16), cmp_w2.astype(BF16), n_cmp)

    n_qk = N_HEADS * HEAD_DIM
    w_q = jnp.concatenate([nsa_w_q[0][:, :n_qk], _group_gate_columns(nsa_w_q[0][:, n_qk:])], axis=1).astype(BF16)
    gate_b = _group_gate_columns(nsa_gate_b[0]).reshape(1, 128)
    q, q_rot, gates = _q_proj(h1, norm_gain[1, 0], w_q, gate_b, rope_tabs)
    o_nsa = _nsa_attention(q, q_rot, kv_cmp[0], kv_cmp_t[1], _overlap_matrix(nch, n_sel).T,
                           k_slc, v_slc_t, k_win, v_win_t, gates)
    out = _attn_out_mlp(o_nsa, h1, nsa_w_o[0].astype(BF16), norm_gain[1, 1],
                        mlp_w1[1].astype(BF16), mlp_w2[1].astype(BF16), final_gain=final_norm)
    return out[None]
```

```python
import functools

import jax
import jax.numpy as jnp
import numpy as np
from jax import lax
from jax.experimental import pallas as pl
from jax.experimental.pallas import tpu as pltpu

HEAD_DIM = 64
N_HEADS = 16
KV_GROUPS = 4
HEADS_PER_GROUP = N_HEADS // KV_GROUPS
CMP_LEN = 32
CMP_STRIDE = 16
CMP_HIDDEN = 256
SEL_LEN = 64
SEL_TOPK = 16
WINDOW = 512
ROPE_THETA = 500000.0
ROT_DIM = HEAD_DIM // 4
Q_TILE = 128
GATE_ROWS = 16
V_ROWS = HEAD_DIM + 16
LOG2E = 1.4426950408889634
CMP_CHUNK = 256
SEL_UNROLL = 4
PROJ_SUB = 256
NORM_EPS = 1e-5
NEG = -1e30
SB_UNDERFLOW = -110.0
SB_NEAR = 256
VMEM_LIMIT = 56 * 1024 * 1024

BF16 = jnp.bfloat16
F32 = jnp.float32


def _params(semantics, vmem=VMEM_LIMIT):
    return pltpu.CompilerParams(dimension_semantics=semantics, vmem_limit_bytes=vmem)


def _rms(x, g):
    return x * lax.rsqrt(jnp.mean(x * x, axis=-1, keepdims=True) + NORM_EPS) * g


def _dot(a, b):
    return jnp.dot(a, b, preferred_element_type=F32)


def _rope128(x, c, s1, s2):
    return x * c + pltpu.roll(x, 128 - ROT_DIM // 2, axis=1) * s1 + pltpu.roll(x, ROT_DIM // 2, axis=1) * s2


def _split3(x):
    hi = x.astype(BF16)
    r = x - hi.astype(F32)
    mid = r.astype(BF16)
    lo = (r - mid.astype(F32)).astype(BF16)
    return hi, mid, lo


def _qkv_kernel(x_ref, g_ref, w_ref, q_ref, kt_ref, v_ref):
    xn = _rms(x_ref[...], g_ref[...]).astype(BF16)
    acc = _dot(xn, w_ref[...])
    d = x_ref.shape[1]
    q_ref[...] = (acc[:, :d] * (HEAD_DIM ** -0.5)).astype(BF16)
    kt_ref[...] = acc[:, d:2 * d].T.astype(BF16)
    v_ref[...] = acc[:, 2 * d:].astype(BF16)


def _qkv_proj(x, gain, w):
    s, d = x.shape
    n = w.shape[1]
    tm = 256
    row = pl.BlockSpec((tm, d), lambda i: (i, 0))
    return pl.pallas_call(
        _qkv_kernel,
        grid=(s // tm,),
        in_specs=[row, pl.BlockSpec((1, d), lambda i: (0, 0)), pl.BlockSpec((d, n), lambda i: (0, 0))],
        out_specs=[row, pl.BlockSpec((d, tm), lambda i: (0, i)), row],
        out_shape=[jax.ShapeDtypeStruct((s, d), BF16), jax.ShapeDtypeStruct((d, s), BF16),
                   jax.ShapeDtypeStruct((s, d), BF16)],
        compiler_params=_params(("arbitrary",)),
        name="qkv_proj",
    )(x, gain.reshape(1, d), w)


def _sb_blocks(qs, kts, vs, tri, carries, mask):
    n = len(qs)
    zs = [_dot(qs[h], kts[h]) for h in range(n)]
    sps = [jnp.maximum(z, 0.0) + jnp.log(1.0 + jnp.exp(-jnp.abs(z))) for z in zs]
    logs = [-sp for sp in sps]
    if mask is not None:
        logs = [jnp.where(mask, x, 0.0) for x in logs]
    excls = [_dot(logs[h].astype(BF16), tri) for h in range(n)]
    ws = [jnp.exp((zs[h] - sps[h]) + (excls[h] + carries[h])) for h in range(n)]
    if mask is not None:
        ws = [jnp.where(mask, w, 0.0) for w in ws]
    outs = [_dot(ws[h].astype(BF16), vs[h]) for h in range(n)]
    new_carries = [carries[h] + jnp.sum(logs[h], axis=1, keepdims=True) for h in range(n)]
    return outs, new_carries


def _sb_kernel(q_ref, kt_ref, v_ref, tri_ref, o_ref):
    i = pl.program_id(1)
    tq = q_ref.shape[0]
    nh = q_ref.shape[1] // HEAD_DIM
    near = tri_ref.shape[0]
    tri_near = tri_ref[...]
    tri = tri_near[:tq, :tq]
    q = q_ref[...]
    head = lambda x, h: x[:, HEAD_DIM * h:HEAD_DIM * (h + 1)]

    start = pl.multiple_of(jnp.maximum(i * tq - (near - tq), 0), tq)
    row = lax.broadcasted_iota(jnp.int32, (tq, near), 0)
    col = lax.broadcasted_iota(jnp.int32, (tq, near), 1)
    causal = col - row < i * tq - start
    kt = kt_ref[:, pl.ds(start, near)]
    v = v_ref[pl.ds(start, near), :]
    qs = [head(q, h) for h in range(nh)]
    split = lambda x: [head(x, h) for h in range(nh)]
    split_t = lambda x: [x[HEAD_DIM * h:HEAD_DIM * (h + 1)] for h in range(nh)]
    outs, carries = _sb_blocks(qs, split_t(kt), split(v), tri_near, [jnp.zeros((tq, 1), F32)] * nh, causal)

    def worst(cs):
        mx = jnp.max(cs[0])
        for c in cs[1:]:
            mx = jnp.maximum(mx, jnp.max(c))
        return mx

    def cond(st):
        return jnp.logical_and(st[0] >= 0, st[1] > SB_UNDERFLOW)

    def body(st):
        kb, _, outs, carries = st
        start = pl.multiple_of(kb * tq, tq)
        kt = kt_ref[:, pl.ds(start, tq)]
        v = v_ref[pl.ds(start, tq), :]
        more, new_c = _sb_blocks(qs, split_t(kt), split(v), tri, list(carries), None)
        return kb - 1, worst(new_c), tuple(outs[h] + more[h] for h in range(nh)), tuple(new_c)

    st = lax.while_loop(cond, body, (start // tq - 1, worst(carries), tuple(outs), tuple(carries)))
    for h in range(nh):
        o_ref[:, HEAD_DIM * h:HEAD_DIM * (h + 1)] = st[2][h].astype(BF16)


def _sb_attention(q, k_t, v):
    s, d = q.shape
    hps = 8
    ngrp = N_HEADS // hps
    w = hps * HEAD_DIM
    near = SB_NEAR + Q_TILE
    idx = jnp.arange(near)
    tri = (idx[:, None] > idx[None, :]).astype(BF16)
    return pl.pallas_call(
        _sb_kernel,
        grid=(ngrp, s // Q_TILE),
        in_specs=[pl.BlockSpec((Q_TILE, w), lambda h, i: (i, h)),
                  pl.BlockSpec((w, s), lambda h, i: (h, 0), pipeline_mode=pl.Buffered(1)),
                  pl.BlockSpec((s, w), lambda h, i: (0, h), pipeline_mode=pl.Buffered(1)),
                  pl.BlockSpec((near, near), lambda h, i: (0, 0))],
        out_specs=pl.BlockSpec((Q_TILE, w), lambda h, i: (i, h)),
        out_shape=jax.ShapeDtypeStruct((s, d), BF16),
        compiler_params=_params(("arbitrary", "arbitrary")),
        name="sb_attention",
    )(q, k_t, v, tri)


def _mlp_kernel(final, tf, o_ref, res_ref, wo_ref, g_ref, w1_ref, w2_ref, *rest):
    out_ref = rest[-1]
    h = res_ref[...] + _dot(o_ref[...], wo_ref[...])
    xn = _rms(h, g_ref[...]).astype(BF16)
    out_ref[...] = h
    for j in range(w1_ref.shape[1] // tf):
        a = jnp.maximum(_dot(xn, w1_ref[:, tf * j:tf * (j + 1)]), 0.0)
        out_ref[...] += _dot((a * a).astype(BF16), w2_ref[tf * j:tf * (j + 1), :])
    if final:
        out_ref[...] = _rms(out_ref[...], rest[0][...])


def _attn_out_mlp(o, resid, wo, gain, w1, w2, final_gain=None):
    s, d = resid.shape
    f = w1.shape[1]
    tm, tf = 512, 512
    final = final_gain is not None
    row = lambda i: (i, 0)
    resident = lambda shape: pl.BlockSpec(shape, lambda i: (0, 0), pipeline_mode=pl.Buffered(1))
    in_specs = [pl.BlockSpec((tm, d), row), pl.BlockSpec((tm, d), row), resident((d, d)), resident((1, d)),
                resident((d, f)), resident((f, d))]
    args = [o, resid, wo, gain.reshape(1, d), w1, w2]
    if final:
        in_specs.append(resident((1, d)))
        args.append(final_gain.reshape(1, d))
    return pl.pallas_call(
        functools.partial(_mlp_kernel, final, tf),
        grid=(s // tm,),
        in_specs=in_specs,
        out_specs=pl.BlockSpec((tm, d), row),
        out_shape=jax.ShapeDtypeStruct((s, d), F32),
        compiler_params=_params(("arbitrary",)),
        name="attn_out_mlp",
    )(*args)


def _row_chunks(tm):
    return [slice(r, r + PROJ_SUB) for r in range(0, tm, PROJ_SUB)]


def _kv_kernel(x_ref, g_ref, w_ref, c_ref, s1_ref, s2_ref, cmp_ref, ks_ref, vs_ref, kw_ref, vw_ref):
    gw = KV_GROUPS * HEAD_DIM
    chunks = _row_chunks(x_ref.shape[0])
    accs = [_dot(_rms(x_ref[rows, :], g_ref[...]).astype(BF16), w_ref[...]) for rows in chunks]
    ones_row = (lax.broadcasted_iota(jnp.int32, (V_ROWS - HEAD_DIM, PROJ_SUB), 0) == 0).astype(BF16)

    def put_k(ref, x, rows):
        c, s1, s2 = c_ref[rows, :], s1_ref[rows, :], s2_ref[rows, :]
        for p in range(gw // 128):
            xp = _rope128(x[:, 128 * p:128 * (p + 1)], c, s1, s2)
            ref[2 * p, rows, :] = xp[:, :HEAD_DIM].astype(BF16)
            ref[2 * p + 1, rows, :] = xp[:, HEAD_DIM:].astype(BF16)

    def put_vt(ref, x, rows):
        xt = x.T
        for g in range(KV_GROUPS):
            ref[g, :HEAD_DIM, rows] = xt[HEAD_DIM * g:HEAD_DIM * (g + 1)].astype(BF16)
            ref[g, HEAD_DIM:, rows] = ones_row

    for rows, acc in zip(chunks, accs):
        cmp_ref[rows, :] = acc[:, :2 * gw]
        put_k(ks_ref, acc[:, 2 * gw:3 * gw], rows)
        put_vt(vs_ref, acc[:, 3 * gw:4 * gw], rows)
        put_k(kw_ref, acc[:, 4 * gw:5 * gw], rows)
        put_vt(vw_ref, acc[:, 5 * gw:6 * gw], rows)


def _kv_proj(h, gain, w, rope_tabs):
    s, d = h.shape
    n = w.shape[1]
    gw = KV_GROUPS * HEAD_DIM
    tm = 2 * PROJ_SUB
    row = lambda i: (i, 0)
    fixed = lambda i: (0, 0)
    k_spec = pl.BlockSpec((KV_GROUPS, tm, HEAD_DIM), lambda i: (0, i, 0))
    k_shape = jax.ShapeDtypeStruct((KV_GROUPS, s, HEAD_DIM), BF16)
    vt_spec = pl.BlockSpec((KV_GROUPS, V_ROWS, tm), lambda i: (0, 0, i))
    vt_shape = jax.ShapeDtypeStruct((KV_GROUPS, V_ROWS, s), BF16)
    return pl.pallas_call(
        _kv_kernel,
        grid=(s // tm,),
        in_specs=[pl.BlockSpec((tm, d), row), pl.BlockSpec((1, d), fixed), pl.BlockSpec((d, n), fixed),
                  pl.BlockSpec((tm, 128), row), pl.BlockSpec((tm, 128), row), pl.BlockSpec((tm, 128), row)],
        out_specs=[pl.BlockSpec((tm, 2 * gw), row), k_spec, vt_spec, k_spec, vt_spec],
        out_shape=[jax.ShapeDtypeStruct((s, 2 * gw), F32), k_shape, vt_shape, k_shape, vt_shape],
        compiler_params=_params(("arbitrary",)),
        name="kv_proj",
    )(h, gain.reshape(1, d), w, *rope_tabs)


def _cmp_kernel(n_cmp, x_ref, pos_ref, w1_ref, w2_ref, o_ref, ot_ref):
    nch = x_ref.shape[0] // CMP_STRIDE
    half = CMP_STRIDE * HEAD_DIM
    w1 = w1_ref[...]
    bias = _dot(pos_ref[...].astype(BF16), w1)[0:1]
    acc = [[jnp.zeros((nch, CMP_HIDDEN), F32) for _ in range(2)] for _ in range(2)]
    for l in range(CMP_STRIDE):
        y = x_ref[pl.ds(l, nch, stride=CMP_STRIDE), :].astype(BF16)
        wa = w1[HEAD_DIM * l:HEAD_DIM * (l + 1)]
        wb = w1[half + HEAD_DIM * l:half + HEAD_DIM * (l + 1)]
        for gg in range(2):
            yg = y[:, HEAD_DIM * gg:HEAD_DIM * (gg + 1)]
            acc[gg][0] = acc[gg][0] + _dot(yg, wa)
            acc[gg][1] = acc[gg][1] + _dot(yg, wb)
    live = lax.broadcasted_iota(jnp.int32, (nch, HEAD_DIM), 0) < n_cmp
    ones_row = (lax.broadcasted_iota(jnp.int32, (V_ROWS - HEAD_DIM, nch), 0) == 0).astype(BF16)
    outs = []
    for gg in range(2):
        hid = acc[gg][0] + pltpu.roll(acc[gg][1], nch - 1, axis=0) + bias
        hid = jax.nn.gelu(hid, approximate=True).astype(BF16)
        outs.append(jnp.where(live, _dot(hid, w2_ref[...]), 0.0))
        o_ref[gg] = outs[gg].astype(BF16)
    out_t = jnp.concatenate(outs, axis=1).T
    for gg in range(2):
        ot_ref[gg, :HEAD_DIM, :] = out_t[HEAD_DIM * gg:HEAD_DIM * (gg + 1)].astype(BF16)
        ot_ref[gg, HEAD_DIM:, :] = ones_row


def _cmp_mlp(cmp_kv, pos_flat, w1, w2, n_cmp):
    s = cmp_kv.shape[0]
    nch = s // CMP_STRIDE
    feat = CMP_LEN * HEAD_DIM
    return pl.pallas_call(
        functools.partial(_cmp_kernel, n_cmp),
        grid=(2, KV_GROUPS // 2),
        in_specs=[pl.BlockSpec((s, 2 * HEAD_DIM), lambda c, p: (0, c * (KV_GROUPS // 2) + p)),
                  pl.BlockSpec((None, 8, feat), lambda c, p: (c, 0, 0)),
                  pl.BlockSpec((None, feat, CMP_HIDDEN), lambda c, p: (c, 0, 0)),
                  pl.BlockSpec((None, CMP_HIDDEN, HEAD_DIM), lambda c, p: (c, 0, 0))],
        out_specs=[pl.BlockSpec((None, 2, nch, HEAD_DIM), lambda c, p: (c, p, 0, 0)),
                   pl.BlockSpec((None, 2, V_ROWS, nch), lambda c, p: (c, p, 0, 0))],
        out_shape=[jax.ShapeDtypeStruct((2, KV_GROUPS, nch, HEAD_DIM), BF16),
                   jax.ShapeDtypeStruct((2, KV_GROUPS, V_ROWS, nch), BF16)],
        compiler_params=_params(("arbitrary", "arbitrary")),
        name="cmp_mlp",
    )(cmp_kv, pos_flat, w1, w2)


def _q_kernel(x_ref, g_ref, w_ref, b_ref, c_ref, s1_ref, s2_ref, q_ref, qr_ref, gate_ref):
    d = q_ref.shape[1]
    chunks = _row_chunks(x_ref.shape[0])
    accs = [_dot(_rms(x_ref[rows, :], g_ref[...]).astype(BF16), w_ref[...]) for rows in chunks]
    for rows, acc in zip(chunks, accs):
        c, s1, s2 = c_ref[rows, :], s1_ref[rows, :], s2_ref[rows, :]
        for p in range(d // 128):
            qp = acc[:, 128 * p:128 * (p + 1)] * (HEAD_DIM ** -0.5 * LOG2E)
            q_ref[rows, 128 * p:128 * (p + 1)] = qp.astype(BF16)
            qr_ref[rows, 128 * p:128 * (p + 1)] = _rope128(qp, c, s1, s2).astype(BF16)
        gates_t = (1.0 / (1.0 + jnp.exp(-(acc[:, d:] + b_ref[...])))).T
        for g in range(KV_GROUPS):
            gate_ref[g, :, rows] = gates_t[GATE_ROWS * g:GATE_ROWS * (g + 1)]


def _q_proj(h, gain, w, gate_b, rope_tabs):
    s, d = h.shape
    n = w.shape[1]
    tm = 2 * PROJ_SUB
    row = lambda i: (i, 0)
    fixed = lambda i: (0, 0)
    return pl.pallas_call(
        _q_kernel,
        grid=(s // tm,),
        in_specs=[pl.BlockSpec((tm, d), row), pl.BlockSpec((1, d), fixed), pl.BlockSpec((d, n), fixed),
                  pl.BlockSpec((1, 128), fixed),
                  pl.BlockSpec((tm, 128), row), pl.BlockSpec((tm, 128), row), pl.BlockSpec((tm, 128), row)],
        out_specs=[pl.BlockSpec((tm, d), row), pl.BlockSpec((tm, d), row),
                   pl.BlockSpec((KV_GROUPS, GATE_ROWS, tm), lambda i: (0, 0, i))],
        out_shape=[jax.ShapeDtypeStruct((s, d), BF16), jax.ShapeDtypeStruct((s, d), BF16),
                   jax.ShapeDtypeStruct((KV_GROUPS, GATE_ROWS, s), F32)],
        compiler_params=_params(("arbitrary",)),
        name="nsa_q_proj",
    )(h, gain.reshape(1, d), w, gate_b, *rope_tabs)


def _kvq_kernel(x_ref, gkv_ref, wkv_ref, gq_ref, wq_ref, b_ref, c_ref, s1_ref, s2_ref,
                cmp_ref, ks_ref, vs_ref, kw_ref, vw_ref, q_ref, qr_ref, gate_ref):
    _kv_kernel(x_ref, gkv_ref, wkv_ref, c_ref, s1_ref, s2_ref, cmp_ref, ks_ref, vs_ref, kw_ref, vw_ref)
    _q_kernel(x_ref, gq_ref, wq_ref, b_ref, c_ref, s1_ref, s2_ref, q_ref, qr_ref, gate_ref)


def _kvq_proj(h, kv_gain, w_kv, q_gain, w_q, gate_b, rope_tabs):
    s, d = h.shape
    gw = KV_GROUPS * HEAD_DIM
    tm = 2 * PROJ_SUB
    row = lambda i: (i, 0)
    fixed = lambda i: (0, 0)
    k_spec = pl.BlockSpec((KV_GROUPS, tm, HEAD_DIM), lambda i: (0, i, 0))
    k_shape = jax.ShapeDtypeStruct((KV_GROUPS, s, HEAD_DIM), BF16)
    vt_spec = pl.BlockSpec((KV_GROUPS, V_ROWS, tm), lambda i: (0, 0, i))
    vt_shape = jax.ShapeDtypeStruct((KV_GROUPS, V_ROWS, s), BF16)
    tab = pl.BlockSpec((tm, 128), row)
    return pl.pallas_call(
        _kvq_kernel,
        grid=(s // tm,),
        in_specs=[pl.BlockSpec((tm, d), row),
                  pl.BlockSpec((1, d), fixed), pl.BlockSpec(w_kv.shape, fixed),
                  pl.BlockSpec((1, d), fixed), pl.BlockSpec(w_q.shape, fixed), pl.BlockSpec((1, 128), fixed),
                  tab, tab, tab],
        out_specs=[pl.BlockSpec((tm, 2 * gw), row), k_spec, vt_spec, k_spec, vt_spec,
                   pl.BlockSpec((tm, d), row), pl.BlockSpec((tm, d), row),
                   pl.BlockSpec((KV_GROUPS, GATE_ROWS, tm), lambda i: (0, 0, i))],
        out_shape=[jax.ShapeDtypeStruct((s, 2 * gw), F32), k_shape, vt_shape, k_shape, vt_shape,
                   jax.ShapeDtypeStruct((s, d), BF16), jax.ShapeDtypeStruct((s, d), BF16),
                   jax.ShapeDtypeStruct((KV_GROUPS, GATE_ROWS, s), F32)],
        compiler_params=_params(("arbitrary",)),
        name="kvq_proj",
    )(h, kv_gain.reshape(1, d), w_kv, q_gain.reshape(1, d), w_q, gate_b, *rope_tabs)


def _group_gate_columns(a):
    lead = a.shape[:-1]
    a = a.reshape(lead + (3, KV_GROUPS, HEADS_PER_GROUP))
    a = jnp.moveaxis(a, -2, -3).reshape(lead + (KV_GROUPS, 3 * HEADS_PER_GROUP))
    a = jnp.pad(a, [(0, 0)] * len(lead) + [(0, 0), (0, GATE_ROWS - 3 * HEADS_PER_GROUP)])
    a = a.reshape(lead + (KV_GROUPS * GATE_ROWS,))
    return jnp.pad(a, [(0, 0)] * len(lead) + [(0, 128 - KV_GROUPS * GATE_ROWS)])


def _stack_heads_t(q):
    qt = q.astype(F32).T
    return jnp.concatenate([qt[HEAD_DIM * h:HEAD_DIM * (h + 1)] for h in range(HEADS_PER_GROUP)], axis=1).astype(BF16)


def _head_cols(x, h):
    return x[:, Q_TILE * h:Q_TILE * (h + 1)]


def _softmax_numerators_t(st, mask):
    es = []
    for h in range(HEADS_PER_GROUP):
        s = mask(_head_cols(st, h))
        es.append(jnp.exp2(s - jnp.max(s, axis=0, keepdims=True)))
    return es


def _nsa_kernel(tk, q_ref, qr_ref, kc_ref, vct_ref, ovt_ref, ks_ref, vst_ref, kw_ref, vwt_ref, wbias_ref, gate_ref,
                o_ref, ocmp_ref, owin_ref, bias_ref, s0_ref, s1_ref, mx0_ref, mx1_ref, m_ref, acc_ref):
    i = pl.program_id(1)
    tq = q_ref.shape[0]
    nc = kc_ref.shape[0]
    n_sel = ovt_ref.shape[0]
    hp = HEADS_PER_GROUP
    q4 = _stack_heads_t(q_ref[...])
    qr4 = _stack_heads_t(qr_ref[...])
    tok = i * tq + lax.broadcasted_iota(jnp.int32, (1, tq), 1)
    gates = gate_ref[...]

    q0 = pl.multiple_of(i * tq, tq)
    span = WINDOW + tq
    wstart = pl.multiple_of(jnp.maximum(i * tq - WINDOW, 0), tq)

    def cmp_and_select(ncv, rows):
        sc = _dot(kc_ref[:ncv, :], q4)
        sw = _dot(kw_ref[pl.ds(wstart, span), :], qr4)
        sd = _dot(ks_ref[pl.ds(q0, tq), :], qr4)
        s0_ref[...] = _dot(ks_ref[:tk, :], qr4)

        n_idx = lax.broadcasted_iota(jnp.int32, (ncv, tq), 0)
        last_complete = lax.shift_right_arithmetic(tok - (CMP_LEN - 1), CMP_STRIDE.bit_length() - 1)
        es = _softmax_numerators_t(sc, lambda s: jnp.where(n_idx <= last_complete, s, NEG))
        wbias = wbias_ref[...]
        ew = _softmax_numerators_t(sw, lambda s: s + wbias)
        not_future = lax.broadcasted_iota(jnp.int32, (tq, tq), 0) <= lax.broadcasted_iota(jnp.int32, (tq, tq), 1)
        sd = jnp.concatenate([jnp.where(not_future, _head_cols(sd, h), NEG) for h in range(hp)], axis=1)
        m0 = jnp.max(sd, axis=0, keepdims=True)
        pd = jnp.exp2(sd - m0)

        o_cmp = _dot(vct_ref[:, :ncv], jnp.concatenate(es, axis=1).astype(BF16))
        o_win = _dot(vwt_ref[:, pl.ds(wstart, span)], jnp.concatenate(ew, axis=1).astype(BF16))
        m_ref[...] = m0
        acc_ref[...] = _dot(vst_ref[:, pl.ds(q0, tq)], pd.astype(BF16))
        inv_c = jnp.where(jnp.concatenate([tok >= CMP_LEN - 1] * hp, axis=1), 1.0 / o_cmp[HEAD_DIM:HEAD_DIM + 1], 0.0)
        ocmp_ref[...] = o_cmp[:HEAD_DIM] * inv_c
        owin_ref[...] = o_win[:HEAD_DIM] * (1.0 / o_win[HEAD_DIM:HEAD_DIM + 1])

        psum = sum(es[h] * _head_cols(inv_c, h) for h in range(hp))
        ovt = ovt_ref[:rows, :ncv]
        imp = sum(_dot(ovt, term) for term in _split3(psum))

        m_idx = lax.broadcasted_iota(jnp.int32, (rows, tq), 0)
        blk_t = tok // SEL_LEN
        forced = (m_idx == 0) | (m_idx == blk_t) | (m_idx == blk_t - 1)
        candidate = jnp.logical_and(m_idx <= blk_t, jnp.logical_not(forced))
        imp = jnp.where(candidate, imp, -jnp.inf)

        def pick(_, v):
            mx = jnp.max(v, axis=0, keepdims=True)
            idx = jnp.min(jnp.where(v == mx, m_idx, rows), axis=0, keepdims=True)
            return jnp.where(m_idx == idx, -jnp.inf, v)

        picked = lax.fori_loop(0, SEL_TOPK - 3, pick, imp, unroll=True) == -jnp.inf
        before = m_idx < (i * tq) // SEL_LEN
        bias_ref[:rows, :] = jnp.where(jnp.logical_and(picked, before), 0.0, NEG)
        if rows < n_sel:
            bias_ref[rows:, :] = jnp.full((n_sel - rows, tq), NEG, F32)

    rows_per_chunk = CMP_CHUNK * CMP_STRIDE // SEL_LEN
    n_prefix = -(-nc // CMP_CHUNK)
    need = ((i + 1) * tq - 1) // (CMP_STRIDE * CMP_CHUNK) + 1
    for c in range(1, n_prefix + 1):
        pl.when(need == c)(functools.partial(cmp_and_select, min(c * CMP_CHUNK, nc), min(c * rows_per_chunk, n_sel)))
    bpt = tk // SEL_LEN

    def sel_bias_and_max(kt, st, s_ref, mx_ref):
        rows = bias_ref[pl.ds(pl.multiple_of(kt * bpt, bpt), bpt), :]
        bias = jnp.concatenate([jnp.broadcast_to(rows[j:j + 1], (SEL_LEN, tq)) for j in range(bpt)], axis=0)
        s = jnp.concatenate([_head_cols(st, h) + bias for h in range(hp)], axis=1)
        s_ref[...] = s
        mx_ref[...] = jnp.max(s, axis=0, keepdims=True)

    def sel_scores(kt, s_ref, mx_ref):
        start = pl.multiple_of(kt * tk, tk)
        sel_bias_and_max(kt, _dot(ks_ref[pl.ds(start, tk), :], qr4), s_ref, mx_ref)

    def sel_accumulate(kt, s_ref, mx_ref):
        start = pl.multiple_of(kt * tk, tk)
        m = m_ref[...]
        m_new = jnp.maximum(m, mx_ref[...])
        p = jnp.exp2(s_ref[...] - m_new)
        acc_ref[...] = jnp.exp2(m - m_new) * acc_ref[...] + _dot(vst_ref[:, pl.ds(start, tk)], p.astype(BF16))
        m_ref[...] = m_new

    last = jnp.maximum(i * tq - 1, 0) // tk
    final_tile = ks_ref.shape[0] // tk - 1

    def sel_pair(j):
        sel_scores(2 * j + 1, s1_ref, mx1_ref)
        sel_accumulate(2 * j, s0_ref, mx0_ref)
        sel_scores(jnp.minimum(2 * j + 2, final_tile), s0_ref, mx0_ref)
        sel_accumulate(2 * j + 1, s1_ref, mx1_ref)

    def sel_pairs(jj, carry):
        for u in range(SEL_UNROLL):
            sel_pair(SEL_UNROLL * jj + u)
        return carry

    sel_bias_and_max(0, s0_ref[...], s0_ref, mx0_ref)
    pairs = (last + 1) // 2
    lax.fori_loop(0, pairs // SEL_UNROLL, sel_pairs, 0)
    done = pairs // SEL_UNROLL * SEL_UNROLL
    run = SEL_UNROLL // 2
    while run >= 1:
        def sel_run(done=done, run=run):
            for u in range(run):
                sel_pair(done + u)

        take = (pairs - done) >= run
        pl.when(take)(sel_run)
        done = done + jnp.where(take, run, 0)
        run //= 2

    @pl.when(last % 2 == 0)
    def _():
        sel_accumulate(last, s0_ref, mx0_ref)

    acc = acc_ref[...]
    o_sel = acc[:HEAD_DIM] * (1.0 / acc[HEAD_DIM:HEAD_DIM + 1])

    o_cmp = ocmp_ref[...]
    o_win = owin_ref[...]
    for h in range(hp):
        mix = (gates[h:h + 1] * _head_cols(o_cmp, h)
               + gates[hp + h:hp + h + 1] * _head_cols(o_sel, h)
               + gates[2 * hp + h:2 * hp + h + 1] * _head_cols(o_win, h))
        o_ref[:, HEAD_DIM * h:HEAD_DIM * (h + 1)] = mix.T.astype(BF16)


def _nsa_attention(q, q_rot, k_cmp, v_cmp_t, overlap_t, k_slc, v_slc_t, k_win, v_win_t, gates):
    s, d = q.shape
    nc = k_cmp.shape[1]
    n_sel = overlap_t.shape[0]
    gw = HEADS_PER_GROUP * HEAD_DIM
    cols = HEADS_PER_GROUP * Q_TILE
    tk = 512
    n_wcase = WINDOW // Q_TILE + 1
    lead = (jnp.arange(n_wcase) * Q_TILE)[:, None, None]
    dist = lead + jnp.arange(Q_TILE)[None, None, :] - jnp.arange(WINDOW + Q_TILE)[None, :, None]
    wbias = jnp.where((dist >= 0) & (dist < WINDOW), 0.0, NEG).astype(F32)
    qspec = pl.BlockSpec((Q_TILE, gw), lambda g, i: (i, g))
    per_group = lambda shape: pl.BlockSpec((None,) + shape, lambda g, i: (g, 0, 0))
    return pl.pallas_call(
        functools.partial(_nsa_kernel, tk),
        grid=(KV_GROUPS, s // Q_TILE),
        in_specs=[qspec, qspec,
                  per_group((nc, HEAD_DIM)), per_group((V_ROWS, nc)),
                  pl.BlockSpec((n_sel, nc), lambda g, i: (0, 0)),
                  per_group((s, HEAD_DIM)), per_group((V_ROWS, s)),
                  per_group((s, HEAD_DIM)), per_group((V_ROWS, s)),
                  pl.BlockSpec((None, WINDOW + Q_TILE, Q_TILE), lambda g, i: (jnp.minimum(i, n_wcase - 1), 0, 0)),
                  pl.BlockSpec((None, GATE_ROWS, Q_TILE), lambda g, i: (g, 0, i))],
        out_specs=qspec,
        out_shape=jax.ShapeDtypeStruct((s, d), BF16),
        scratch_shapes=[pltpu.VMEM((HEAD_DIM, cols), F32), pltpu.VMEM((HEAD_DIM, cols), F32),
                        pltpu.VMEM((n_sel, Q_TILE), F32),
                        pltpu.VMEM((tk, cols), F32), pltpu.VMEM((tk, cols), F32),
                        pltpu.VMEM((1, cols), F32), pltpu.VMEM((1, cols), F32),
                        pltpu.VMEM((1, cols), F32), pltpu.VMEM((V_ROWS, cols), F32)],
        compiler_params=_params(("arbitrary", "arbitrary")),
        name="nsa_attention",
    )(q, q_rot, k_cmp, v_cmp_t, overlap_t, k_slc, v_slc_t, k_win, v_win_t, wbias, gates)


def _rope_tables(s):
    half = ROT_DIM // 2
    inv_freq = (np.float32(ROPE_THETA) ** (-np.arange(half, dtype=np.float32) * np.float32(2.0) / ROT_DIM))
    ang = np.arange(s, dtype=np.float32)[:, None] * inv_freq.astype(np.float32)[None, :]
    cos, sin = np.cos(ang).astype(np.float32), np.sin(ang).astype(np.float32)
    rest = HEAD_DIM - ROT_DIM
    c = np.concatenate([cos, cos, np.ones((s, rest), np.float32)], axis=1)
    s1 = np.concatenate([-sin, np.zeros((s, half + rest), np.float32)], axis=1)
    s2 = np.concatenate([np.zeros((s, half), np.float32), sin, np.zeros((s, rest), np.float32)], axis=1)
    return tuple(jnp.asarray(np.tile(a, (1, 2))) for a in (c, s1, s2))


def _overlap_matrix(nch, n_sel):
    cmp_start = jnp.arange(nch)[:, None] * CMP_STRIDE
    sel_start = jnp.arange(n_sel)[None, :] * SEL_LEN
    return ((cmp_start < sel_start + SEL_LEN) & (cmp_start + CMP_LEN > sel_start)).astype(BF16)


def kernel(x, norm_gain, sb_w_qkv, sb_w_o, kv_norm, nsa_w_kv, cmp_pos, cmp_w1, cmp_w2,
           nsa_w_q, nsa_gate_b, nsa_w_o, mlp_w1, mlp_w2, final_norm):
    b, s, d = x.shape
    assert b == 1 and d == N_HEADS * HEAD_DIM
    assert s % 512 == 0 and s >= WINDOW + Q_TILE and s // SEL_LEN >= SEL_TOPK
    n_cmp = (s - CMP_LEN) // CMP_STRIDE + 1
    nch = s // CMP_STRIDE
    n_sel = s // SEL_LEN
    h0 = x[0]
    rope_tabs = _rope_tables(s)

    o_sb = _sb_attention(*_qkv_proj(h0, norm_gain[0, 0], sb_w_qkv[0].astype(BF16)))
    h1 = _attn_out_mlp(o_sb, h0, sb_w_o[0].astype(BF16), norm_gain[0, 1],
                       mlp_w1[0].astype(BF16), mlp_w2[0].astype(BF16))

    n_qk = N_HEADS * HEAD_DIM
    w_q = jnp.concatenate([nsa_w_q[0][:, :n_qk], _group_gate_columns(nsa_w_q[0][:, n_qk:])], axis=1).astype(BF16)
    gate_b = _group_gate_columns(nsa_gate_b[0]).reshape(1, 128)
    cmp_kv, k_slc, v_slc_t, k_win, v_win_t, q, q_rot, gates = _kvq_proj(
        h1, kv_norm, nsa_w_kv.astype(BF16), norm_gain[1, 0], w_q, gate_b, rope_tabs)
    pos_flat = jnp.broadcast_to(cmp_pos.reshape(2, 1, CMP_LEN * HEAD_DIM), (2, 8, CMP_LEN * HEAD_DIM))
    kv_cmp, kv_cmp_t = _cmp_mlp(cmp_kv, pos_flat, cmp_w1.astype(BF16), cmp_w2.astype(BF16), n_cmp)

    o_nsa = _nsa_attention(q, q_rot, kv_cmp[0], kv_cmp_t[1], _overlap_matrix(nch, n_sel).T,
                           k_slc, v_slc_t, k_win, v_win_t, gates)
    out = _attn_out_mlp(o_nsa, h1, nsa_w_o[0].astype(BF16), norm_gain[1, 1],
                        mlp_w1[1].astype(BF16), mlp_w2[1].astype(BF16), final_gain=final_norm)
    return out[None]
```
